```python
import math
import jax
import jax.numpy as jnp
from jax import lax
import numpy as np

D_MODEL = 1024
BATCH = 1
SEQ = 16384
DEPTH = 2

GRID_W = 64
CTX_LEN = 256
N_EVEN = (DEPTH + 1) // 2
N_ODD = DEPTH // 2
N_MOD = 6
EPS = 1e-6

ATTN_HEADS = 4
ATTN_KV_HEADS = 2
ATTN_GROUP = ATTN_HEADS // ATTN_KV_HEADS
HEAD_DIM = 128
ROPE_THETA = 10000.0
ROPE_PAIRS = HEAD_DIM // 4
Q_BLOCK = 128

MLSTM_HEADS = 4
MLSTM_DK = 64
MLSTM_DV = 128
MLSTM_CHUNK = 128
N_DIRS = 2

A_Q = ATTN_HEADS * HEAD_DIM
A_KV = ATTN_KV_HEADS * HEAD_DIM
B_QK = MLSTM_HEADS * MLSTM_DK
B_V = MLSTM_HEADS * MLSTM_DV
B_G = N_DIRS * MLSTM_HEADS
EVEN_SPLITS = (A_Q, A_KV, A_KV, B_QK, B_QK, B_V, B_G, B_G, B_V)
EVEN_IN = A_Q + 2 * A_KV + 2 * B_QK + 2 * B_V + 2 * B_G
MIX_WIDTH = A_Q + B_V

HYENA_WIDTH = D_MODEL
SHORT_CONV = 3
FILTER_EMB = 33
FILTER_BANDS = (FILTER_EMB - 1) // 2
FILTER_HIDDEN = 64
FILTER_OUT_SCALE = 0.02
DECAY_TARGET = 1e-2
FAST_DECAY = 0.3
SLOW_DECAY = 1.5

N_EXPERTS = 32
TOP_K = 4
D_EXPERT = D_MODEL
SWIGLU_LIMIT = 7.0
SWIGLU_ALPHA = 1.702
MOE_BLOCK = 256

kernel_name = 'hybrid_gqa_mlstm_hyena_moe_dit'


def rmsnorm(x, gain):
    xf = x.astype(jnp.float32)
    y = xf * lax.rsqrt(jnp.mean(xf * xf, axis=-1, keepdims=True) + EPS)
    return y.astype(x.dtype) * gain


def modulate(x, shift, scale):
    return x * (1.0 + scale) + shift


def rope_tables(rows):
    row = jnp.repeat(jnp.arange(rows, dtype=jnp.int32), GRID_W)
    col = jnp.tile(jnp.arange(GRID_W, dtype=jnp.int32), rows)
    pos = jnp.stack([row, col], axis=-1).astype(jnp.float32)
    inv_freq = ROPE_THETA ** (-jnp.arange(ROPE_PAIRS, dtype=jnp.float32) / ROPE_PAIRS)
    ang = pos[:, :, None] * inv_freq
    return jnp.cos(ang), jnp.sin(ang)


def apply_rope(x, cos, sin):
    xs = x.astype(jnp.float32).reshape(*x.shape[:-1], 2, 2, ROPE_PAIRS)
    x1, x2 = xs[..., 0, :], xs[..., 1, :]
    c = cos[None, :, None]
    s = sin[None, :, None]
    out = jnp.stack([x1 * c - x2 * s, x2 * c + x1 * s], axis=-2)
    return out.reshape(x.shape).astype(x.dtype)


def attend(q, k, v):
    s = jnp.einsum('bqhgd,bkhd->bhgqk', q, k, preferred_element_type=jnp.float32) * (HEAD_DIM ** -0.5)
    p = jax.nn.softmax(s, axis=-1).astype(v.dtype)
    return jnp.einsum('bhgqk,bkhd->bqhgd', p, v)


def mlstm_chunkwise(q, k, v, log_i, log_f, state, return_h):
    L = q.shape[-2]
    nc = L // MLSTM_CHUNK
    n_lead = q.ndim - 2

    def to_chunks(a):
        lead = a.shape[:n_lead]
        a = a.reshape(*lead, nc, MLSTM_CHUNK, *a.shape[n_lead + 1:])
        return jnp.moveaxis(a, n_lead, 0)

    tril = jnp.tril(jnp.ones((MLSTM_CHUNK, MLSTM_CHUNK), dtype=bool))

    def step(carry, inp):
        C, n, m = carry
        qc, kc, vc, ic, fc = inp
        b = jnp.cumsum(fc, axis=-1)
        log_d = jnp.where(tril, b[..., :, None] - b[..., None, :] + ic[..., None, :], -jnp.inf)
        inter = b + m[..., None]
        m_row = jnp.maximum(inter, jnp.max(log_d, axis=-1))
        w_inter = jnp.exp(inter - m_row)
        s = jnp.einsum('...td,...sd->...ts', qc, kc) * jnp.exp(log_d - m_row[..., None])
        num = jnp.einsum('...ts,...sv->...tv', s, vc) + w_inter[..., None] * jnp.einsum('...vd,...td->...tv', C, qc)
        den = jnp.sum(s, axis=-1) + w_inter * jnp.einsum('...d,...td->...t', n, qc)
        h = num / jnp.maximum(jnp.abs(den), jnp.exp(-m_row))[..., None]
        b_last = b[..., -1]
        log_w = b_last[..., None] - b + ic
        m_new = jnp.maximum(b_last + m, jnp.max(log_w, axis=-1))
        w = jnp.exp(log_w - m_new[..., None])
        decay = jnp.exp(b_last + m - m_new)
        C_new = decay[..., None, None] * C + jnp.einsum('...s,...sv,...sd->...vd', w, vc, kc)
        n_new = decay[..., None] * n + jnp.einsum('...s,...sd->...d', w, kc)
        return (C_new, n_new, m_new), (h if return_h else None)

    xs = (to_chunks(q), to_chunks(k), to_chunks(v), to_chunks(log_i), to_chunks(log_f))
    new_state, hs = lax.scan(step, state, xs)
    if not return_h:
        return None, new_state
    hs = jnp.moveaxis(hs, 0, -3)
    return hs.reshape(*hs.shape[:-3], L, hs.shape[-1]), new_state


def mlstm_bidir(q, k, v, ig, fg, f_bias, state, return_h):
    def dirs(a):
        a = jnp.moveaxis(a.astype(jnp.float32), 1, 2)
        return jnp.stack([a, jnp.flip(a, axis=2)], axis=0)

    def gate_dirs(g):
        g = jnp.transpose(g.astype(jnp.float32), (2, 0, 3, 1))
        return jnp.stack([g[0], jnp.flip(g[1], axis=-1)], axis=0)

    log_i = gate_dirs(ig)
    log_f = jax.nn.log_sigmoid(gate_dirs(fg) + f_bias.astype(jnp.float32)[:, None, :, None])
    h, new_state = mlstm_chunkwise(dirs(q) * (MLSTM_DK ** -0.5), dirs(k), dirs(v), log_i, log_f, state, return_h)
    if not return_h:
        return None, new_state
    h = h[0] + jnp.flip(h[1], axis=2)
    return jnp.moveaxis(h, 1, 2).astype(v.dtype), new_state


def even_mixer(a_lat, a_ctx, cos, sin, w_in, b_in, q_gain, k_gain, f_bias, h_gain, w_out, b_out, ctx_out):
    B, L, _ = a_lat.shape
    n_ctx = a_ctx.shape[1]
    offsets = np.cumsum(EVEN_SPLITS)[:-1].tolist()
    pl = jnp.split(a_lat @ w_in + b_in, offsets, axis=-1)
    pc = jnp.split(a_ctx @ w_in + b_in, offsets, axis=-1)

    def attn_heads(parts, rope):
        n = parts[0].shape[1]
        q = rmsnorm(parts[0].reshape(B, n, ATTN_HEADS, HEAD_DIM), q_gain)
        k = rmsnorm(parts[1].reshape(B, n, ATTN_KV_HEADS, HEAD_DIM), k_gain)
        if rope:
            q = apply_rope(q, cos, sin)
            k = apply_rope(k, cos, sin)
        return (q.reshape(B, n, ATTN_KV_HEADS, ATTN_GROUP, HEAD_DIM), k,
                parts[2].reshape(B, n, ATTN_KV_HEADS, HEAD_DIM))

    def mlstm_heads(parts):
        n = parts[3].shape[1]
        return (parts[3].reshape(B, n, MLSTM_HEADS, MLSTM_DK), parts[4].reshape(B, n, MLSTM_HEADS, MLSTM_DK),
                parts[5].reshape(B, n, MLSTM_HEADS, MLSTM_DV), parts[6].reshape(B, n, N_DIRS, MLSTM_HEADS),
                parts[7].reshape(B, n, N_DIRS, MLSTM_HEADS))

    def mlstm_out(h, og):
        n = h.shape[1]
        return jax.nn.sigmoid(og) * rmsnorm(h, h_gain.reshape(MLSTM_HEADS, MLSTM_DV)).reshape(B, n, B_V)

    ql, kl, vl = attn_heads(pl, True)
    qc, kc, vc = attn_heads(pc, False)
    k_all = jnp.concatenate([kl, kc], axis=1)
    v_all = jnp.concatenate([vl, vc], axis=1)
    qb = jnp.moveaxis(ql.reshape(B, L // Q_BLOCK, Q_BLOCK, ATTN_KV_HEADS, ATTN_GROUP, HEAD_DIM), 1, 0)
    att_l = lax.map(lambda qq: attend(qq, k_all, v_all), qb)
    att_l = jnp.moveaxis(att_l, 0, 1).reshape(B, L, A_Q)

    state0 = (jnp.zeros((N_DIRS, B, MLSTM_HEADS, MLSTM_DV, MLSTM_DK), jnp.float32),
              jnp.zeros((N_DIRS, B, MLSTM_HEADS, MLSTM_DK), jnp.float32),
              jnp.zeros((N_DIRS, B, MLSTM_HEADS), jnp.float32))
    h_c, st_ctx = mlstm_bidir(*mlstm_heads(pc), f_bias, state0, ctx_out)
    h_l, _ = mlstm_bidir(*mlstm_heads(pl), f_bias, st_ctx, True)

    y_lat = jnp.concatenate([att_l, mlstm_out(h_l, pl[8])], axis=-1) @ w_out + b_out
    if not ctx_out:
        return y_lat, None
    att_c = attend(qc, kc, vc).reshape(B, n_ctx, A_Q)
    y_ctx = jnp.concatenate([att_c, mlstm_out(h_c, pc[8])], axis=-1) @ w_out + b_out
    return y_lat, y_ctx


def short_conv(z, w, b):
    ch = z.shape[-1]
    y = lax.conv_general_dilated(z, w[:, None, :].astype(z.dtype), window_strides=(1,), padding=((1, 1),),
                                 dimension_numbers=('NWC', 'WIO', 'NWC'), feature_group_count=ch)
    return y + b


def hyena_filters(n, w1, b1, freq, w2, b2, w3, b3, w4):
    f32 = jnp.float32
    t = jnp.linspace(0.0, 1.0, n, dtype=f32)[:, None]
    w = (2.0 * math.pi / n) * jnp.arange(n, dtype=f32)[:, None]
    f = jnp.linspace(1e-4, FILTER_BANDS - 1, FILTER_BANDS, dtype=f32)
    z = jnp.concatenate([t, jnp.cos(f * w), -jnp.sin(f * w)], axis=-1)
    fr = freq.astype(f32)
    h = jnp.sin(fr * (z @ w1.astype(f32) + b1.astype(f32)))
    h = jnp.sin(fr * (h @ w2.astype(f32) + b2.astype(f32)))
    h = jnp.sin(fr * (h @ w3.astype(f32) + b3.astype(f32)))
    h = h @ w4.astype(f32)
    deltas = jnp.abs(jnp.linspace(math.log(DECAY_TARGET) / SLOW_DECAY, math.log(DECAY_TARGET) / FAST_DECAY,
                                  HYENA_WIDTH, dtype=f32))
    decay = jnp.exp(-t * deltas)
    return h[:, :HYENA_WIDTH] * decay, h[:, HYENA_WIDTH:] * decay


def long_conv(v, h_fwd, h_bwd):
    L, ch = h_fwd.shape
    taps = jnp.concatenate([h_fwd, jnp.zeros((1, ch), h_fwd.dtype), h_bwd[:0:-1]], axis=0)
    vf = jnp.fft.rfft(v.astype(jnp.float32), n=2 * L, axis=1)
    y = jnp.fft.irfft(vf * jnp.fft.rfft(taps, axis=0)[None], n=2 * L, axis=1)[:, :L]
    return y.astype(v.dtype)


def hyena_mixer(u, w_in, b_in, conv_w, conv_b, fw1, fb1, ffreq, fw2, fb2, fw3, fb3, fw4, skip, w_out, b_out):
    n = u.shape[1]
    z = short_conv(u @ w_in + b_in, conv_w, conv_b)
    x0, x1, v = jnp.split(z, 3, axis=-1)
    h_fwd, h_bwd = hyena_filters(n, fw1, fb1, ffreq, fw2, fb2, fw3, fb3, fw4)
    v = v * x1
    v = long_conv(v, h_fwd, h_bwd) + v * skip
    return (v * x0) @ w_out + b_out


def moe_ffn(xt, router_w, router_b, w_gu, b_gu, w_down, b_down):
    T, D = xt.shape
    M = T * TOP_K
    n_blocks = -(-M // MOE_BLOCK) + N_EXPERTS
    logits = (xt @ router_w + router_b).astype(jnp.float32)
    top_logit, top_e = lax.top_k(logits, TOP_K)
    gate = jax.nn.softmax(top_logit, axis=-1)
    e_flat = top_e.reshape(M)
    order = jnp.argsort(e_flat)
    e_sorted = e_flat[order]
    tok_sorted = order // TOP_K
    g_sorted = gate.reshape(M)[order].astype(xt.dtype)
    counts = jnp.bincount(e_flat, length=N_EXPERTS)
    padded = (counts + MOE_BLOCK - 1) // MOE_BLOCK * MOE_BLOCK
    starts = jnp.cumsum(counts) - counts
    pad_ends = jnp.cumsum(padded)
    pad_starts = pad_ends - padded
    dest = pad_starts[e_sorted] + jnp.arange(M) - starts[e_sorted]
    buf = jnp.zeros((n_blocks * MOE_BLOCK, D), xt.dtype).at[dest].set(xt[tok_sorted])
    blk_e = jnp.minimum(jnp.searchsorted(pad_ends, jnp.arange(n_blocks) * MOE_BLOCK, side='right'), N_EXPERTS - 1)

    def expert_block(args):
        rows, e = args
        gu = rows @ w_gu[e] + b_gu[e]
        g = jnp.minimum(gu[:, :D_EXPERT], SWIGLU_LIMIT)
        u = jnp.clip(gu[:, D_EXPERT:], -SWIGLU_LIMIT, SWIGLU_LIMIT)
        return (g * jax.nn.sigmoid(SWIGLU_ALPHA * g) * (u + 1.0)) @ w_down[e] + b_down[e]

    out = lax.map(expert_block, (buf.reshape(n_blocks, MOE_BLOCK, D), blk_e)).reshape(-1, D)
    return jax.ops.segment_sum(out[dest] * g_sorted[:, None], tok_sorted, num_segments=T)


def setup_inputs(seed: int = 0) -> dict:
    key = jax.random.key(seed)
    keys = jax.random.split(key, 48)
    counter = [0]
    f32 = jnp.float32
    D = D_MODEL

    def nrm(shape, scale):
        k = keys[counter[0]]
        counter[0] += 1
        return jax.random.normal(k, shape, f32) * scale

    def gain(shape):
        return 1.0 + nrm(shape, 0.02)

    return {
        'x': nrm((BATCH, SEQ, D), 1.0),
        'c': nrm((BATCH, D), 1.0),
        'ctx': nrm((BATCH, CTX_LEN, D), 1.0),
        'c_ctx': nrm((D,), 1.0),
        'ada_w': nrm((DEPTH, D, N_MOD * D), 0.5 * D ** -0.5),
        'ada_b': nrm((DEPTH, N_MOD * D), 0.02),
        'norm1_g': gain((DEPTH, D)),
        'norm2_g': gain((DEPTH, D)),
        'ev_w_in': nrm((N_EVEN, D, EVEN_IN), D ** -0.5),
        'ev_b_in': nrm((N_EVEN, EVEN_IN), 0.02),
        'ev_q_gain': gain((N_EVEN, HEAD_DIM)),
        'ev_k_gain': gain((N_EVEN, HEAD_DIM)),
        'ev_f_bias': jnp.linspace(3.0, 6.0, MLSTM_HEADS, dtype=f32)[None, None, :] + nrm((N_EVEN, N_DIRS, MLSTM_HEADS), 0.1),
        'ev_h_gain': gain((N_EVEN, B_V)),
        'ev_w_out': nrm((N_EVEN, MIX_WIDTH, D), MIX_WIDTH ** -0.5),
        'ev_b_out': nrm((N_EVEN, D), 0.02),
        'od_w_in': nrm((N_ODD, D, 3 * HYENA_WIDTH), D ** -0.5),
        'od_b_in': nrm((N_ODD, 3 * HYENA_WIDTH), 0.02),
        'od_conv_w': nrm((N_ODD, SHORT_CONV, 3 * HYENA_WIDTH), SHORT_CONV ** -0.5),
        'od_conv_b': nrm((N_ODD, 3 * HYENA_WIDTH), 0.02),
        'od_filt_w1': nrm((N_ODD, FILTER_EMB, FILTER_HIDDEN), FILTER_EMB ** -0.5),
        'od_filt_b1': nrm((N_ODD, FILTER_HIDDEN), 0.1),
        'od_filt_freq': gain((N_ODD, FILTER_HIDDEN)),
        'od_filt_w2': nrm((N_ODD, FILTER_HIDDEN, FILTER_HIDDEN), FILTER_HIDDEN ** -0.5),
        'od_filt_b2': nrm((N_ODD, FILTER_HIDDEN), 0.1),
        'od_filt_w3': nrm((N_ODD, FILTER_HIDDEN, FILTER_HIDDEN), FILTER_HIDDEN ** -0.5),
        'od_filt_b3': nrm((N_ODD, FILTER_HIDDEN), 0.1),
        'od_filt_w4': nrm((N_ODD, FILTER_HIDDEN, 2 * HYENA_WIDTH), FILTER_OUT_SCALE * FILTER_HIDDEN ** -0.5),
        'od_skip': nrm((N_ODD, HYENA_WIDTH), 1.0),
        'od_w_out': nrm((N_ODD, HYENA_WIDTH, D), HYENA_WIDTH ** -0.5),
        'od_b_out': nrm((N_ODD, D), 0.02),
        'router_w': nrm((DEPTH, D, N_EXPERTS), D ** -0.5),
        'router_b': nrm((DEPTH, N_EXPERTS), 0.01),
        'moe_w_gu': nrm((DEPTH, N_EXPERTS, D, 2 * D_EXPERT), D ** -0.5),
        'moe_b_gu': nrm((DEPTH, N_EXPERTS, 2 * D_EXPERT), 0.02),
        'moe_w_down': nrm((DEPTH, N_EXPERTS, D_EXPERT, D), D_EXPERT ** -0.5),
        'moe_b_down': nrm((DEPTH, N_EXPERTS, D), 0.02),
        'final_g': gain((D,)),
    }


def reference(x, c, ctx, c_ctx, ada_w, ada_b, norm1_g, norm2_g, ev_w_in, ev_b_in, ev_q_gain, ev_k_gain,
              ev_f_bias, ev_h_gain, ev_w_out, ev_b_out, od_w_in, od_b_in, od_conv_w, od_conv_b, od_filt_w1,
              od_filt_b1, od_filt_freq, od_filt_w2, od_filt_b2, od_filt_w3, od_filt_b3, od_filt_w4, od_skip,
              od_w_out, od_b_out, router_w, router_b, moe_w_gu, moe_b_gu, moe_w_down, moe_b_down, final_g):
    B, L, D = x.shape
    n_ctx = ctx.shape[1]
    rows = L // GRID_W
    cos, sin = rope_tables(rows)
    h, hc = x, ctx
    for i in range(DEPTH):
        j = i // 2
        is_even = i % 2 == 0
        ctx_out = any(k % 2 == 0 for k in range(i + 1, DEPTH))
        need_ctx_in = is_even or ctx_out
        mod = jnp.split((jax.nn.silu(c) @ ada_w[i] + ada_b[i])[:, None, :], N_MOD, axis=-1)
        a = modulate(rmsnorm(h, norm1_g[i]), mod[0], mod[1])
        if need_ctx_in:
            cmod = jnp.split(jax.nn.silu(c_ctx) @ ada_w[i] + ada_b[i], N_MOD, axis=-1)
            ac = modulate(rmsnorm(hc, norm1_g[i]), cmod[0], cmod[1])
        if is_even:
            y, yc = even_mixer(a, ac, cos, sin, ev_w_in[j], ev_b_in[j], ev_q_gain[j], ev_k_gain[j], ev_f_bias[j],
                               ev_h_gain[j], ev_w_out[j], ev_b_out[j], ctx_out)
        else:
            hy = (od_w_in[j], od_b_in[j], od_conv_w[j], od_conv_b[j], od_filt_w1[j], od_filt_b1[j],
                  od_filt_freq[j], od_filt_w2[j], od_filt_b2[j], od_filt_w3[j], od_filt_b3[j], od_filt_w4[j],
                  od_skip[j], od_w_out[j], od_b_out[j])
            y = hyena_mixer(a, *hy)
            yc = hyena_mixer(ac, *hy) if ctx_out else None
        h = h + mod[2] * y
        tokens = [modulate(rmsnorm(h, norm2_g[i]), mod[3], mod[4]).reshape(B * L, D)]
        if ctx_out:
            hc = hc + cmod[2] * yc
            tokens.append(modulate(rmsnorm(hc, norm2_g[i]), cmod[3], cmod[4]).reshape(B * n_ctx, D))
        f = moe_ffn(jnp.concatenate(tokens, axis=0), router_w[i], router_b[i], moe_w_gu[i], moe_b_gu[i],
                    moe_w_down[i], moe_b_down[i])
        h = h + mod[5] * f[:B * L].reshape(B, L, D)
        if ctx_out:
            hc = hc + cmod[5] * f[B * L:].reshape(B, n_ctx, D)
    return rmsnorm(h, final_g)
```

```python
import functools
import math

import numpy as np
import jax
import jax.numpy as jnp
from jax import lax
from jax.experimental import pallas as pl
from jax.experimental.pallas import tpu as pltpu

f32 = jnp.float32
bf16 = jnp.bfloat16
i32 = jnp.int32
HIGHEST = lax.Precision.HIGHEST

EPS = 1e-6
GRID_W = 64
HEAD_DIM = 128
ATTN_HEADS = 4
ATTN_KV_HEADS = 2
ROPE_THETA = 10000.0
ROPE_PAIRS = HEAD_DIM // 4
MLSTM_HEADS = 4
MLSTM_DK = 64
MLSTM_DV = 128
CHUNK = 128
N_EXPERTS = 32
TOP_K = 4
SWIGLU_LIMIT = 7.0
SWIGLU_ALPHA = 1.702
FILTER_EMB = 33
FILTER_BANDS = 16
DECAY_TARGET = 1e-2
FAST_DECAY = 0.3
SLOW_DECAY = 1.5

LANES = 128
VMEM_LIMIT = 56 * 1024 * 1024
MOE_BM = 256


def _cp(sem, vmem=VMEM_LIMIT):
    return pltpu.CompilerParams(dimension_semantics=sem, vmem_limit_bytes=vmem)


def _full(shape):
    n = len(shape)
    return pl.BlockSpec(shape, lambda *a, _n=n: (0,) * _n)


def _rms(x, g):
    return x * lax.rsqrt(jnp.mean(x * x, axis=-1, keepdims=True) + EPS) * g


def _log_sigmoid(x):
    return jnp.minimum(x, 0.0) - jnp.log(1.0 + jnp.exp(-jnp.abs(x)))


def _dot(a, b):
    return jnp.dot(a, b, preferred_element_type=f32)


def _dot_nt(a, b, precision=None):
    return lax.dot_general(a, b, (((1,), (1,)), ((), ())), preferred_element_type=f32, precision=precision)


def _dot_tn(a, b):
    return lax.dot_general(a, b, (((0,), (0,)), ((), ())), preferred_element_type=f32)


def _mods_body(cv_ref, w_ref, b_ref, o_ref):
    cv = cv_ref[...]
    s = cv * jax.nn.sigmoid(cv)
    o_ref[0] = jnp.dot(s, w_ref[0], precision=HIGHEST, preferred_element_type=f32) + b_ref[0]


def _mods(cv, ada_w, ada_b):
    depth, d, nm = ada_w.shape
    tn = 1024
    return pl.pallas_call(
        _mods_body,
        grid=(depth, nm // tn),
        in_specs=[_full((8, d)),
                  pl.BlockSpec((1, d, tn), lambda i, j: (i, 0, j)),
                  pl.BlockSpec((1, 1, tn), lambda i, j: (i, 0, j))],
        out_specs=pl.BlockSpec((1, 8, tn), lambda i, j: (i, 0, j)),
        out_shape=jax.ShapeDtypeStruct((depth, 8, nm), f32),
        compiler_params=_cp(("arbitrary", "arbitrary")),
        name="mods",
    )(cv, ada_w, ada_b.reshape(depth, 1, nm))


def _ev_in_body(x_ref, mod_ref, g1_ref, wm_ref, bm_ref, wkT_ref, bkT_ref, wg_ref, bg_ref, wgT_ref, bgT_ref,
                qg_ref, kg_ref, cos_ref, sin_ref,
                q_ref, k_ref, v_ref, mq_ref, mkT_ref, mv_ref, og_ref, gc_ref, gT_ref, *, mrow, d):
    x = x_ref[...]
    shift = mod_ref[mrow:mrow + 1, 0:d]
    scale = mod_ref[mrow:mrow + 1, d:2 * d]
    a = (_rms(x, g1_ref[...]) * (1.0 + scale) + shift).astype(bf16)
    z = _dot(a, wm_ref[...]) + bm_ref[...]
    cos = cos_ref[...]
    sin = sin_ref[...]
    lane = lax.broadcasted_iota(i32, cos.shape, 1)
    first_half = (lane % 64) < 32

    def qk_head(zh, gain, mult):
        zn = zh * lax.rsqrt(jnp.mean(zh * zh, axis=-1, keepdims=True) + EPS) * gain
        partner = jnp.where(first_half, pltpu.roll(zn, 96, 1), pltpu.roll(zn, 32, 1))
        return ((zn * cos + partner * sin) * mult).astype(bf16)

    for h in range(ATTN_HEADS):
        q_ref[:, h * 128:(h + 1) * 128] = qk_head(z[:, h * 128:(h + 1) * 128], qg_ref[...], HEAD_DIM ** -0.5)
    for h in range(ATTN_KV_HEADS):
        k_ref[:, h * 128:(h + 1) * 128] = qk_head(z[:, 512 + h * 128:512 + (h + 1) * 128], kg_ref[...], 1.0)
    v_ref[...] = z[:, 768:1024].astype(bf16)
    mq_ref[...] = (z[:, 1024:1280] * (MLSTM_DK ** -0.5)).astype(bf16)
    mv_ref[...] = z[:, 1280:1792].astype(bf16)
    og_ref[...] = z[:, 1792:2304]
    mkT_ref[...] = (_dot_nt(wkT_ref[...], a) + bkT_ref[...]).astype(bf16)
    gc_ref[...] = _dot(a, wg_ref[...]) + bg_ref[...]
    gT_ref[...] = _dot_nt(wgT_ref[...], a) + bgT_ref[...]


def _ev_in(x, mod, g1, wts, cos, sin, mrow, tm):
    r, d = x.shape
    wm, bm, wkT, bkT, wg, bg, wgT, bgT, qg, kg = wts
    row = lambda n: pl.BlockSpec((tm, n), lambda i: (i, 0))
    col = lambda n: pl.BlockSpec((n, tm), lambda i: (0, i))
    outs = [((r, 512), bf16, row(512)), ((r, 256), bf16, row(256)), ((r, 256), bf16, row(256)),
            ((r, 256), bf16, row(256)), ((256, r), bf16, col(256)), ((r, 512), bf16, row(512)),
            ((r, 512), f32, row(512)), ((r, LANES), f32, row(LANES)), ((16, r), f32, col(16))]
    return pl.pallas_call(
        functools.partial(_ev_in_body, mrow=mrow, d=d),
        grid=(r // tm,),
        in_specs=[row(d), _full(mod.shape), _full(g1.shape), _full(wm.shape), _full(bm.shape), _full(wkT.shape),
                  _full(bkT.shape), _full(wg.shape), _full(bg.shape), _full(wgT.shape), _full(bgT.shape),
                  _full(qg.shape), _full(kg.shape), row(LANES), row(LANES)],
        out_specs=[o[2] for o in outs],
        out_shape=[jax.ShapeDtypeStruct(o[0], o[1]) for o in outs],
        compiler_params=_cp(("arbitrary",)),
        name="ev_in",
    )(x, mod, g1, wm, bm, wkT, bkT, wg, bg, wgT, bgT, qg, kg, cos, sin)


def _attn_body(q_ref, k_ref, v_ref, o_ref, m_sc, l_sc, acc_sc):
    j = pl.program_id(1)

    @pl.when(j == 0)
    def _():
        m_sc[...] = jnp.full(m_sc.shape, -jnp.inf, f32)
        l_sc[...] = jnp.zeros(l_sc.shape, f32)
        acc_sc[...] = jnp.zeros(acc_sc.shape, f32)

    for h in range(ATTN_HEADS):
        g = h // (ATTN_HEADS // ATTN_KV_HEADS)
        q = q_ref[:, h * 128:(h + 1) * 128]
        k = k_ref[:, g * 128:(g + 1) * 128]
        v = v_ref[:, g * 128:(g + 1) * 128]
        s = _dot_nt(q, k)
        m_prev = m_sc[h]
        m_new = jnp.maximum(m_prev, jnp.max(s, axis=-1, keepdims=True))
        alpha = jnp.exp(m_prev - m_new)
        p = jnp.exp(s - m_new)
        l_sc[h] = alpha * l_sc[h] + jnp.sum(p, axis=-1, keepdims=True)
        acc_sc[h] = alpha * acc_sc[h] + _dot(p.astype(bf16), v)
        m_sc[h] = m_new

    @pl.when(j == pl.num_programs(1) - 1)
    def _():
        for h in range(ATTN_HEADS):
            o_ref[:, h * 128:(h + 1) * 128] = (acc_sc[h] / l_sc[h]).astype(bf16)


def _attention(q, k, v, tq, tk):
    nq, nk = q.shape[0], k.shape[0]
    return pl.pallas_call(
        _attn_body,
        grid=(nq // tq, nk // tk),
        in_specs=[pl.BlockSpec((tq, 512), lambda i, j: (i, 0)),
                  pl.BlockSpec((tk, 256), lambda i, j: (j, 0)),
                  pl.BlockSpec((tk, 256), lambda i, j: (j, 0))],
        out_specs=pl.BlockSpec((tq, 512), lambda i, j: (i, 0)),
        out_shape=jax.ShapeDtypeStruct((nq, 512), bf16),
        scratch_shapes=[pltpu.VMEM((ATTN_HEADS, tq, 1), f32), pltpu.VMEM((ATTN_HEADS, tq, 1), f32),
                        pltpu.VMEM((ATTN_HEADS, tq, 128), f32)],
        compiler_params=_cp(("arbitrary", "arbitrary")),
        name="attention",
    )(q, k, v)


def _mlstm_body(qf_ref, kTf_ref, vf_ref, gcf_ref, gTf_ref, qb_ref, kTb_ref, vb_ref, gcb_ref, gTb_ref,
                fbr_ref, fbc_ref, c0_ref, m0_ref, hf_ref, hb_ref, c_out_ref, m_out_ref, c_sc, m_sc):
    j = pl.program_id(0)

    @pl.when(j == 0)
    def _():
        c_sc[...] = c0_ref[...]
        m_sc[...] = m0_ref[...]

    ti = lax.broadcasted_iota(i32, (CHUNK, CHUNK), 0)
    si = lax.broadcasted_iota(i32, (CHUNK, CHUNK), 1)
    ones_c = jnp.ones((CHUNK, 128), f32)
    for d in range(2):
        q_ref, kT_ref, v_ref, gc_ref, gT_ref, h_ref = ((qf_ref, kTf_ref, vf_ref, gcf_ref, gTf_ref, hf_ref) if d == 0 else
                                                      (qb_ref, kTb_ref, vb_ref, gcb_ref, gTb_ref, hb_ref))
        mask = (si <= ti) if d == 0 else (si >= ti)
        lmat = mask.astype(f32)
        umat = ((ti <= si) if d == 0 else (ti >= si)).astype(f32)
        gc = gc_ref[...]
        gT = gT_ref[...]
        fcol = _log_sigmoid(gc + fbr_ref[...])
        frow = _log_sigmoid(gT + fbc_ref[...])
        bcol = jnp.dot(lmat, fcol, precision=HIGHEST, preferred_element_type=f32)
        brow = jnp.dot(frow, umat, precision=HIGHEST, preferred_element_type=f32)
        last = CHUNK - 1 if d == 0 else 0
        for h in range(MLSTM_HEADS):
            idx = d * MLSTM_HEADS + h
            ji, jf = idx, 8 + idx
            b_t = bcol[:, jf:jf + 1]
            i_t = gc[:, ji:ji + 1]
            logd = jnp.where(mask, b_t - brow[jf:jf + 1, :] + gT[ji:ji + 1, :], -jnp.inf)
            m_old = m_sc[idx][:, 0:1]
            m_row = jnp.maximum(b_t + m_old, jnp.max(logd, axis=-1, keepdims=True))
            w_inter = jnp.exp(b_t + m_old - m_row)
            dm = jnp.exp(logd - m_row)
            qh = q_ref[:, h * MLSTM_DK:(h + 1) * MLSTM_DK]
            kTh = kT_ref[h * MLSTM_DK:(h + 1) * MLSTM_DK, :]
            vaug = jnp.concatenate([v_ref[:, h * 128:(h + 1) * 128].astype(f32), ones_c], axis=1)
            s = _dot(qh, kTh) * dm
            intra = _dot(s.astype(bf16), vaug.astype(bf16))
            c_old = c_sc[idx]
            inter = _dot(qh, c_old.astype(bf16))
            num = intra[:, 0:128] + w_inter * inter[:, 0:128]
            den = intra[:, 128:129] + w_inter * inter[:, 128:129]
            h_ref[:, h * 128:(h + 1) * 128] = num / jnp.maximum(jnp.abs(den), jnp.exp(-m_row))
            b_last = b_t[last:last + 1, :]
            lw = b_last - b_t + i_t
            m_new = jnp.maximum(b_last + m_old, jnp.max(lw, axis=0, keepdims=True))
            w = jnp.exp(lw - m_new)
            decay = jnp.exp(b_last + m_old - m_new)
            c_sc[idx] = decay * c_old + _dot(kTh, (w * vaug).astype(bf16))
            m_sc[idx] = jnp.broadcast_to(m_new, (1, 128))

    @pl.when(j == pl.num_programs(0) - 1)
    def _():
        c_out_ref[...] = c_sc[...]
        m_out_ref[...] = m_sc[...]


def _mlstm(mq, mkT, mv, gc, gT, fbr, fbc, c0, m0):
    n = mq.shape[0]
    nc = n // CHUNK
    fwd = lambda j: j
    bwd = lambda j: nc - 1 - j

    def specs(ix):
        return [pl.BlockSpec((CHUNK, 256), lambda j: (ix(j), 0)),
                pl.BlockSpec((256, CHUNK), lambda j: (0, ix(j))),
                pl.BlockSpec((CHUNK, 512), lambda j: (ix(j), 0)),
                pl.BlockSpec((CHUNK, LANES), lambda j: (ix(j), 0)),
                pl.BlockSpec((16, CHUNK), lambda j: (0, ix(j)))]

    nh = 2 * MLSTM_HEADS
    return pl.pallas_call(
        _mlstm_body,
        grid=(nc,),
        in_specs=specs(fwd) + specs(bwd) + [_full(fbr.shape), _full(fbc.shape), _full(c0.shape), _full(m0.shape)],
        out_specs=[pl.BlockSpec((CHUNK, 512), lambda j: (fwd(j), 0)),
                   pl.BlockSpec((CHUNK, 512), lambda j: (bwd(j), 0)),
                   _full(c0.shape), _full(m0.shape)],
        out_shape=[jax.ShapeDtypeStruct((n, 512), f32), jax.ShapeDtypeStruct((n, 512), f32),
                   jax.ShapeDtypeStruct(c0.shape, f32), jax.ShapeDtypeStruct(m0.shape, f32)],
        scratch_shapes=[pltpu.VMEM((nh, MLSTM_DK, 256), f32), pltpu.VMEM((nh, 1, 128), f32)],
        compiler_params=_cp(("arbitrary",)),
        name="mlstm",
    )(mq, mkT, mv, gc, gT, mq, mkT, mv, gc, gT, fbr, fbc, c0, m0)


def _post_block(x, y, mod_ref, g2_ref, rw_ref, rb_ref, run_sc, h_ref, t_ref, meta_ref, gate_ref, cnt_ref, d):
    tm = x.shape[0]
    i = pl.program_id(0)

    @pl.when(i == 0)
    def _():
        run_sc[...] = jnp.zeros(run_sc.shape, f32)

    h1 = x + mod_ref[0:1, 2 * d:3 * d] * y
    h_ref[...] = h1
    t = _rms(h1, g2_ref[...]) * (1.0 + mod_ref[0:1, 4 * d:5 * d]) + mod_ref[0:1, 3 * d:4 * d]
    t_ref[...] = t
    logits = jnp.dot(t, rw_ref[...], precision=HIGHEST, preferred_element_type=f32) + rb_ref[...]
    lane = lax.broadcasted_iota(i32, (tm, LANES), 1)
    lanef = lane.astype(f32)
    neg = jnp.float32(-jnp.inf)
    cur = jnp.where(lane < N_EXPERTS, logits, neg)
    tops, ids, hots = [], [], []
    for _ in range(TOP_K):
        mx = jnp.max(cur, axis=-1, keepdims=True)
        idx = jnp.min(jnp.where(cur == mx, lanef, 1e9), axis=-1, keepdims=True)
        hot = lanef == idx
        cur = jnp.where(hot, neg, cur)
        tops.append(mx)
        ids.append(idx)
        hots.append(hot)
    es = [jnp.exp(tk - tops[0]) for tk in tops]
    den = es[0] + es[1] + es[2] + es[3]
    cnt = jnp.zeros((tm, LANES), f32)
    for hot in hots:
        cnt = cnt + hot.astype(f32)
    ri = lax.broadcasted_iota(i32, (tm, tm), 0)
    ci = lax.broadcasted_iota(i32, (tm, tm), 1)
    strict = (ci < ri).astype(bf16)
    before = _dot(strict, cnt.astype(bf16)) + run_sc[...]
    meta = jnp.zeros((tm, LANES), f32)
    gates = jnp.zeros((tm, LANES), f32)
    for k in range(TOP_K):
        rank = jnp.sum(jnp.where(hots[k], before, 0.0), axis=-1, keepdims=True)
        meta = jnp.where(lane == k, ids[k], meta)
        meta = jnp.where(lane == TOP_K + k, rank, meta)
        gates = jnp.where(lane == k, es[k] / den, gates)
    gate_ref[...] = gates
    meta_ref[...] = meta.T[0:8, :].astype(i32)
    run_sc[...] = run_sc[...] + jnp.sum(cnt, axis=0, keepdims=True)

    @pl.when(i == pl.num_programs(0) - 1)
    def _():
        cnt_ref[...] = jnp.broadcast_to(run_sc[...], cnt_ref.shape).astype(i32)


def _post_specs(n, d, tm):
    row = lambda w: pl.BlockSpec((tm, w), lambda i: (i, 0))
    out_specs = [row(d), row(d), pl.BlockSpec((8, tm), lambda i: (0, i)), row(LANES), _full((8, LANES))]
    out_shape = [jax.ShapeDtypeStruct((n, d), f32), jax.ShapeDtypeStruct((n, d), f32),
                 jax.ShapeDtypeStruct((8, n), i32), jax.ShapeDtypeStruct((n, LANES), f32),
                 jax.ShapeDtypeStruct((8, LANES), i32)]
    return out_specs, out_shape


def _ev_out_body(att_ref, hf_ref, hb_ref, og_ref, x_ref, mod_ref, hg_ref, w_ref, b_ref, g2_ref, rw_ref, rb_ref,
                 h_ref, t_ref, meta_ref, gate_ref, cnt_ref, run_sc, *, d):
    hs = hf_ref[...] + hb_ref[...]
    og = og_ref[...]
    parts = [att_ref[...]]
    for h in range(MLSTM_HEADS):
        sl = slice(h * 128, (h + 1) * 128)
        hh = hs[:, sl]
        hn = hh * lax.rsqrt(jnp.mean(hh * hh, axis=-1, keepdims=True) + EPS) * hg_ref[:, sl]
        parts.append((jax.nn.sigmoid(og[:, sl]) * hn).astype(bf16))
    y = _dot(jnp.concatenate(parts, axis=1), w_ref[...]) + b_ref[...]
    _post_block(x_ref[...], y, mod_ref, g2_ref, rw_ref, rb_ref, run_sc, h_ref, t_ref, meta_ref, gate_ref, cnt_ref, d)


def _ev_out(att, hf, hb, og, x, mod, hg, w, b, g2, rw, rb, tm):
    n, d = x.shape
    row = lambda wd: pl.BlockSpec((tm, wd), lambda i: (i, 0))
    out_specs, out_shape = _post_specs(n, d, tm)
    return pl.pallas_call(
        functools.partial(_ev_out_body, d=d),
        grid=(n // tm,),
        in_specs=[row(512), row(512), row(512), row(512), row(d), _full(mod.shape), _full(hg.shape), _full(w.shape),
                  _full(b.shape), _full(g2.shape), _full(rw.shape), _full(rb.shape)],
        out_specs=out_specs, out_shape=out_shape,
        scratch_shapes=[pltpu.VMEM((1, LANES), f32)],
        compiler_params=_cp(("arbitrary",)),
        name="ev_out",
    )(att, hf, hb, og, x, mod, hg, w, b, g2, rw, rb)


def _dispatch_body(starts_ref, meta_ref, t_ref, buf_ref, sem):
    tm = t_ref.shape[0]

    def copy(r, k):
        dest = starts_ref[meta_ref[k, r]] + meta_ref[TOP_K + k, r]
        return pltpu.make_async_copy(t_ref.at[pl.ds(r, 1)], buf_ref.at[pl.ds(dest, 1)], sem)

    def issue(r, c):
        for k in range(TOP_K):
            copy(r, k).start()
        return c

    def drain(r, c):
        for k in range(TOP_K):
            copy(r, k).wait()
        return c

    lax.fori_loop(0, tm, issue, 0)
    lax.fori_loop(0, tm, drain, 0)


def _dispatch(starts, meta, t, tm):
    n, d = t.shape
    return pl.pallas_call(
        _dispatch_body,
        grid_spec=pltpu.PrefetchScalarGridSpec(
            num_scalar_prefetch=1,
            grid=(n // tm,),
            in_specs=[pl.BlockSpec((8, tm), lambda i, s: (0, i), memory_space=pltpu.SMEM),
                      pl.BlockSpec((tm, d), lambda i, s: (i, 0))],
            out_specs=pl.BlockSpec(memory_space=pl.ANY),
            scratch_shapes=[pltpu.SemaphoreType.DMA(())]),
        out_shape=jax.ShapeDtypeStruct((n * TOP_K, d), f32),
        compiler_params=_cp(("arbitrary",)),
        name="moe_dispatch",
    )(starts, meta, t)


def _expert_body(blk_ref, exp_ref, lo_ref, act_ref, x_ref, wgu_ref, bgu_ref, wd_ref, bd_ref, o_ref, *, de):
    i = pl.program_id(0)

    @pl.when(act_ref[i] == 1)
    def _():
        x = x_ref[...].astype(bf16)
        gu = _dot(x, wgu_ref[0].astype(bf16)) + bgu_ref[0]
        g = jnp.minimum(gu[:, :de], SWIGLU_LIMIT)
        u = jnp.clip(gu[:, de:], -SWIGLU_LIMIT, SWIGLU_LIMIT)
        a = g * jax.nn.sigmoid(SWIGLU_ALPHA * g) * (u + 1.0)
        y = _dot(a.astype(bf16), wd_ref[0].astype(bf16)) + bd_ref[0]
        lo = lo_ref[i]

        @pl.when(lo == 0)
        def _():
            o_ref[...] = y

        @pl.when(lo > 0)
        def _():
            row = lax.broadcasted_iota(i32, y.shape, 0)
            o_ref[...] = jnp.where(row >= lo, y, o_ref[...])


def _experts(sched, buf, wgu, bgu, wd, bd):
    m, d = buf.shape
    ne, _, de2 = wgu.shape
    de = de2 // 2
    n_items = sched[0].shape[0]
    return pl.pallas_call(
        functools.partial(_expert_body, de=de),
        grid_spec=pltpu.PrefetchScalarGridSpec(
            num_scalar_prefetch=4,
            grid=(n_items,),
            in_specs=[pl.BlockSpec((MOE_BM, d), lambda i, b, e, l, a: (b[i], 0)),
                      pl.BlockSpec((1, d, de2), lambda i, b, e, l, a: (e[i], 0, 0)),
                      pl.BlockSpec((1, 1, de2), lambda i, b, e, l, a: (e[i], 0, 0)),
                      pl.BlockSpec((1, de, d), lambda i, b, e, l, a: (e[i], 0, 0)),
                      pl.BlockSpec((1, 1, d), lambda i, b, e, l, a: (e[i], 0, 0))],
            out_specs=pl.BlockSpec((MOE_BM, d), lambda i, b, e, l, a: (b[i], 0))),
        out_shape=jax.ShapeDtypeStruct((m, d), f32),
        compiler_params=_cp(("arbitrary",)),
        name="moe_experts",
    )(*sched, buf, wgu, bgu.reshape(ne, 1, de2), wd, bd.reshape(ne, 1, d))


def _combine_body(starts_ref, meta_ref, eo_ref, gate_ref, h_ref, mod_ref, fg_ref, o_ref, stage, sem, *, d, final):
    tm = h_ref.shape[0]

    def copy(r, k):
        src = starts_ref[meta_ref[k, r]] + meta_ref[TOP_K + k, r]
        return pltpu.make_async_copy(eo_ref.at[pl.ds(src, 1)], stage.at[k, pl.ds(r, 1)], sem)

    def issue(r, c):
        for k in range(TOP_K):
            copy(r, k).start()
        return c

    def drain(r, c):
        for k in range(TOP_K):
            copy(r, k).wait()
        return c

    lax.fori_loop(0, tm, issue, 0)
    lax.fori_loop(0, tm, drain, 0)
    gates = gate_ref[...]
    f = gates[:, 0:1] * stage[0]
    for k in range(1, TOP_K):
        f = f + gates[:, k:k + 1] * stage[k]
    h = h_ref[...] + mod_ref[0:1, 5 * d:6 * d] * f
    if final:
        h = _rms(h, fg_ref[...])
    o_ref[...] = h


def _combine(starts, meta, eo, gates, h, mod, fg, tm, final):
    n, d = h.shape
    return pl.pallas_call(
        functools.partial(_combine_body, d=d, final=final),
        grid_spec=pltpu.PrefetchScalarGridSpec(
            num_scalar_prefetch=1,
            grid=(n // tm,),
            in_specs=[pl.BlockSpec((8, tm), lambda i, s: (0, i), memory_space=pltpu.SMEM),
                      pl.BlockSpec(memory_space=pl.ANY),
                      pl.BlockSpec((tm, LANES), lambda i, s: (i, 0)),
                      pl.BlockSpec((tm, d), lambda i, s: (i, 0)),
                      pl.BlockSpec(mod.shape, lambda i, s: (0, 0)),
                      pl.BlockSpec(fg.shape, lambda i, s: (0, 0))],
            out_specs=pl.BlockSpec((tm, d), lambda i, s: (i, 0)),
            scratch_shapes=[pltpu.VMEM((TOP_K, tm, d), f32), pltpu.SemaphoreType.DMA(())]),
        out_shape=jax.ShapeDtypeStruct((n, d), f32),
        compiler_params=_cp(("arbitrary",)),
        name="moe_combine",
    )(starts, meta, eo, gates, h, mod, fg)


def _moe(t, meta, gates, counts, h, mod, fg, wgu, bgu, wd, bd, final):
    n, d = t.shape
    m = n * TOP_K
    nblk = m // MOE_BM
    n_items = nblk + N_EXPERTS - 1
    counts = counts.astype(i32)
    ends = jnp.cumsum(counts)
    starts = ends - counts
    first_blk = starts // MOE_BM
    last_blk = jnp.maximum(ends - 1, starts) // MOE_BM
    per = jnp.where(counts > 0, last_blk - first_blk + 1, 0)
    cum = jnp.cumsum(per)
    off = cum - per
    total = cum[-1]
    it = jnp.arange(n_items, dtype=i32)
    e_it = jnp.minimum(jnp.searchsorted(cum, it, side='right').astype(i32), N_EXPERTS - 1)
    blk_it = first_blk[e_it] + it - off[e_it]
    lo_it = jnp.maximum(starts[e_it] - blk_it * MOE_BM, 0)
    active = it < total
    last = jnp.maximum(total - 1, 0)
    blk_it = jnp.where(active, blk_it, blk_it[last]).astype(i32)
    e_it = jnp.where(active, e_it, e_it[last]).astype(i32)
    lo_it = jnp.where(active, lo_it, 0).astype(i32)
    sched = (blk_it, e_it, lo_it, active.astype(i32))
    buf = _dispatch(starts, meta, t, 256)
    eo = _experts(sched, buf, wgu, bgu, wd, bd)
    return _combine(starts, meta, eo, gates, h, mod, fg, 256, final)


def _rope_tables(n):
    rows = n // GRID_W
    row = jnp.repeat(jnp.arange(rows, dtype=i32), GRID_W)
    col = jnp.tile(jnp.arange(GRID_W, dtype=i32), rows)
    pos = jnp.stack([row, col], axis=-1).astype(f32)
    inv_freq = ROPE_THETA ** (-jnp.arange(ROPE_PAIRS, dtype=f32) / ROPE_PAIRS)
    ang = pos[:, :, None] * inv_freq
    c, s = jnp.cos(ang), jnp.sin(ang)
    cos = jnp.concatenate([c[:, 0], c[:, 0], c[:, 1], c[:, 1]], axis=-1)
    sin = jnp.concatenate([-s[:, 0], s[:, 0], -s[:, 1], s[:, 1]], axis=-1)
    return cos, sin


def _router_params(router_w, router_b):
    d = router_w.shape[0]
    rw = jnp.zeros((d, LANES), f32).at[:, :N_EXPERTS].set(router_w)
    rb = jnp.zeros((1, LANES), f32).at[0, :N_EXPERTS].set(router_b)
    return rw, rb


def _even_layer(x, ctx, mod, g1, g2, w_in, b_in, q_gain, k_gain, f_bias, h_gain, w_out, b_out, router_w, router_b):
    n, d = x.shape
    nctx = ctx.shape[0]
    cols = lambda a, b: w_in[:, a:b]
    wm = jnp.concatenate([cols(0, 1280), cols(1536, 2048), cols(2064, 2576)], axis=1).astype(bf16)
    bm = jnp.concatenate([b_in[0:1280], b_in[1536:2048], b_in[2064:2576]])[None, :]
    wkT = cols(1280, 1536).T.astype(bf16)
    bkT = b_in[1280:1536][:, None]
    wgate = cols(2048, 2064)
    wg = jnp.zeros((d, LANES), f32).at[:, :16].set(wgate).astype(bf16)
    bg = jnp.zeros((1, LANES), f32).at[0, :16].set(b_in[2048:2064])
    wgT = wgate.T.astype(bf16)
    bgT = b_in[2048:2064][:, None]
    wts = (wm, bm, wkT, bkT, wg, bg, wgT, bgT, q_gain[None, :], k_gain[None, :])
    cos, sin = _rope_tables(n)
    lat = _ev_in(x, mod, g1, wts, cos, sin, 0, min(512, n))
    cx = _ev_in(ctx, mod, g1, wts, jnp.ones((nctx, LANES), f32), jnp.zeros((nctx, LANES), f32), 1, nctx)
    q, k, v, mq, mkT, mv, og, gc, gT = lat
    _, kc, vc, mqc, mkTc, mvc, _, gcc, gTc = cx

    att = _attention(q, jnp.concatenate([k, kc], axis=0), jnp.concatenate([v, vc], axis=0), min(512, n), 640)

    fb = f_bias.reshape(-1)
    fbr = jnp.zeros((1, LANES), f32).at[0, 8:16].set(fb)
    fbc = jnp.zeros((16, LANES), f32).at[8:16, :].set(jnp.broadcast_to(fb[:, None], (8, LANES)))
    nh = 2 * MLSTM_HEADS
    c0 = jnp.zeros((nh, MLSTM_DK, 256), f32)
    m0 = jnp.zeros((nh, 1, 128), f32)
    _, _, c1, m1 = _mlstm(mqc, mkTc, mvc, gcc, gTc, fbr, fbc, c0, m0)
    hf, hb, _, _ = _mlstm(mq, mkT, mv, gc, gT, fbr, fbc, c1, m1)

    rw, rb = _router_params(router_w, router_b)
    return _ev_out(att, hf, hb, og, x, mod, h_gain[None, :], w_out.astype(bf16), b_out[None, :], g2, rw, rb,
                   min(512, n))


FFT_N2 = 128
CONV_CT = 8


def _hy_in_body(x_ref, mod_ref, g1_ref, wT_ref, b_ref, zT_ref, *, d):
    a = (_rms(x_ref[...], g1_ref[...]) * (1.0 + mod_ref[0:1, d:2 * d]) + mod_ref[0:1, 0:d]).astype(bf16)
    zT_ref[...] = _dot_nt(wT_ref[...], a) + b_ref[...]


def _hy_in(x, mod, g1, wT, b, tm):
    n, d = x.shape
    c3 = wT.shape[0]
    return pl.pallas_call(
        functools.partial(_hy_in_body, d=d),
        grid=(n // tm,),
        in_specs=[pl.BlockSpec((tm, d), lambda i: (i, 0)), _full(mod.shape), _full(g1.shape), _full(wT.shape),
                  _full(b.shape)],
        out_specs=pl.BlockSpec((c3, tm), lambda i: (0, i)),
        out_shape=jax.ShapeDtypeStruct((c3, n), f32),
        compiler_params=_cp(("arbitrary",)),
        name="hy_in",
    )(x, mod, g1, wT, b)


def _hy_conv_body(z_ref, zp_ref, zn_ref, cw_ref, vp_ref, x0_ref, *, ch):
    i = pl.program_id(0)
    tt = z_ref.shape[1]
    has_prev = (i > 0).astype(f32)
    has_next = (i < pl.num_programs(0) - 1).astype(f32)
    lane = lax.broadcasted_iota(i32, (ch, tt), 1)

    def conv(g):
        rows = slice(g * ch, (g + 1) * ch)
        z = z_ref[rows, :]
        pcol = zp_ref[rows, FFT_N2 - 1:FFT_N2] * has_prev
        ncol = zn_ref[rows, 0:1] * has_next
        zm1 = jnp.where(lane == 0, pcol, pltpu.roll(z, 1, 1))
        zp1 = jnp.where(lane == tt - 1, ncol, pltpu.roll(z, tt - 1, 1))
        cw = cw_ref[rows, :]
        return cw[:, 0:1] * zm1 + cw[:, 1:2] * z + cw[:, 2:3] * zp1 + cw[:, 3:4]

    x0 = conv(0)
    vp = conv(2) * conv(1)
    for j in range(tt // FFT_N2):
        x0_ref[j] = x0[:, j * FFT_N2:(j + 1) * FFT_N2]
        vp_ref[j] = vp[:, j * FFT_N2:(j + 1) * FFT_N2]


def _hy_conv(zT, cw, tt):
    c3, n = zT.shape
    ch = c3 // 3
    nb = n // FFT_N2
    per = tt // FFT_N2
    return pl.pallas_call(
        functools.partial(_hy_conv_body, ch=ch),
        grid=(n // tt,),
        in_specs=[pl.BlockSpec((c3, tt), lambda i: (0, i)),
                  pl.BlockSpec((c3, FFT_N2), lambda i: (0, jnp.maximum(i * per - 1, 0))),
                  pl.BlockSpec((c3, FFT_N2), lambda i: (0, jnp.minimum(i * per + per, nb - 1))),
                  _full(cw.shape)],
        out_specs=[pl.BlockSpec((per, ch, FFT_N2), lambda i: (i, 0, 0)),
                   pl.BlockSpec((per, ch, FFT_N2), lambda i: (i, 0, 0))],
        out_shape=[jax.ShapeDtypeStruct((nb, ch, FFT_N2), f32), jax.ShapeDtypeStruct((nb, ch, FFT_N2), f32)],
        compiler_params=_cp(("arbitrary",)),
        name="hy_conv",
    )(zT, zT, zT, cw)


def _hy_filt_body(zz_ref, w1_ref, b1_ref, fr_ref, w2_ref, b2_ref, w3_ref, b3_ref, w4T_ref, dl_ref, tr_ref, mr_ref,
                  o_ref):
    hdot = lambda a, b: jnp.dot(a, b, precision=HIGHEST, preferred_element_type=f32)
    fr = fr_ref[...]
    h = jnp.sin(fr * (hdot(zz_ref[...], w1_ref[...]) + b1_ref[...]))
    h = jnp.sin(fr * (hdot(h, w2_ref[...]) + b2_ref[...]))
    h = jnp.sin(fr * (hdot(h, w3_ref[...]) + b3_ref[...]))
    taps = _dot_nt(w4T_ref[0], h, precision=HIGHEST)
    o_ref[0] = taps * jnp.exp(-dl_ref[...] * tr_ref[0]) * mr_ref[0]


def _hy_filt(zz, w1, b1, fr, w2, b2, w3, b3, w4T, deltas, trow, mrow):
    n2l = zz.shape[0]
    nb2 = n2l // FFT_N2
    ch = w4T.shape[1]
    return pl.pallas_call(
        _hy_filt_body,
        grid=(nb2,),
        in_specs=[pl.BlockSpec((FFT_N2, LANES), lambda i: (i, 0)), _full(w1.shape), _full(b1.shape), _full(fr.shape),
                  _full(w2.shape), _full(b2.shape), _full(w3.shape), _full(b3.shape),
                  pl.BlockSpec((1,) + w4T.shape[1:], lambda i: ((i >= nb2 // 2).astype(i32), 0, 0)),
                  _full(deltas.shape),
                  pl.BlockSpec((1, 1, FFT_N2), lambda i: (i, 0, 0)),
                  pl.BlockSpec((1, 1, FFT_N2), lambda i: (i, 0, 0))],
        out_specs=pl.BlockSpec((1, ch, FFT_N2), lambda i: (i, 0, 0)),
        out_shape=jax.ShapeDtypeStruct((nb2, ch, FFT_N2), f32),
        compiler_params=_cp(("arbitrary",)),
        name="hy_filt",
    )(zz, w1, b1, fr, w2, b2, w3, b3, w4T, deltas, trow, mrow)


def _dft_stage12(x_ref, f1_ref, tr_ref, ti_ref, r2_ref):
    ct = x_ref.shape[1]
    n1 = f1_ref.shape[0] // 2
    x = jnp.concatenate([x_ref[:, c, :] for c in range(ct)], axis=1).astype(bf16)
    a = _dot(f1_ref[...], x)
    tr, ti = tr_ref[...], ti_ref[...]
    lhs = []
    for c in range(ct):
        ar = a[0:n1, c * FFT_N2:(c + 1) * FFT_N2]
        ai = a[n1:2 * n1, c * FFT_N2:(c + 1) * FFT_N2]
        lhs.append(jnp.concatenate([ar * tr - ai * ti, ar * ti + ai * tr], axis=1).astype(bf16))
    xf = _dot(jnp.concatenate(lhs, axis=0), r2_ref[...])
    return [(xf[c * n1:(c + 1) * n1, 0:FFT_N2], xf[c * n1:(c + 1) * n1, FFT_N2:2 * FFT_N2]) for c in range(ct)]


def _hy_gfft_body(t_ref, f1_ref, tr_ref, ti_ref, r2_ref, gr_ref, gi_ref):
    for c, (xr, xi) in enumerate(_dft_stage12(t_ref, f1_ref, tr_ref, ti_ref, r2_ref)):
        gr_ref[:, c, :] = xr
        gi_ref[:, c, :] = xi


def _hy_gfft(taps3, f1, tr, ti, r2):
    n1, ch, _ = taps3.shape
    blk = pl.BlockSpec((n1, CONV_CT, FFT_N2), lambda i: (0, i, 0))
    return pl.pallas_call(
        _hy_gfft_body,
        grid=(ch // CONV_CT,),
        in_specs=[blk, _full(f1.shape), _full(tr.shape), _full(ti.shape), _full(r2.shape)],
        out_specs=[blk, blk],
        out_shape=[jax.ShapeDtypeStruct(taps3.shape, f32), jax.ShapeDtypeStruct(taps3.shape, f32)],
        compiler_params=_cp(("arbitrary",)),
        name="hy_gfft",
    )(taps3, f1, tr, ti, r2)


def _hy_fftconv_body(v_ref, gr_ref, gi_ref, f1_ref, tr_ref, ti_ref, r2_ref, r2i_ref, i2_ref, y_ref):
    ct = v_ref.shape[1]
    n1 = f1_ref.shape[0] // 2
    tr, ti = tr_ref[...], ti_ref[...]
    lhs = []
    for c, (xr, xi) in enumerate(_dft_stage12(v_ref, f1_ref, tr_ref, ti_ref, r2_ref)):
        gr, gi = gr_ref[:, c, :], gi_ref[:, c, :]
        lhs.append(jnp.concatenate([xr * gr - xi * gi, xr * gi + xi * gr], axis=1).astype(bf16))
    b = _dot(jnp.concatenate(lhs, axis=0), r2i_ref[...])
    brs, bis = [], []
    for c in range(ct):
        br = b[c * n1:(c + 1) * n1, 0:FFT_N2]
        bi = b[c * n1:(c + 1) * n1, FFT_N2:2 * FFT_N2]
        brs.append(br * tr + bi * ti)
        bis.append(bi * tr - br * ti)
    rhs = jnp.concatenate([jnp.concatenate(brs, axis=1), jnp.concatenate(bis, axis=1)], axis=0).astype(bf16)
    y = _dot(i2_ref[...], rhs)
    for c in range(ct):
        y_ref[:, c, :] = y[:, c * FFT_N2:(c + 1) * FFT_N2]


def _hy_fftconv(vp3, gr, gi, f1, tr, ti, r2, r2i, i2):
    nb, ch, _ = vp3.shape
    n1 = gr.shape[0]
    vblk = pl.BlockSpec((nb, CONV_CT, FFT_N2), lambda i: (0, i, 0))
    gblk = pl.BlockSpec((n1, CONV_CT, FFT_N2), lambda i: (0, i, 0))
    return pl.pallas_call(
        _hy_fftconv_body,
        grid=(ch // CONV_CT,),
        in_specs=[vblk, gblk, gblk, _full(f1.shape), _full(tr.shape), _full(ti.shape), _full(r2.shape),
                  _full(r2i.shape), _full(i2.shape)],
        out_specs=vblk,
        out_shape=jax.ShapeDtypeStruct(vp3.shape, f32),
        compiler_params=_cp(("arbitrary",)),
        name="hy_fftconv",
    )(vp3, gr, gi, f1, tr, ti, r2, r2i, i2)


def _hy_out_body(y_ref, vp_ref, x0_ref, sk_ref, w_ref, b_ref, x_ref, mod_ref, g2_ref, rw_ref, rb_ref,
                 h_ref, t_ref, meta_ref, gate_ref, cnt_ref, run_sc, *, d):
    sk = sk_ref[...]
    g = jnp.concatenate([((y_ref[j] + vp_ref[j] * sk) * x0_ref[j]).astype(bf16) for j in range(y_ref.shape[0])],
                        axis=1)
    y = _dot_tn(g, w_ref[...]) + b_ref[...]
    _post_block(x_ref[...], y, mod_ref, g2_ref, rw_ref, rb_ref, run_sc, h_ref, t_ref, meta_ref, gate_ref, cnt_ref, d)


def _hy_out(y3, vp3, x03, skip, w, b, x, mod, g2, rw, rb, tm):
    n, d = x.shape
    ch = y3.shape[1]
    per = tm // FFT_N2
    blk = pl.BlockSpec((per, ch, FFT_N2), lambda i: (i, 0, 0))
    out_specs, out_shape = _post_specs(n, d, tm)
    return pl.pallas_call(
        functools.partial(_hy_out_body, d=d),
        grid=(n // tm,),
        in_specs=[blk, blk, blk, _full(skip.shape), _full(w.shape), _full(b.shape),
                  pl.BlockSpec((tm, d), lambda i: (i, 0)), _full(mod.shape), _full(g2.shape), _full(rw.shape),
                  _full(rb.shape)],
        out_specs=out_specs, out_shape=out_shape,
        scratch_shapes=[pltpu.VMEM((1, LANES), f32)],
        compiler_params=_cp(("arbitrary",)),
        name="hy_out",
    )(y3, vp3, x03, skip, w, b, x, mod, g2, rw, rb)


def _dft_constants(n):
    nn = 2 * n
    n1 = nn // FFT_N2
    k1 = np.arange(n1)[:, None]
    phi = 2.0 * np.pi * (k1 * np.arange(n1)[None, :] % n1) / n1
    f1 = np.concatenate([np.cos(phi), -np.sin(phi)], axis=0)
    th = 2.0 * np.pi * (k1 * np.arange(FFT_N2)[None, :]) / nn
    tr, ti = np.cos(th), -np.sin(th)
    ps = 2.0 * np.pi * (np.arange(FFT_N2)[:, None] * np.arange(FFT_N2)[None, :] % FFT_N2) / FFT_N2
    c2, s2 = np.cos(ps), np.sin(ps)
    r2 = np.block([[c2, -s2], [s2, c2]])
    r2i = np.block([[c2, s2], [-s2, c2]])
    phi_i = phi[: n1 // 2, :]
    i2 = np.concatenate([np.cos(phi_i), -np.sin(phi_i)], axis=1) / nn
    as32 = lambda a: jnp.asarray(a.astype(np.float32))
    return dict(f1=as32(f1).astype(bf16), f1v=as32(f1[:, : n1 // 2]).astype(bf16), tr=as32(tr), ti=as32(ti),
                r2=as32(r2).astype(bf16), r2i=as32(r2i).astype(bf16), i2=as32(i2).astype(bf16))


def _filter_tables(n):
    t = jnp.linspace(0.0, 1.0, n, dtype=f32)[:, None]
    w = (2.0 * math.pi / n) * jnp.arange(n, dtype=f32)[:, None]
    f = jnp.linspace(1e-4, FILTER_BANDS - 1, FILTER_BANDS, dtype=f32)
    z = jnp.concatenate([t, jnp.cos(f * w), -jnp.sin(f * w)], axis=-1)
    pos2 = jnp.arange(2 * n, dtype=i32)
    pos = jnp.where(pos2 < n, pos2, jnp.minimum(2 * n - pos2, n - 1))
    zz = jnp.zeros((2 * n, LANES), f32).at[:, :FILTER_EMB].set(z[pos])
    trow = t[pos, 0].reshape(2 * n // FFT_N2, 1, FFT_N2)
    mrow = (pos2 != n).astype(f32).reshape(2 * n // FFT_N2, 1, FFT_N2)
    return zz, trow, mrow


def _odd_layer(x, mod, g1, g2, w_in, b_in, conv_w, conv_b, fw1, fb1, ffreq, fw2, fb2, fw3, fb3, fw4, skip, w_out,
               b_out, router_w, router_b):
    n, d = x.shape
    ch = w_out.shape[0]
    tm = min(512, n)
    zT = _hy_in(x, mod, g1, w_in.T.astype(bf16), b_in[:, None], tm)
    cw = jnp.zeros((3 * ch, LANES), f32).at[:, 0:3].set(conv_w.T).at[:, 3].set(conv_b)
    vp3, x03 = _hy_conv(zT, cw, tm)

    zz, trow, mrow = _filter_tables(n)
    w1 = jnp.zeros((LANES, fw1.shape[1]), f32).at[:FILTER_EMB].set(fw1)
    w4T = jnp.stack([fw4[:, :ch].T, fw4[:, ch:].T])
    deltas = jnp.abs(jnp.linspace(math.log(DECAY_TARGET) / SLOW_DECAY, math.log(DECAY_TARGET) / FAST_DECAY, ch,
                                  dtype=f32))[:, None]
    taps3 = _hy_filt(zz, w1, fb1[None, :], ffreq[None, :], fw2, fb2[None, :], fw3, fb3[None, :], w4T, deltas, trow,
                     mrow)
    k = _dft_constants(n)
    gr, gi = _hy_gfft(taps3, k["f1"], k["tr"], k["ti"], k["r2"])
    y3 = _hy_fftconv(vp3, gr, gi, k["f1v"], k["tr"], k["ti"], k["r2"], k["r2i"], k["i2"])
    rw, rb = _router_params(router_w, router_b)
    return _hy_out(y3, vp3, x03, skip[:, None], w_out.astype(bf16), b_out[None, :], x, mod, g2, rw, rb, tm)


def kernel(x, c, ctx, c_ctx, ada_w, ada_b, norm1_g, norm2_g, ev_w_in, ev_b_in, ev_q_gain, ev_k_gain, ev_f_bias,
           ev_h_gain, ev_w_out, ev_b_out, od_w_in, od_b_in, od_conv_w, od_conv_b, od_filt_w1, od_filt_b1,
           od_filt_freq, od_filt_w2, od_filt_b2, od_filt_w3, od_filt_b3, od_filt_w4, od_skip, od_w_out, od_b_out,
           router_w, router_b, moe_w_gu, moe_b_gu, moe_w_down, moe_b_down, final_g):
    bsz, n, d = x.shape
    assert bsz == 1 and c.shape[0] == 1
    depth = ada_w.shape[0]
    assert depth in (1, 2)
    cv = jnp.zeros((8, d), f32).at[0].set(c[0]).at[1].set(c_ctx)
    mods = _mods(cv, ada_w, ada_b)
    fg = final_g[None, :]

    h, t, meta, gates, cnt = _even_layer(
        x[0], ctx[0], mods[0], norm1_g[0][None, :], norm2_g[0][None, :], ev_w_in[0], ev_b_in[0], ev_q_gain[0],
        ev_k_gain[0], ev_f_bias[0], ev_h_gain[0], ev_w_out[0], ev_b_out[0], router_w[0], router_b[0])
    h = _moe(t, meta, gates, cnt[0, :N_EXPERTS], h, mods[0], fg, moe_w_gu[0], moe_b_gu[0], moe_w_down[0],
             moe_b_down[0], final=(depth == 1))
    if depth == 2:
        h, t, meta, gates, cnt = _odd_layer(
            h, mods[1], norm1_g[1][None, :], norm2_g[1][None, :], od_w_in[0], od_b_in[0], od_conv_w[0], od_conv_b[0],
            od_filt_w1[0], od_filt_b1[0], od_filt_freq[0], od_filt_w2[0], od_filt_b2[0], od_filt_w3[0],
            od_filt_b3[0], od_filt_w4[0], od_skip[0], od_w_out[0], od_b_out[0], router_w[1], router_b[1])
        h = _moe(t, meta, gates, cnt[0, :N_EXPERTS], h, mods[1], fg, moe_w_gu[1], moe_b_gu[1], moe_w_down[1],
                 moe_b_down[1], final=True)
    return h[None]
```

```python
import functools
import math

import numpy as np
import jax
import jax.numpy as jnp
from jax import lax
from jax.experimental import pallas as pl
from jax.experimental.pallas import tpu as pltpu

f32 = jnp.float32
bf16 = jnp.bfloat16
i32 = jnp.int32
HIGHEST = lax.Precision.HIGHEST

EPS = 1e-6
GRID_W = 64
HEAD_DIM = 128
ATTN_HEADS = 4
ATTN_KV_HEADS = 2
ROPE_THETA = 10000.0
ROPE_PAIRS = HEAD_DIM // 4
MLSTM_HEADS = 4
MLSTM_DK = 64
MLSTM_DV = 128
CHUNK = 128
N_EXPERTS = 32
TOP_K = 4
SWIGLU_LIMIT = 7.0
SWIGLU_ALPHA = 1.702
FILTER_EMB = 33
FILTER_BANDS = 16
DECAY_TARGET = 1e-2
FAST_DECAY = 0.3
SLOW_DECAY = 1.5

LANES = 128
VMEM_LIMIT = 56 * 1024 * 1024
MOE_BM = 256


def _cp(sem, vmem=VMEM_LIMIT):
    return pltpu.CompilerParams(dimension_semantics=sem, vmem_limit_bytes=vmem)


def _full(shape):
    n = len(shape)
    return pl.BlockSpec(shape, lambda *a, _n=n: (0,) * _n)


def _rms(x, g):
    return x * lax.rsqrt(jnp.mean(x * x, axis=-1, keepdims=True) + EPS) * g


def _log_sigmoid(x):
    return jnp.minimum(x, 0.0) - jnp.log(1.0 + jnp.exp(-jnp.abs(x)))


def _dot(a, b):
    return jnp.dot(a, b, preferred_element_type=f32)


def _dot_nt(a, b, precision=None):
    return lax.dot_general(a, b, (((1,), (1,)), ((), ())), preferred_element_type=f32, precision=precision)


def _dot_tn(a, b):
    return lax.dot_general(a, b, (((0,), (0,)), ((), ())), preferred_element_type=f32)


def _mods_body(cv_ref, w_ref, b_ref, o_ref):
    cv = cv_ref[...]
    s = cv * jax.nn.sigmoid(cv)
    o_ref[0] = jnp.dot(s, w_ref[0], precision=HIGHEST, preferred_element_type=f32) + b_ref[0]


def _mods(cv, ada_w, ada_b):
    depth, d, nm = ada_w.shape
    tn = 1024
    return pl.pallas_call(
        _mods_body,
        grid=(depth, nm // tn),
        in_specs=[_full((8, d)),
                  pl.BlockSpec((1, d, tn), lambda i, j: (i, 0, j)),
                  pl.BlockSpec((1, 1, tn), lambda i, j: (i, 0, j))],
        out_specs=pl.BlockSpec((1, 8, tn), lambda i, j: (i, 0, j)),
        out_shape=jax.ShapeDtypeStruct((depth, 8, nm), f32),
        compiler_params=_cp(("arbitrary", "arbitrary")),
        name="mods",
    )(cv, ada_w, ada_b.reshape(depth, 1, nm))


def _ev_in_body(x_ref, mod_ref, g1_ref, wm_ref, bm_ref, wkT_ref, bkT_ref, wg_ref, bg_ref, wgT_ref, bgT_ref,
                qg_ref, kg_ref, cos_ref, sin_ref,
                q_ref, k_ref, v_ref, mq_ref, mkT_ref, mv_ref, og_ref, gc_ref, gT_ref, *, mrow, d):
    x = x_ref[...]
    shift = mod_ref[mrow:mrow + 1, 0:d]
    scale = mod_ref[mrow:mrow + 1, d:2 * d]
    a = (_rms(x, g1_ref[...]) * (1.0 + scale) + shift).astype(bf16)
    z = _dot(a, wm_ref[...]) + bm_ref[...]
    cos = cos_ref[...]
    sin = sin_ref[...]
    lane = lax.broadcasted_iota(i32, cos.shape, 1)
    first_half = (lane % 64) < 32

    def qk_head(zh, gain, mult):
        zn = zh * lax.rsqrt(jnp.mean(zh * zh, axis=-1, keepdims=True) + EPS) * gain
        partner = jnp.where(first_half, pltpu.roll(zn, 96, 1), pltpu.roll(zn, 32, 1))
        return ((zn * cos + partner * sin) * mult).astype(bf16)

    for h in range(ATTN_HEADS):
        q_ref[:, h * 128:(h + 1) * 128] = qk_head(z[:, h * 128:(h + 1) * 128], qg_ref[...],
                                                   HEAD_DIM ** -0.5 * math.log2(math.e))
    for h in range(ATTN_KV_HEADS):
        k_ref[:, h * 128:(h + 1) * 128] = qk_head(z[:, 512 + h * 128:512 + (h + 1) * 128], kg_ref[...], 1.0)
    v_ref[...] = z[:, 768:1024].astype(bf16)
    mq_ref[...] = (z[:, 1024:1280] * (MLSTM_DK ** -0.5)).astype(bf16)
    mv_ref[...] = z[:, 1280:1792].astype(bf16)
    og_ref[...] = z[:, 1792:2304]
    mkT_ref[...] = (_dot_nt(wkT_ref[...], a) + bkT_ref[...]).astype(bf16)
    gc_ref[...] = _dot(a, wg_ref[...]) + bg_ref[...]
    gT_ref[...] = _dot_nt(wgT_ref[...], a) + bgT_ref[...]


def _ev_in(x, mod, g1, wts, cos, sin, mrow, tm):
    r, d = x.shape
    wm, bm, wkT, bkT, wg, bg, wgT, bgT, qg, kg = wts
    row = lambda n: pl.BlockSpec((tm, n), lambda i: (i, 0))
    col = lambda n: pl.BlockSpec((n, tm), lambda i: (0, i))
    outs = [((r, 512), bf16, row(512)), ((r, 256), bf16, row(256)), ((r, 256), bf16, row(256)),
            ((r, 256), bf16, row(256)), ((256, r), bf16, col(256)), ((r, 512), bf16, row(512)),
            ((r, 512), f32, row(512)), ((r, LANES), f32, row(LANES)), ((16, r), f32, col(16))]
    return pl.pallas_call(
        functools.partial(_ev_in_body, mrow=mrow, d=d),
        grid=(r // tm,),
        in_specs=[row(d), _full(mod.shape), _full(g1.shape), _full(wm.shape), _full(bm.shape), _full(wkT.shape),
                  _full(bkT.shape), _full(wg.shape), _full(bg.shape), _full(wgT.shape), _full(bgT.shape),
                  _full(qg.shape), _full(kg.shape), row(LANES), row(LANES)],
        out_specs=[o[2] for o in outs],
        out_shape=[jax.ShapeDtypeStruct(o[0], o[1]) for o in outs],
        compiler_params=_cp(("arbitrary",)),
        name="ev_in",
    )(x, mod, g1, wm, bm, wkT, bkT, wg, bg, wgT, bgT, qg, kg, cos, sin)


def _attn_body(q_ref, k_ref, v_ref, o_ref, m_sc, acc_sc):
    j = pl.program_id(1)
    group = ATTN_HEADS // ATTN_KV_HEADS

    @pl.when(j == 0)
    def _():
        m_sc[...] = jnp.full(m_sc.shape, -jnp.inf, f32)
        acc_sc[...] = jnp.zeros(acc_sc.shape, f32)

    for g in range(ATTN_KV_HEADS):
        q = jnp.concatenate([q_ref[:, (g * group + i) * 128:(g * group + i + 1) * 128] for i in range(group)], axis=0)
        s = _dot_nt(q, k_ref[:, g * 128:(g + 1) * 128])
        m_prev = m_sc[g]
        m_new = jnp.maximum(m_prev, jnp.max(s, axis=-1, keepdims=True))
        p = jnp.exp2(s - m_new).astype(bf16)
        acc_sc[g] = jnp.exp2(m_prev - m_new) * acc_sc[g] + _dot(p, v_ref[:, g * 256:(g + 1) * 256])
        m_sc[g] = m_new

    @pl.when(j == pl.num_programs(1) - 1)
    def _():
        tq = q_ref.shape[0]
        for g in range(ATTN_KV_HEADS):
            acc = acc_sc[g]
            o = (acc[:, 0:128] / acc[:, 128:256]).astype(bf16)
            for i in range(group):
                o_ref[:, (g * group + i) * 128:(g * group + i + 1) * 128] = o[i * tq:(i + 1) * tq]


def _attention(q, k, v, tq, tk):
    nq, nk = q.shape[0], k.shape[0]
    assert nq % tq == 0 and nk % tk == 0
    rows = tq * (ATTN_HEADS // ATTN_KV_HEADS)
    return pl.pallas_call(
        _attn_body,
        grid=(nq // tq, nk // tk),
        in_specs=[pl.BlockSpec((tq, 512), lambda i, j: (i, 0)),
                  pl.BlockSpec((tk, 256), lambda i, j: (j, 0)),
                  pl.BlockSpec((tk, 512), lambda i, j: (j, 0))],
        out_specs=pl.BlockSpec((tq, 512), lambda i, j: (i, 0)),
        out_shape=jax.ShapeDtypeStruct((nq, 512), bf16),
        scratch_shapes=[pltpu.VMEM((ATTN_KV_HEADS, rows, 1), f32), pltpu.VMEM((ATTN_KV_HEADS, rows, 256), f32)],
        compiler_params=_cp(("arbitrary", "arbitrary")),
        name="attention",
    )(q, k, v)


def _mlstm_body(qf_ref, kTf_ref, vf_ref, gcf_ref, gTf_ref, qb_ref, kTb_ref, vb_ref, gcb_ref, gTb_ref,
                fbr_ref, fbc_ref, c0_ref, m0_ref, hf_ref, hb_ref, c_out_ref, m_out_ref, c_sc, m_sc):
    j = pl.program_id(0)

    @pl.when(j == 0)
    def _():
        c_sc[...] = c0_ref[...]
        m_sc[...] = m0_ref[...]

    ti = lax.broadcasted_iota(i32, (CHUNK, CHUNK), 0)
    si = lax.broadcasted_iota(i32, (CHUNK, CHUNK), 1)
    ones_c = jnp.ones((CHUNK, 128), f32)
    for d in range(2):
        q_ref, kT_ref, v_ref, gc_ref, gT_ref, h_ref = ((qf_ref, kTf_ref, vf_ref, gcf_ref, gTf_ref, hf_ref) if d == 0 else
                                                      (qb_ref, kTb_ref, vb_ref, gcb_ref, gTb_ref, hb_ref))
        mask = (si <= ti) if d == 0 else (si >= ti)
        lmat = mask.astype(f32)
        umat = ((ti <= si) if d == 0 else (ti >= si)).astype(f32)
        gc = gc_ref[...]
        gT = gT_ref[...]
        fcol = _log_sigmoid(gc + fbr_ref[...])
        frow = _log_sigmoid(gT + fbc_ref[...])
        bcol = jnp.dot(lmat, fcol, precision=HIGHEST, preferred_element_type=f32)
        brow = jnp.dot(frow, umat, precision=HIGHEST, preferred_element_type=f32)
        last = CHUNK - 1 if d == 0 else 0
        for h in range(MLSTM_HEADS):
            idx = d * MLSTM_HEADS + h
            ji, jf = idx, 8 + idx
            b_t = bcol[:, jf:jf + 1]
            i_t = gc[:, ji:ji + 1]
            logd = jnp.where(mask, b_t - brow[jf:jf + 1, :] + gT[ji:ji + 1, :], -jnp.inf)
            m_old = m_sc[idx][:, 0:1]
            m_row = jnp.maximum(b_t + m_old, jnp.max(logd, axis=-1, keepdims=True))
            w_inter = jnp.exp(b_t + m_old - m_row)
            dm = jnp.exp(logd - m_row)
            qh = q_ref[:, h * MLSTM_DK:(h + 1) * MLSTM_DK]
            kTh = kT_ref[h * MLSTM_DK:(h + 1) * MLSTM_DK, :]
            vaug = jnp.concatenate([v_ref[:, h * 128:(h + 1) * 128].astype(f32), ones_c], axis=1)
            s = _dot(qh, kTh) * dm
            intra = _dot(s.astype(bf16), vaug.astype(bf16))
            c_old = c_sc[idx]
            inter = _dot(qh, c_old.astype(bf16))
            num = intra[:, 0:128] + w_inter * inter[:, 0:128]
            den = intra[:, 128:129] + w_inter * inter[:, 128:129]
            h_ref[:, h * 128:(h + 1) * 128] = num / jnp.maximum(jnp.abs(den), jnp.exp(-m_row))
            b_last = b_t[last:last + 1, :]
            lw = b_last - b_t + i_t
            m_new = jnp.maximum(b_last + m_old, jnp.max(lw, axis=0, keepdims=True))
            w = jnp.exp(lw - m_new)
            decay = jnp.exp(b_last + m_old - m_new)
            c_sc[idx] = decay * c_old + _dot(kTh, (w * vaug).astype(bf16))
            m_sc[idx] = jnp.broadcast_to(m_new, (1, 128))

    @pl.when(j == pl.num_programs(0) - 1)
    def _():
        c_out_ref[...] = c_sc[...]
        m_out_ref[...] = m_sc[...]


def _mlstm(mq, mkT, mv, gc, gT, fbr, fbc, c0, m0):
    n = mq.shape[0]
    nc = n // CHUNK
    fwd = lambda j: j
    bwd = lambda j: nc - 1 - j

    def specs(ix):
        return [pl.BlockSpec((CHUNK, 256), lambda j: (ix(j), 0)),
                pl.BlockSpec((256, CHUNK), lambda j: (0, ix(j))),
                pl.BlockSpec((CHUNK, 512), lambda j: (ix(j), 0)),
                pl.BlockSpec((CHUNK, LANES), lambda j: (ix(j), 0)),
                pl.BlockSpec((16, CHUNK), lambda j: (0, ix(j)))]

    nh = 2 * MLSTM_HEADS
    return pl.pallas_call(
        _mlstm_body,
        grid=(nc,),
        in_specs=specs(fwd) + specs(bwd) + [_full(fbr.shape), _full(fbc.shape), _full(c0.shape), _full(m0.shape)],
        out_specs=[pl.BlockSpec((CHUNK, 512), lambda j: (fwd(j), 0)),
                   pl.BlockSpec((CHUNK, 512), lambda j: (bwd(j), 0)),
                   _full(c0.shape), _full(m0.shape)],
        out_shape=[jax.ShapeDtypeStruct((n, 512), f32), jax.ShapeDtypeStruct((n, 512), f32),
                   jax.ShapeDtypeStruct(c0.shape, f32), jax.ShapeDtypeStruct(m0.shape, f32)],
        scratch_shapes=[pltpu.VMEM((nh, MLSTM_DK, 256), f32), pltpu.VMEM((nh, 1, 128), f32)],
        compiler_params=_cp(("arbitrary",)),
        name="mlstm",
    )(mq, mkT, mv, gc, gT, mq, mkT, mv, gc, gT, fbr, fbc, c0, m0)


def _post_block(x, y, mod_ref, g2_ref, rw_ref, rb_ref, run_sc, h_ref, t_ref, meta_ref, cnt_ref, d):
    tm = x.shape[0]
    i = pl.program_id(0)

    @pl.when(i == 0)
    def _():
        run_sc[...] = jnp.zeros(run_sc.shape, f32)

    h1 = x + mod_ref[0:1, 2 * d:3 * d] * y
    h_ref[...] = h1
    t = _rms(h1, g2_ref[...]) * (1.0 + mod_ref[0:1, 4 * d:5 * d]) + mod_ref[0:1, 3 * d:4 * d]
    for j in range(d // LANES):
        t_ref[pl.ds(j, tm, stride=d // LANES), :] = t[:, j * LANES:(j + 1) * LANES]
    t_hi = t.astype(bf16)
    t_lo = (t - t_hi.astype(f32)).astype(bf16)
    logits = _dot(t_hi, rw_ref[0]) + (_dot(t_hi, rw_ref[1]) + _dot(t_lo, rw_ref[0])) + rb_ref[...]
    lane = lax.broadcasted_iota(i32, (tm, LANES), 1)
    lanef = lane.astype(f32)
    neg = jnp.float32(-jnp.inf)
    cur = jnp.where(lane < N_EXPERTS, logits, neg)
    tops, ids, hots = [], [], []
    for _ in range(TOP_K):
        mx = jnp.max(cur, axis=-1, keepdims=True)
        idx = jnp.min(jnp.where(cur == mx, lanef, 1e9), axis=-1, keepdims=True)
        hot = lanef == idx
        cur = jnp.where(hot, neg, cur)
        tops.append(mx)
        ids.append(idx)
        hots.append(hot)
    es = [jnp.exp(tk - tops[0]) for tk in tops]
    den = es[0] + es[1] + es[2] + es[3]
    cnt = jnp.zeros((tm, LANES), f32)
    for hot in hots:
        cnt = cnt + hot.astype(f32)
    ri = lax.broadcasted_iota(i32, (tm, tm), 0)
    ci = lax.broadcasted_iota(i32, (tm, tm), 1)
    strict = (ci < ri).astype(bf16)
    before = _dot(strict, cnt.astype(bf16)) + run_sc[...]
    meta = jnp.zeros((tm, LANES), f32)
    for k in range(TOP_K):
        rank = jnp.sum(jnp.where(hots[k], before, 0.0), axis=-1, keepdims=True)
        meta = jnp.where(lane == k, ids[k], meta)
        meta = jnp.where(lane == TOP_K + k, rank, meta)
        meta = jnp.where(lane == 2 * TOP_K + k, es[k] / den, meta)
    meta_ref[...] = meta
    run_sc[...] = run_sc[...] + jnp.sum(cnt, axis=0, keepdims=True)

    @pl.when(i == pl.num_programs(0) - 1)
    def _():
        cnt_ref[...] = jnp.broadcast_to(run_sc[...], cnt_ref.shape).astype(i32)


def _post_specs(n, d, tm):
    row = lambda w: pl.BlockSpec((tm, w), lambda i: (i, 0))
    out_specs = [row(d), pl.BlockSpec((tm * (d // LANES), LANES), lambda i: (i, 0)), row(LANES), _full((8, LANES))]
    out_shape = [jax.ShapeDtypeStruct((n, d), f32), jax.ShapeDtypeStruct((n * (d // LANES), LANES), f32),
                 jax.ShapeDtypeStruct((n, LANES), f32), jax.ShapeDtypeStruct((8, LANES), i32)]
    return out_specs, out_shape


def _ev_out_body(att_ref, hf_ref, hb_ref, og_ref, x_ref, mod_ref, hg_ref, w_ref, b_ref, g2_ref, rw_ref, rb_ref,
                 h_ref, t_ref, meta_ref, cnt_ref, run_sc, *, d):
    hs = hf_ref[...] + hb_ref[...]
    og = og_ref[...]
    parts = [att_ref[...]]
    for h in range(MLSTM_HEADS):
        sl = slice(h * 128, (h + 1) * 128)
        hh = hs[:, sl]
        hn = hh * lax.rsqrt(jnp.mean(hh * hh, axis=-1, keepdims=True) + EPS) * hg_ref[:, sl]
        parts.append((jax.nn.sigmoid(og[:, sl]) * hn).astype(bf16))
    y = _dot(jnp.concatenate(parts, axis=1), w_ref[...]) + b_ref[...]
    _post_block(x_ref[...], y, mod_ref, g2_ref, rw_ref, rb_ref, run_sc, h_ref, t_ref, meta_ref, cnt_ref, d)


def _ev_out(att, hf, hb, og, x, mod, hg, w, b, g2, rw, rb, tm):
    n, d = x.shape
    row = lambda wd: pl.BlockSpec((tm, wd), lambda i: (i, 0))
    out_specs, out_shape = _post_specs(n, d, tm)
    return pl.pallas_call(
        functools.partial(_ev_out_body, d=d),
        grid=(n // tm,),
        in_specs=[row(512), row(512), row(512), row(512), row(d), _full(mod.shape), _full(hg.shape), _full(w.shape),
                  _full(b.shape), _full(g2.shape), _full(rw.shape), _full(rb.shape)],
        out_specs=out_specs, out_shape=out_shape,
        scratch_shapes=[pltpu.VMEM((1, LANES), f32)],
        compiler_params=_cp(("arbitrary",)),
        name="ev_out",
    )(att, hf, hb, og, x, mod, hg, w, b, g2, rw, rb)


ROW_UNROLL = 8


TOKEN_ROWS = 8


def _dispatch_body(dest_ref, t_ref, buf_ref, sem):
    tm = t_ref.shape[0] // TOKEN_ROWS

    def issue(rb, c):
        base = pl.multiple_of(rb * (ROW_UNROLL * TOKEN_ROWS), ROW_UNROLL * TOKEN_ROWS)
        for u in range(ROW_UNROLL):
            src = t_ref.at[pl.ds(base + u * TOKEN_ROWS, TOKEN_ROWS)]
            for k in range(TOP_K):
                dest = pl.multiple_of(dest_ref[(rb * ROW_UNROLL + u) * TOP_K + k], TOKEN_ROWS)
                pltpu.make_async_copy(src, buf_ref.at[pl.ds(dest, TOKEN_ROWS)], sem).start(priority=k % 2)
        return c

    lax.fori_loop(0, tm // ROW_UNROLL, issue, 0)
    for k in range(TOP_K):
        pltpu.make_async_copy(t_ref, buf_ref.at[pl.ds(0, tm * TOKEN_ROWS)], sem).wait()


def _dispatch(dest, t, tm):
    rows, l = t.shape
    return pl.pallas_call(
        _dispatch_body,
        grid=(rows // (tm * TOKEN_ROWS),),
        in_specs=[pl.BlockSpec((tm * TOP_K,), lambda i: (i,), memory_space=pltpu.SMEM),
                  pl.BlockSpec((tm * TOKEN_ROWS, l), lambda i: (i, 0))],
        out_specs=pl.BlockSpec(memory_space=pl.ANY),
        out_shape=jax.ShapeDtypeStruct((rows * TOP_K, l), f32),
        scratch_shapes=[pltpu.SemaphoreType.DMA(())],
        compiler_params=_cp(("arbitrary",)),
        name="moe_dispatch",
    )(dest, t)


def _expert_body(blk_ref, exp_ref, lo_ref, act_ref, chg_ref, x_ref, wgu_ref, bgu_ref, wd_ref, bd_ref, o_ref,
                 wgu_sc, wd_sc, *, de):
    i = pl.program_id(0)

    @pl.when(chg_ref[i] == 1)
    def _():
        wgu_sc[...] = wgu_ref[0, 0].astype(bf16)
        wd_sc[...] = wd_ref[0, 0].astype(bf16)

    @pl.when(act_ref[i] == 1)
    def _():
        ns = TOKEN_ROWS
        x = jnp.concatenate([x_ref[pl.ds(j, MOE_BM, stride=ns), :] for j in range(ns)], axis=1).astype(bf16)
        gu = _dot(x, wgu_sc[...]) + bgu_ref[0, 0]
        g = jnp.minimum(gu[:, :de], SWIGLU_LIMIT)
        u = jnp.clip(gu[:, de:], -SWIGLU_LIMIT, SWIGLU_LIMIT)
        a = g * jax.nn.sigmoid(SWIGLU_ALPHA * g) * (u + 1.0)
        y = _dot(a.astype(bf16), wd_sc[...]) + bd_ref[0, 0]
        lo = lo_ref[i]

        @pl.when(lo == 0)
        def _():
            for j in range(ns):
                o_ref[pl.ds(j, MOE_BM, stride=ns), :] = y[:, j * LANES:(j + 1) * LANES]

        @pl.when(lo > 0)
        def _():
            keep = lax.broadcasted_iota(i32, (y.shape[0], LANES), 0) >= lo
            for j in range(ns):
                rows = pl.ds(j, MOE_BM, stride=ns)
                o_ref[rows, :] = jnp.where(keep, y[:, j * LANES:(j + 1) * LANES], o_ref[rows, :])


def _experts(sched, buf, wgu, bgu, wd, bd, layer):
    nl, ne, d, de2 = wgu.shape
    assert d == TOKEN_ROWS * LANES
    de = de2 // 2
    n_items = sched[0].shape[0]
    wmap = lambda i, b, e, l, a, c: (layer, e[i], 0, 0)
    rows = pl.BlockSpec((MOE_BM * TOKEN_ROWS, LANES), lambda i, b, e, l, a, c: (b[i], 0))
    return pl.pallas_call(
        functools.partial(_expert_body, de=de),
        grid_spec=pltpu.PrefetchScalarGridSpec(
            num_scalar_prefetch=5,
            grid=(n_items,),
            in_specs=[rows,
                      pl.BlockSpec((1, 1, d, de2), wmap),
                      pl.BlockSpec((1, 1, 1, de2), wmap),
                      pl.BlockSpec((1, 1, de, d), wmap),
                      pl.BlockSpec((1, 1, 1, d), wmap)],
            out_specs=rows,
            scratch_shapes=[pltpu.VMEM((d, de2), bf16), pltpu.VMEM((de, d), bf16)]),
        out_shape=jax.ShapeDtypeStruct(buf.shape, f32),
        compiler_params=_cp(("arbitrary",)),
        name="moe_experts",
    )(*sched, buf, wgu, bgu.reshape(nl, ne, 1, de2), wd, bd.reshape(nl, ne, 1, d))


def _combine_body(dest_ref, eo_ref, meta_ref, h_ref, mod_ref, fg_ref, o_ref, stage, sem, *, d, final):
    tm = h_ref.shape[0]
    ns = TOKEN_ROWS

    def issue(rb, c):
        base = pl.multiple_of(rb * (ROW_UNROLL * ns), ROW_UNROLL * ns)
        for u in range(ROW_UNROLL):
            for k in range(TOP_K):
                src = pl.multiple_of(dest_ref[(rb * ROW_UNROLL + u) * TOP_K + k], ns)
                pltpu.make_async_copy(eo_ref.at[pl.ds(src, ns)], stage.at[pl.ds(k * tm * ns + base + u * ns, ns)],
                                      sem).start(priority=k % 2)
        return c

    lax.fori_loop(0, tm // ROW_UNROLL, issue, 0)
    for k in range(TOP_K):
        pltpu.make_async_copy(eo_ref.at[pl.ds(0, tm * ns)], stage.at[pl.ds(k * tm * ns, tm * ns)], sem).wait()
    meta = meta_ref[...]
    f = None
    for k in range(TOP_K):
        rows = jnp.concatenate([stage[pl.ds(k * tm * ns + j, tm, stride=ns), :] for j in range(ns)], axis=1)
        term = meta[:, 2 * TOP_K + k:2 * TOP_K + k + 1] * rows
        f = term if f is None else f + term
    h = h_ref[...] + mod_ref[0:1, 5 * d:6 * d] * f
    if final:
        h = _rms(h, fg_ref[...])
    o_ref[...] = h


def _combine(dest, eo, meta, h, mod, fg, tm, final):
    n, d = h.shape
    return pl.pallas_call(
        functools.partial(_combine_body, d=d, final=final),
        grid=(n // tm,),
        in_specs=[pl.BlockSpec((tm * TOP_K,), lambda i: (i,), memory_space=pltpu.SMEM),
                  pl.BlockSpec(memory_space=pl.ANY),
                  pl.BlockSpec((tm, LANES), lambda i: (i, 0)),
                  pl.BlockSpec((tm, d), lambda i: (i, 0)),
                  _full(mod.shape), _full(fg.shape)],
        out_specs=pl.BlockSpec((tm, d), lambda i: (i, 0)),
        out_shape=jax.ShapeDtypeStruct((n, d), f32),
        scratch_shapes=[pltpu.VMEM((TOP_K * tm * TOKEN_ROWS, LANES), f32), pltpu.SemaphoreType.DMA(())],
        compiler_params=_cp(("arbitrary",)),
        name="moe_combine",
    )(dest, eo, meta, h, mod, fg)


def _moe(t, meta, counts, h, mod, fg, wgu, bgu, wd, bd, layer, final):
    n, d = h.shape
    m = n * TOP_K
    nblk = m // MOE_BM
    n_items = nblk + N_EXPERTS - 1
    counts = counts.astype(i32)
    ends = jnp.cumsum(counts)
    starts = ends - counts
    first_blk = starts // MOE_BM
    last_blk = jnp.maximum(ends - 1, starts) // MOE_BM
    per = jnp.where(counts > 0, last_blk - first_blk + 1, 0)
    cum = jnp.cumsum(per)
    off = cum - per
    total = cum[-1]
    it = jnp.minimum(jnp.arange(n_items, dtype=i32), jnp.maximum(total - 1, 0))
    e_it = jnp.minimum(jnp.sum((cum[None, :] <= it[:, None]).astype(i32), axis=1), N_EXPERTS - 1)
    eids = jnp.arange(N_EXPERTS, dtype=i32)[None, :]
    pick = lambda v: jnp.sum(jnp.where(e_it[:, None] == eids, v[None, :], 0), axis=1)
    blk_it = pick(first_blk) + it - pick(off)
    lo_it = jnp.maximum(pick(starts) - blk_it * MOE_BM, 0)
    active = (jnp.arange(n_items, dtype=i32) < total).astype(i32)
    changed = jnp.concatenate([jnp.ones((1,), i32), (e_it[1:] != e_it[:-1]).astype(i32)])
    sched = (blk_it.astype(i32), e_it.astype(i32), lo_it.astype(i32), active, changed)
    ids = meta[:, 0:TOP_K].astype(i32)
    ranks = meta[:, TOP_K:2 * TOP_K].astype(i32)
    dest = (jnp.sum(jnp.where(ids[:, :, None] == eids[None], starts[None, None, :], 0), axis=-1) + ranks).reshape(-1)
    dest = dest * TOKEN_ROWS
    buf = _dispatch(dest, t, 256)
    eo = _experts(sched, buf, wgu, bgu, wd, bd, layer)
    return _combine(dest, eo, meta, h, mod, fg, 256, final)


def _rope_tables(n):
    rows = n // GRID_W
    row = np.repeat(np.arange(rows), GRID_W)
    col = np.tile(np.arange(GRID_W), rows)
    pos = np.stack([row, col], axis=-1).astype(np.float64)
    inv_freq = ROPE_THETA ** (-np.arange(ROPE_PAIRS, dtype=np.float64) / ROPE_PAIRS)
    ang = pos[:, :, None] * inv_freq
    c, s = np.cos(ang), np.sin(ang)
    cos = np.concatenate([c[:, 0], c[:, 0], c[:, 1], c[:, 1]], axis=-1).astype(np.float32)
    sin = np.concatenate([-s[:, 0], s[:, 0], -s[:, 1], s[:, 1]], axis=-1).astype(np.float32)
    return jnp.asarray(cos), jnp.asarray(sin)


def _router_params(router_w, router_b):
    d = router_w.shape[0]
    rw = jnp.zeros((d, LANES), f32).at[:, :N_EXPERTS].set(router_w)
    rb = jnp.zeros((1, LANES), f32).at[0, :N_EXPERTS].set(router_b)
    rw_hi = rw.astype(bf16)
    rw_lo = (rw - rw_hi.astype(f32)).astype(bf16)
    return jnp.stack([rw_hi, rw_lo]), rb


def _even_layer(x, ctx, mod, g1, g2, w_in, b_in, q_gain, k_gain, f_bias, h_gain, w_out, b_out, router_w, router_b):
    n, d = x.shape
    nctx = ctx.shape[0]
    cols = lambda a, b: w_in[:, a:b]
    wm = jnp.concatenate([cols(0, 1280), cols(1536, 2048), cols(2064, 2576)], axis=1).astype(bf16)
    bm = jnp.concatenate([b_in[0:1280], b_in[1536:2048], b_in[2064:2576]])[None, :]
    wkT = cols(1280, 1536).T.astype(bf16)
    bkT = b_in[1280:1536][:, None]
    wgate = cols(2048, 2064)
    wg = jnp.zeros((d, LANES), f32).at[:, :16].set(wgate).astype(bf16)
    bg = jnp.zeros((1, LANES), f32).at[0, :16].set(b_in[2048:2064])
    wgT = wgate.T.astype(bf16)
    bgT = b_in[2048:2064][:, None]
    wts = (wm, bm, wkT, bkT, wg, bg, wgT, bgT, q_gain[None, :], k_gain[None, :])
    cos, sin = _rope_tables(n)
    lat = _ev_in(x, mod, g1, wts, cos, sin, 0, min(512, n))
    cx = _ev_in(ctx, mod, g1, wts, jnp.ones((nctx, LANES), f32), jnp.zeros((nctx, LANES), f32), 1, nctx)
    q, k, v, mq, mkT, mv, og, gc, gT = lat
    _, kc, vc, mqc, mkTc, mvc, _, gcc, gTc = cx

    v_all = jnp.concatenate([v, vc], axis=0)
    ones = jnp.ones((n + nctx, 128), bf16)
    v_aug = jnp.concatenate([v_all[:, 0:128], ones, v_all[:, 128:256], ones], axis=1)
    att = _attention(q, jnp.concatenate([k, kc], axis=0), v_aug, min(512, n), 1280)

    fb = f_bias.reshape(-1)
    fbr = jnp.zeros((1, LANES), f32).at[0, 8:16].set(fb)
    fbc = jnp.zeros((16, LANES), f32).at[8:16, :].set(jnp.broadcast_to(fb[:, None], (8, LANES)))
    nh = 2 * MLSTM_HEADS
    c0 = jnp.zeros((nh, MLSTM_DK, 256), f32)
    m0 = jnp.zeros((nh, 1, 128), f32)
    _, _, c1, m1 = _mlstm(mqc, mkTc, mvc, gcc, gTc, fbr, fbc, c0, m0)
    hf, hb, _, _ = _mlstm(mq, mkT, mv, gc, gT, fbr, fbc, c1, m1)

    rw, rb = _router_params(router_w, router_b)
    return _ev_out(att, hf, hb, og, x, mod, h_gain[None, :], w_out.astype(bf16), b_out[None, :], g2, rw, rb,
                   min(512, n))


FFT_N2 = 128
CONV_CT = 8


def _hy_in_body(x_ref, mod_ref, g1_ref, wT_ref, b_ref, zT_ref, *, d):
    a = (_rms(x_ref[...], g1_ref[...]) * (1.0 + mod_ref[0:1, d:2 * d]) + mod_ref[0:1, 0:d]).astype(bf16)
    zT_ref[...] = _dot_nt(wT_ref[...], a) + b_ref[...]


def _hy_in(x, mod, g1, wT, b, tm):
    n, d = x.shape
    c3 = wT.shape[0]
    return pl.pallas_call(
        functools.partial(_hy_in_body, d=d),
        grid=(n // tm,),
        in_specs=[pl.BlockSpec((tm, d), lambda i: (i, 0)), _full(mod.shape), _full(g1.shape), _full(wT.shape),
                  _full(b.shape)],
        out_specs=pl.BlockSpec((c3, tm), lambda i: (0, i)),
        out_shape=jax.ShapeDtypeStruct((c3, n), f32),
        compiler_params=_cp(("arbitrary",)),
        name="hy_in",
    )(x, mod, g1, wT, b)


def _hy_conv_body(z_ref, zp_ref, zn_ref, cw_ref, vp_ref, x0_ref, *, ch):
    i = pl.program_id(0)
    tt = z_ref.shape[1]
    has_prev = (i > 0).astype(f32)
    has_next = (i < pl.num_programs(0) - 1).astype(f32)
    lane = lax.broadcasted_iota(i32, (ch, tt), 1)

    def conv(g):
        rows = slice(g * ch, (g + 1) * ch)
        z = z_ref[rows, :]
        pcol = zp_ref[rows, FFT_N2 - 1:FFT_N2] * has_prev
        ncol = zn_ref[rows, 0:1] * has_next
        zm1 = jnp.where(lane == 0, pcol, pltpu.roll(z, 1, 1))
        zp1 = jnp.where(lane == tt - 1, ncol, pltpu.roll(z, tt - 1, 1))
        cw = cw_ref[rows, :]
        return cw[:, 0:1] * zm1 + cw[:, 1:2] * z + cw[:, 2:3] * zp1 + cw[:, 3:4]

    x0 = conv(0)
    vp = conv(2) * conv(1)
    for j in range(tt // FFT_N2):
        x0_ref[j] = x0[:, j * FFT_N2:(j + 1) * FFT_N2]
        vp_ref[j] = vp[:, j * FFT_N2:(j + 1) * FFT_N2]


def _hy_conv(zT, cw, tt):
    c3, n = zT.shape
    ch = c3 // 3
    nb = n // FFT_N2
    per = tt // FFT_N2
    return pl.pallas_call(
        functools.partial(_hy_conv_body, ch=ch),
        grid=(n // tt,),
        in_specs=[pl.BlockSpec((c3, tt), lambda i: (0, i)),
                  pl.BlockSpec((c3, FFT_N2), lambda i: (0, jnp.maximum(i * per - 1, 0))),
                  pl.BlockSpec((c3, FFT_N2), lambda i: (0, jnp.minimum(i * per + per, nb - 1))),
                  _full(cw.shape)],
        out_specs=[pl.BlockSpec((per, ch, FFT_N2), lambda i: (i, 0, 0)),
                   pl.BlockSpec((per, ch, FFT_N2), lambda i: (i, 0, 0))],
        out_shape=[jax.ShapeDtypeStruct((nb, ch, FFT_N2), f32), jax.ShapeDtypeStruct((nb, ch, FFT_N2), f32)],
        compiler_params=_cp(("arbitrary",)),
        name="hy_conv",
    )(zT, zT, zT, cw)


def _hy_filt_body(zz_ref, w1_ref, b1_ref, fr_ref, w2_ref, b2_ref, w3_ref, b3_ref, w4T_ref, dl_ref, tr_ref, mr_ref,
                  o_ref):
    hdot = lambda a, b: jnp.dot(a, b, precision=HIGHEST, preferred_element_type=f32)
    fr = fr_ref[...]
    h = jnp.sin(fr * (hdot(zz_ref[...], w1_ref[...]) + b1_ref[...]))
    h = jnp.sin(fr * (hdot(h, w2_ref[...]) + b2_ref[...]))
    h = jnp.sin(fr * (hdot(h, w3_ref[...]) + b3_ref[...]))
    taps = _dot_nt(w4T_ref[0], h.astype(bf16))
    o_ref[0] = taps * jnp.exp(-dl_ref[...] * tr_ref[0]) * mr_ref[0]


def _hy_filt(zz, w1, b1, fr, w2, b2, w3, b3, w4T, deltas, trow, mrow):
    n2l = zz.shape[0]
    nb2 = n2l // FFT_N2
    ch = w4T.shape[1]
    return pl.pallas_call(
        _hy_filt_body,
        grid=(nb2,),
        in_specs=[pl.BlockSpec((FFT_N2, LANES), lambda i: (i, 0)), _full(w1.shape), _full(b1.shape), _full(fr.shape),
                  _full(w2.shape), _full(b2.shape), _full(w3.shape), _full(b3.shape),
                  pl.BlockSpec((1,) + w4T.shape[1:], lambda i: ((i >= nb2 // 2).astype(i32), 0, 0)),
                  _full(deltas.shape),
                  pl.BlockSpec((1, 1, FFT_N2), lambda i: (i, 0, 0)),
                  pl.BlockSpec((1, 1, FFT_N2), lambda i: (i, 0, 0))],
        out_specs=pl.BlockSpec((1, ch, FFT_N2), lambda i: (i, 0, 0)),
        out_shape=jax.ShapeDtypeStruct((nb2, ch, FFT_N2), f32),
        compiler_params=_cp(("arbitrary",)),
        name="hy_filt",
    )(zz, w1, b1, fr, w2, b2, w3, b3, w4T, deltas, trow, mrow)


def _dft_stage12(x_ref, f1_ref, tr_ref, ti_ref, r2_ref):
    ct = x_ref.shape[1]
    n1 = f1_ref.shape[0] // 2
    x = jnp.concatenate([x_ref[:, c, :] for c in range(ct)], axis=1).astype(bf16)
    a = _dot(f1_ref[...], x)
    tr, ti = tr_ref[...], ti_ref[...]
    lhs = []
    for c in range(ct):
        ar = a[0:n1, c * FFT_N2:(c + 1) * FFT_N2]
        ai = a[n1:2 * n1, c * FFT_N2:(c + 1) * FFT_N2]
        lhs.append(jnp.concatenate([ar * tr - ai * ti, ar * ti + ai * tr], axis=1).astype(bf16))
    xf = _dot(jnp.concatenate(lhs, axis=0), r2_ref[...])
    return [(xf[c * n1:(c + 1) * n1, 0:FFT_N2], xf[c * n1:(c + 1) * n1, FFT_N2:2 * FFT_N2]) for c in range(ct)]


def _hy_gfft_body(t_ref, f1_ref, tr_ref, ti_ref, r2_ref, gr_ref, gi_ref):
    for c, (xr, xi) in enumerate(_dft_stage12(t_ref, f1_ref, tr_ref, ti_ref, r2_ref)):
        gr_ref[:, c, :] = xr
        gi_ref[:, c, :] = xi


def _hy_gfft(taps3, f1, tr, ti, r2):
    n1, ch, _ = taps3.shape
    blk = pl.BlockSpec((n1, CONV_CT, FFT_N2), lambda i: (0, i, 0))
    return pl.pallas_call(
        _hy_gfft_body,
        grid=(ch // CONV_CT,),
        in_specs=[blk, _full(f1.shape), _full(tr.shape), _full(ti.shape), _full(r2.shape)],
        out_specs=[blk, blk],
        out_shape=[jax.ShapeDtypeStruct(taps3.shape, f32), jax.ShapeDtypeStruct(taps3.shape, f32)],
        compiler_params=_cp(("arbitrary",)),
        name="hy_gfft",
    )(taps3, f1, tr, ti, r2)


def _hy_fftconv_body(v_ref, gr_ref, gi_ref, f1_ref, tr_ref, ti_ref, r2_ref, r2i_ref, i2_ref, y_ref):
    ct = v_ref.shape[1]
    n1 = f1_ref.shape[0] // 2
    tr, ti = tr_ref[...], ti_ref[...]
    lhs = []
    for c, (xr, xi) in enumerate(_dft_stage12(v_ref, f1_ref, tr_ref, ti_ref, r2_ref)):
        gr, gi = gr_ref[:, c, :], gi_ref[:, c, :]
        lhs.append(jnp.concatenate([xr * gr - xi * gi, xr * gi + xi * gr], axis=1).astype(bf16))
    b = _dot(jnp.concatenate(lhs, axis=0), r2i_ref[...])
    brs, bis = [], []
    for c in range(ct):
        br = b[c * n1:(c + 1) * n1, 0:FFT_N2]
        bi = b[c * n1:(c + 1) * n1, FFT_N2:2 * FFT_N2]
        brs.append(br * tr + bi * ti)
        bis.append(bi * tr - br * ti)
    rhs = jnp.concatenate([jnp.concatenate(brs, axis=1), jnp.concatenate(bis, axis=1)], axis=0).astype(bf16)
    y = _dot(i2_ref[...], rhs)
    for c in range(ct):
        y_ref[:, c, :] = y[:, c * FFT_N2:(c + 1) * FFT_N2]


def _hy_fftconv(vp3, gr, gi, f1, tr, ti, r2, r2i, i2):
    nb, ch, _ = vp3.shape
    n1 = gr.shape[0]
    vblk = pl.BlockSpec((nb, CONV_CT, FFT_N2), lambda i: (0, i, 0))
    gblk = pl.BlockSpec((n1, CONV_CT, FFT_N2), lambda i: (0, i, 0))
    return pl.pallas_call(
        _hy_fftconv_body,
        grid=(ch // CONV_CT,),
        in_specs=[vblk, gblk, gblk, _full(f1.shape), _full(tr.shape), _full(ti.shape), _full(r2.shape),
                  _full(r2i.shape), _full(i2.shape)],
        out_specs=vblk,
        out_shape=jax.ShapeDtypeStruct(vp3.shape, f32),
        compiler_params=_cp(("arbitrary",)),
        name="hy_fftconv",
    )(vp3, gr, gi, f1, tr, ti, r2, r2i, i2)


def _hy_out_body(y_ref, vp_ref, x0_ref, sk_ref, w_ref, b_ref, x_ref, mod_ref, g2_ref, rw_ref, rb_ref,
                 h_ref, t_ref, meta_ref, cnt_ref, run_sc, *, d):
    sk = sk_ref[...]
    g = jnp.concatenate([((y_ref[j] + vp_ref[j] * sk) * x0_ref[j]).astype(bf16) for j in range(y_ref.shape[0])],
                        axis=1)
    y = _dot_tn(g, w_ref[...]) + b_ref[...]
    _post_block(x_ref[...], y, mod_ref, g2_ref, rw_ref, rb_ref, run_sc, h_ref, t_ref, meta_ref, cnt_ref, d)


def _hy_out(y3, vp3, x03, skip, w, b, x, mod, g2, rw, rb, tm):
    n, d = x.shape
    ch = y3.shape[1]
    per = tm // FFT_N2
    blk = pl.BlockSpec((per, ch, FFT_N2), lambda i: (i, 0, 0))
    out_specs, out_shape = _post_specs(n, d, tm)
    return pl.pallas_call(
        functools.partial(_hy_out_body, d=d),
        grid=(n // tm,),
        in_specs=[blk, blk, blk, _full(skip.shape), _full(w.shape), _full(b.shape),
                  pl.BlockSpec((tm, d), lambda i: (i, 0)), _full(mod.shape), _full(g2.shape), _full(rw.shape),
                  _full(rb.shape)],
        out_specs=out_specs, out_shape=out_shape,
        scratch_shapes=[pltpu.VMEM((1, LANES), f32)],
        compiler_params=_cp(("arbitrary",)),
        name="hy_out",
    )(y3, vp3, x03, skip, w, b, x, mod, g2, rw, rb)


def _dft_constants(n):
    nn = 2 * n
    n1 = nn // FFT_N2
    k1 = np.arange(n1)[:, None]
    phi = 2.0 * np.pi * (k1 * np.arange(n1)[None, :] % n1) / n1
    f1 = np.concatenate([np.cos(phi), -np.sin(phi)], axis=0)
    th = 2.0 * np.pi * (k1 * np.arange(FFT_N2)[None, :]) / nn
    tr, ti = np.cos(th), -np.sin(th)
    ps = 2.0 * np.pi * (np.arange(FFT_N2)[:, None] * np.arange(FFT_N2)[None, :] % FFT_N2) / FFT_N2
    c2, s2 = np.cos(ps), np.sin(ps)
    r2 = np.block([[c2, -s2], [s2, c2]])
    r2i = np.block([[c2, s2], [-s2, c2]])
    phi_i = phi[: n1 // 2, :]
    i2 = np.concatenate([np.cos(phi_i), -np.sin(phi_i)], axis=1) / nn
    as32 = lambda a: jnp.asarray(a.astype(np.float32))
    return dict(f1=as32(f1).astype(bf16), f1v=as32(f1[:, : n1 // 2]).astype(bf16), tr=as32(tr), ti=as32(ti),
                r2=as32(r2).astype(bf16), r2i=as32(r2i).astype(bf16), i2=as32(i2).astype(bf16))


def _filter_tables(n):
    t = np.linspace(0.0, 1.0, n)[:, None]
    w = (2.0 * np.pi / n) * np.arange(n)[:, None]
    f = np.linspace(1e-4, FILTER_BANDS - 1, FILTER_BANDS)
    z = np.concatenate([t, np.cos(f * w), -np.sin(f * w)], axis=-1)
    pos2 = np.arange(2 * n)
    pos = np.where(pos2 < n, pos2, np.minimum(2 * n - pos2, n - 1))
    zz = np.zeros((2 * n, LANES), np.float32)
    zz[:, :FILTER_EMB] = z[pos]
    trow = t[pos, 0].astype(np.float32).reshape(2 * n // FFT_N2, 1, FFT_N2)
    mrow = (pos2 != n).astype(np.float32).reshape(2 * n // FFT_N2, 1, FFT_N2)
    return jnp.asarray(zz), jnp.asarray(trow), jnp.asarray(mrow)


def _odd_layer(x, mod, g1, g2, w_in, b_in, conv_w, conv_b, fw1, fb1, ffreq, fw2, fb2, fw3, fb3, fw4, skip, w_out,
               b_out, router_w, router_b):
    n, d = x.shape
    ch = w_out.shape[0]
    tm = min(512, n)
    zT = _hy_in(x, mod, g1, w_in.T.astype(bf16), b_in[:, None], tm)
    cw = jnp.zeros((3 * ch, LANES), f32).at[:, 0:3].set(conv_w.T).at[:, 3].set(conv_b)
    vp3, x03 = _hy_conv(zT, cw, tm)

    zz, trow, mrow = _filter_tables(n)
    w1 = jnp.zeros((LANES, fw1.shape[1]), f32).at[:FILTER_EMB].set(fw1)
    w4T = jnp.stack([fw4[:, :ch].T, fw4[:, ch:].T]).astype(bf16)
    deltas = jnp.abs(jnp.linspace(math.log(DECAY_TARGET) / SLOW_DECAY, math.log(DECAY_TARGET) / FAST_DECAY, ch,
                                  dtype=f32))[:, None]
    taps3 = _hy_filt(zz, w1, fb1[None, :], ffreq[None, :], fw2, fb2[None, :], fw3, fb3[None, :], w4T, deltas, trow,
                     mrow)
    k = _dft_constants(n)
    gr, gi = _hy_gfft(taps3, k["f1"], k["tr"], k["ti"], k["r2"])
    y3 = _hy_fftconv(vp3, gr, gi, k["f1v"], k["tr"], k["ti"], k["r2"], k["r2i"], k["i2"])
    rw, rb = _router_params(router_w, router_b)
    return _hy_out(y3, vp3, x03, skip[:, None], w_out.astype(bf16), b_out[None, :], x, mod, g2, rw, rb, tm)


def kernel(x, c, ctx, c_ctx, ada_w, ada_b, norm1_g, norm2_g, ev_w_in, ev_b_in, ev_q_gain, ev_k_gain, ev_f_bias,
           ev_h_gain, ev_w_out, ev_b_out, od_w_in, od_b_in, od_conv_w, od_conv_b, od_filt_w1, od_filt_b1,
           od_filt_freq, od_filt_w2, od_filt_b2, od_filt_w3, od_filt_b3, od_filt_w4, od_skip, od_w_out, od_b_out,
           router_w, router_b, moe_w_gu, moe_b_gu, moe_w_down, moe_b_down, final_g):
    bsz, n, d = x.shape
    assert bsz == 1 and c.shape[0] == 1
    depth = ada_w.shape[0]
    assert depth in (1, 2)
    cv = jnp.zeros((8, d), f32).at[0].set(c[0]).at[1].set(c_ctx)
    mods = _mods(cv, ada_w, ada_b)
    fg = final_g[None, :]

    h, t, meta, cnt = _even_layer(
        x[0], ctx[0], mods[0], norm1_g[0][None, :], norm2_g[0][None, :], ev_w_in[0], ev_b_in[0], ev_q_gain[0],
        ev_k_gain[0], ev_f_bias[0], ev_h_gain[0], ev_w_out[0], ev_b_out[0], router_w[0], router_b[0])
    h = _moe(t, meta, cnt[0, :N_EXPERTS], h, mods[0], fg, moe_w_gu, moe_b_gu, moe_w_down, moe_b_down, 0,
             final=(depth == 1))
    if depth == 2:
        h, t, meta, cnt = _odd_layer(
            h, mods[1], norm1_g[1][None, :], norm2_g[1][None, :], od_w_in[0], od_b_in[0], od_conv_w[0], od_conv_b[0],
            od_filt_w1[0], od_filt_b1[0], od_filt_freq[0], od_filt_w2[0], od_filt_b2[0], od_filt_w3[0],
            od_filt_b3[0], od_filt_w4[0], od_skip[0], od_w_out[0], od_b_out[0], router_w[1], router_b[1])
        h = _moe(t, meta, cnt[0, :N_EXPERTS], h, mods[1], fg, moe_w_gu, moe_b_gu, moe_w_down, moe_b_down, 1,
                 final=True)
    return h[None]
```

```python
import functools
import math

import numpy as np
import jax
import jax.numpy as jnp
from jax import lax
from jax.experimental import pallas as pl
from jax.experimental.pallas import tpu as pltpu

f32 = jnp.float32
bf16 = jnp.bfloat16
i32 = jnp.int32
HIGHEST = lax.Precision.HIGHEST

EPS = 1e-6
GRID_W = 64
HEAD_DIM = 128
ATTN_HEADS = 4
ATTN_KV_HEADS = 2
ROPE_THETA = 10000.0
ROPE_PAIRS = HEAD_DIM // 4
MLSTM_HEADS = 4
MLSTM_DK = 64
MLSTM_DV = 128
CHUNK = 128
N_EXPERTS = 32
TOP_K = 4
SWIGLU_LIMIT = 7.0
SWIGLU_ALPHA = 1.702
FILTER_EMB = 33
FILTER_BANDS = 16
DECAY_TARGET = 1e-2
FAST_DECAY = 0.3
SLOW_DECAY = 1.5

LANES = 128
VMEM_LIMIT = 56 * 1024 * 1024
MOE_BM = 512
ATTN_SLACK = 64.0


def _cp(sem, vmem=VMEM_LIMIT):
    return pltpu.CompilerParams(dimension_semantics=sem, vmem_limit_bytes=vmem)


def _full(shape):
    n = len(shape)
    return pl.BlockSpec(shape, lambda *a, _n=n: (0,) * _n)


def _rms(x, g):
    return x * lax.rsqrt(jnp.mean(x * x, axis=-1, keepdims=True) + EPS) * g


def _log_sigmoid(x):
    return jnp.minimum(x, 0.0) - jnp.log(1.0 + jnp.exp(-jnp.abs(x)))


def _dot(a, b):
    return jnp.dot(a, b, preferred_element_type=f32)


def _dot_nt(a, b, precision=None):
    return lax.dot_general(a, b, (((1,), (1,)), ((), ())), preferred_element_type=f32, precision=precision)


def _dot_tn(a, b):
    return lax.dot_general(a, b, (((0,), (0,)), ((), ())), preferred_element_type=f32)


def _mods_body(cv_ref, w_ref, b_ref, o_ref):
    cv = cv_ref[...]
    s = cv * jax.nn.sigmoid(cv)
    o_ref[0] = jnp.dot(s, w_ref[0], precision=HIGHEST, preferred_element_type=f32) + b_ref[0]


def _mods(cv, ada_w, ada_b):
    depth, d, nm = ada_w.shape
    tn = 1024
    return pl.pallas_call(
        _mods_body,
        grid=(depth, nm // tn),
        in_specs=[_full((8, d)),
                  pl.BlockSpec((1, d, tn), lambda i, j: (i, 0, j)),
                  pl.BlockSpec((1, 1, tn), lambda i, j: (i, 0, j))],
        out_specs=pl.BlockSpec((1, 8, tn), lambda i, j: (i, 0, j)),
        out_shape=jax.ShapeDtypeStruct((depth, 8, nm), f32),
        compiler_params=_cp(("arbitrary", "arbitrary")),
        name="mods",
    )(cv, ada_w, ada_b.reshape(depth, 1, nm))


def _ev_in_body(x_ref, mod_ref, g1_ref, wm_ref, bm_ref, wkT_ref, bkT_ref, wg_ref, bg_ref, wgT_ref, bgT_ref,
                qg_ref, kg_ref, cos_ref, sin_ref,
                q_ref, k_ref, v_ref, mq_ref, mkT_ref, mv_ref, og_ref, gc_ref, gT_ref, *, mrow, d):
    x = x_ref[...]
    shift = mod_ref[mrow:mrow + 1, 0:d]
    scale = mod_ref[mrow:mrow + 1, d:2 * d]
    a = (_rms(x, g1_ref[...]) * (1.0 + scale) + shift).astype(bf16)
    z = _dot(a, wm_ref[...]) + bm_ref[...]
    cos = cos_ref[...]
    sin = sin_ref[...]
    lane = lax.broadcasted_iota(i32, cos.shape, 1)
    first_half = (lane % 64) < 32

    def qk_head(zh, gain, mult):
        zn = zh * lax.rsqrt(jnp.mean(zh * zh, axis=-1, keepdims=True) + EPS) * gain
        partner = jnp.where(first_half, pltpu.roll(zn, 96, 1), pltpu.roll(zn, 32, 1))
        return ((zn * cos + partner * sin) * mult).astype(bf16)

    for h in range(ATTN_HEADS):
        q_ref[:, h * 128:(h + 1) * 128] = qk_head(z[:, h * 128:(h + 1) * 128], qg_ref[...],
                                                   HEAD_DIM ** -0.5 * math.log2(math.e))
    for h in range(ATTN_KV_HEADS):
        k_ref[:, h * 128:(h + 1) * 128] = qk_head(z[:, 512 + h * 128:512 + (h + 1) * 128], kg_ref[...], 1.0)
    v_ref[...] = z[:, 768:1024].astype(bf16)
    mq_ref[...] = (z[:, 1024:1280] * (MLSTM_DK ** -0.5)).astype(bf16)
    mv_ref[...] = z[:, 1280:1792].astype(bf16)
    og_ref[...] = z[:, 1792:2304]
    mkT_ref[...] = (_dot_nt(wkT_ref[...], a) + bkT_ref[...]).astype(bf16)
    gc_ref[...] = _dot(a, wg_ref[...]) + bg_ref[...]
    gT_ref[...] = _dot_nt(wgT_ref[...], a) + bgT_ref[...]


def _ev_in(x, mod, g1, wts, cos, sin, mrow, tm):
    r, d = x.shape
    wm, bm, wkT, bkT, wg, bg, wgT, bgT, qg, kg = wts
    row = lambda n: pl.BlockSpec((tm, n), lambda i: (i, 0))
    col = lambda n: pl.BlockSpec((n, tm), lambda i: (0, i))
    outs = [((r, 512), bf16, row(512)), ((r, 256), bf16, row(256)), ((r, 256), bf16, row(256)),
            ((r, 256), bf16, row(256)), ((256, r), bf16, col(256)), ((r, 512), bf16, row(512)),
            ((r, 512), f32, row(512)), ((r, LANES), f32, row(LANES)), ((16, r), f32, col(16))]
    return pl.pallas_call(
        functools.partial(_ev_in_body, mrow=mrow, d=d),
        grid=(r // tm,),
        in_specs=[row(d), _full(mod.shape), _full(g1.shape), _full(wm.shape), _full(bm.shape), _full(wkT.shape),
                  _full(bkT.shape), _full(wg.shape), _full(bg.shape), _full(wgT.shape), _full(bgT.shape),
                  _full(qg.shape), _full(kg.shape), row(LANES), row(LANES)],
        out_specs=[o[2] for o in outs],
        out_shape=[jax.ShapeDtypeStruct(o[0], o[1]) for o in outs],
        compiler_params=_cp(("arbitrary",)),
        name="ev_in",
    )(x, mod, g1, wm, bm, wkT, bkT, wg, bg, wgT, bgT, qg, kg, cos, sin)


def _attn_body(q_ref, k_ref, v_ref, o_ref, m_sc, acc_sc):
    j = pl.program_id(1)
    group = ATTN_HEADS // ATTN_KV_HEADS

    @pl.when(j == 0)
    def _():
        m_sc[...] = jnp.full(m_sc.shape, -jnp.inf, f32)
        acc_sc[...] = jnp.zeros(acc_sc.shape, f32)

    def scores(g):
        q = jnp.concatenate([q_ref[:, (g * group + i) * 128:(g * group + i + 1) * 128] for i in range(group)], axis=0)
        return _dot_nt(q, k_ref[:, g * 128:(g + 1) * 128])

    pvs, zmax = [], None
    for g in range(ATTN_KV_HEADS):
        z = scores(g) - m_sc[g]
        pvs.append(_dot(jnp.exp2(z).astype(bf16), v_ref[:, g * 256:(g + 1) * 256]))
        zm = jnp.max(z)
        zmax = zm if zmax is None else jnp.maximum(zmax, zm)
    in_range = zmax <= ATTN_SLACK

    @pl.when(in_range)
    def _():
        for g in range(ATTN_KV_HEADS):
            acc_sc[g] = acc_sc[g] + pvs[g]

    @pl.when(jnp.logical_not(in_range))
    def _():
        for g in range(ATTN_KV_HEADS):
            s = scores(g)
            m_prev = m_sc[g]
            m_new = jnp.maximum(m_prev, jnp.max(s, axis=-1, keepdims=True))
            p = jnp.exp2(s - m_new).astype(bf16)
            acc_sc[g] = jnp.exp2(m_prev - m_new) * acc_sc[g] + _dot(p, v_ref[:, g * 256:(g + 1) * 256])
            m_sc[g] = m_new

    @pl.when(j == pl.num_programs(1) - 1)
    def _():
        tq = q_ref.shape[0]
        for g in range(ATTN_KV_HEADS):
            acc = acc_sc[g]
            o = (acc[:, 0:128] / acc[:, 128:256]).astype(bf16)
            for i in range(group):
                o_ref[:, (g * group + i) * 128:(g * group + i + 1) * 128] = o[i * tq:(i + 1) * tq]


def _attention(q, k, v, tq, tk):
    nq, nk = q.shape[0], k.shape[0]
    assert nq % tq == 0 and nk % tk == 0
    rows = tq * (ATTN_HEADS // ATTN_KV_HEADS)
    return pl.pallas_call(
        _attn_body,
        grid=(nq // tq, nk // tk),
        in_specs=[pl.BlockSpec((tq, 512), lambda i, j: (i, 0)),
                  pl.BlockSpec((tk, 256), lambda i, j: (j, 0)),
                  pl.BlockSpec((tk, 512), lambda i, j: (j, 0))],
        out_specs=pl.BlockSpec((tq, 512), lambda i, j: (i, 0)),
        out_shape=jax.ShapeDtypeStruct((nq, 512), bf16),
        scratch_shapes=[pltpu.VMEM((ATTN_KV_HEADS, rows, 1), f32), pltpu.VMEM((ATTN_KV_HEADS, rows, 256), f32)],
        compiler_params=_cp(("arbitrary", "arbitrary")),
        name="attention",
    )(q, k, v)


def _mlstm_body(qf_ref, kTf_ref, vf_ref, gcf_ref, gTf_ref, qb_ref, kTb_ref, vb_ref, gcb_ref, gTb_ref,
                fbr_ref, fbc_ref, c0_ref, m0_ref, hf_ref, hb_ref, c_out_ref, m_out_ref, c_sc, m_sc):
    j = pl.program_id(0)

    @pl.when(j == 0)
    def _():
        c_sc[...] = c0_ref[...]
        m_sc[...] = m0_ref[...]

    ti = lax.broadcasted_iota(i32, (CHUNK, CHUNK), 0)
    si = lax.broadcasted_iota(i32, (CHUNK, CHUNK), 1)
    ones_c = jnp.ones((CHUNK, 128), f32)
    for d in range(2):
        q_ref, kT_ref, v_ref, gc_ref, gT_ref, h_ref = ((qf_ref, kTf_ref, vf_ref, gcf_ref, gTf_ref, hf_ref) if d == 0 else
                                                      (qb_ref, kTb_ref, vb_ref, gcb_ref, gTb_ref, hb_ref))
        mask = (si <= ti) if d == 0 else (si >= ti)
        lmat = mask.astype(f32)
        umat = ((ti <= si) if d == 0 else (ti >= si)).astype(f32)
        gc = gc_ref[...]
        gT = gT_ref[...]
        fcol = _log_sigmoid(gc + fbr_ref[...])
        frow = _log_sigmoid(gT + fbc_ref[...])
        bcol = jnp.dot(lmat, fcol, precision=HIGHEST, preferred_element_type=f32)
        brow = jnp.dot(frow, umat, precision=HIGHEST, preferred_element_type=f32)
        last = CHUNK - 1 if d == 0 else 0
        for h in range(MLSTM_HEADS):
            idx = d * MLSTM_HEADS + h
            ji, jf = idx, 8 + idx
            b_t = bcol[:, jf:jf + 1]
            i_t = gc[:, ji:ji + 1]
            logd = jnp.where(mask, b_t - brow[jf:jf + 1, :] + gT[ji:ji + 1, :], -jnp.inf)
            m_old = m_sc[idx][:, 0:1]
            m_row = jnp.maximum(b_t + m_old, jnp.max(logd, axis=-1, keepdims=True))
            w_inter = jnp.exp(b_t + m_old - m_row)
            dm = jnp.exp(logd - m_row)
            qh = q_ref[:, h * MLSTM_DK:(h + 1) * MLSTM_DK]
            kTh = kT_ref[h * MLSTM_DK:(h + 1) * MLSTM_DK, :]
            vaug = jnp.concatenate([v_ref[:, h * 128:(h + 1) * 128].astype(f32), ones_c], axis=1)
            s = _dot(qh, kTh) * dm
            intra = _dot(s.astype(bf16), vaug.astype(bf16))
            c_old = c_sc[idx]
            inter = _dot(qh, c_old.astype(bf16))
            num = intra[:, 0:128] + w_inter * inter[:, 0:128]
            den = intra[:, 128:129] + w_inter * inter[:, 128:129]
            h_ref[:, h * 128:(h + 1) * 128] = num / jnp.maximum(jnp.abs(den), jnp.exp(-m_row))
            b_last = b_t[last:last + 1, :]
            lw = b_last - b_t + i_t
            m_new = jnp.maximum(b_last + m_old, jnp.max(lw, axis=0, keepdims=True))
            w = jnp.exp(lw - m_new)
            decay = jnp.exp(b_last + m_old - m_new)
            c_sc[idx] = decay * c_old + _dot(kTh, (w * vaug).astype(bf16))
            m_sc[idx] = jnp.broadcast_to(m_new, (1, 128))

    @pl.when(j == pl.num_programs(0) - 1)
    def _():
        c_out_ref[...] = c_sc[...]
        m_out_ref[...] = m_sc[...]


def _mlstm(mq, mkT, mv, gc, gT, fbr, fbc, c0, m0):
    n = mq.shape[0]
    nc = n // CHUNK
    fwd = lambda j: j
    bwd = lambda j: nc - 1 - j

    def specs(ix):
        return [pl.BlockSpec((CHUNK, 256), lambda j: (ix(j), 0)),
                pl.BlockSpec((256, CHUNK), lambda j: (0, ix(j))),
                pl.BlockSpec((CHUNK, 512), lambda j: (ix(j), 0)),
                pl.BlockSpec((CHUNK, LANES), lambda j: (ix(j), 0)),
                pl.BlockSpec((16, CHUNK), lambda j: (0, ix(j)))]

    nh = 2 * MLSTM_HEADS
    return pl.pallas_call(
        _mlstm_body,
        grid=(nc,),
        in_specs=specs(fwd) + specs(bwd) + [_full(fbr.shape), _full(fbc.shape), _full(c0.shape), _full(m0.shape)],
        out_specs=[pl.BlockSpec((CHUNK, 512), lambda j: (fwd(j), 0)),
                   pl.BlockSpec((CHUNK, 512), lambda j: (bwd(j), 0)),
                   _full(c0.shape), _full(m0.shape)],
        out_shape=[jax.ShapeDtypeStruct((n, 512), f32), jax.ShapeDtypeStruct((n, 512), f32),
                   jax.ShapeDtypeStruct(c0.shape, f32), jax.ShapeDtypeStruct(m0.shape, f32)],
        scratch_shapes=[pltpu.VMEM((nh, MLSTM_DK, 256), f32), pltpu.VMEM((nh, 1, 128), f32)],
        compiler_params=_cp(("arbitrary",)),
        name="mlstm",
    )(mq, mkT, mv, gc, gT, mq, mkT, mv, gc, gT, fbr, fbc, c0, m0)


def _post_block(x, y, mod_ref, g2_ref, rw_ref, rb_ref, run_sc, h_ref, t_ref, meta_ref, cnt_ref, d):
    tm = x.shape[0]
    i = pl.program_id(0)

    @pl.when(i == 0)
    def _():
        run_sc[...] = jnp.zeros(run_sc.shape, f32)

    h1 = x + mod_ref[0:1, 2 * d:3 * d] * y
    h_ref[...] = h1
    t = _rms(h1, g2_ref[...]) * (1.0 + mod_ref[0:1, 4 * d:5 * d]) + mod_ref[0:1, 3 * d:4 * d]
    for j in range(d // LANES):
        t_ref[pl.ds(j, tm, stride=d // LANES), :] = t[:, j * LANES:(j + 1) * LANES]
    t_hi = t.astype(bf16)
    t_lo = (t - t_hi.astype(f32)).astype(bf16)
    logits = _dot(t_hi, rw_ref[0]) + (_dot(t_hi, rw_ref[1]) + _dot(t_lo, rw_ref[0])) + rb_ref[...]
    lane = lax.broadcasted_iota(i32, (tm, LANES), 1)
    lanef = lane.astype(f32)
    neg = jnp.float32(-jnp.inf)
    cur = jnp.where(lane < N_EXPERTS, logits, neg)
    tops, ids, hots = [], [], []
    for _ in range(TOP_K):
        mx = jnp.max(cur, axis=-1, keepdims=True)
        idx = jnp.min(jnp.where(cur == mx, lanef, 1e9), axis=-1, keepdims=True)
        hot = lanef == idx
        cur = jnp.where(hot, neg, cur)
        tops.append(mx)
        ids.append(idx)
        hots.append(hot)
    es = [jnp.exp(tk - tops[0]) for tk in tops]
    den = es[0] + es[1] + es[2] + es[3]
    cnt = jnp.zeros((tm, LANES), f32)
    for hot in hots:
        cnt = cnt + hot.astype(f32)
    ri = lax.broadcasted_iota(i32, (tm, tm), 0)
    ci = lax.broadcasted_iota(i32, (tm, tm), 1)
    strict = (ci < ri).astype(bf16)
    before = _dot(strict, cnt.astype(bf16)) + run_sc[...]
    meta = jnp.zeros((tm, LANES), f32)
    for k in range(TOP_K):
        rank = jnp.sum(jnp.where(hots[k], before, 0.0), axis=-1, keepdims=True)
        meta = jnp.where(lane == k, ids[k], meta)
        meta = jnp.where(lane == TOP_K + k, rank, meta)
        meta = jnp.where(lane == 2 * TOP_K + k, es[k] / den, meta)
    meta_ref[...] = meta
    run_sc[...] = run_sc[...] + jnp.sum(cnt, axis=0, keepdims=True)

    @pl.when(i == pl.num_programs(0) - 1)
    def _():
        cnt_ref[...] = jnp.broadcast_to(run_sc[...], cnt_ref.shape).astype(i32)


def _post_specs(n, d, tm):
    row = lambda w: pl.BlockSpec((tm, w), lambda i: (i, 0))
    out_specs = [row(d), pl.BlockSpec((tm * (d // LANES), LANES), lambda i: (i, 0)), row(LANES), _full((8, LANES))]
    out_shape = [jax.ShapeDtypeStruct((n, d), f32), jax.ShapeDtypeStruct((n * (d // LANES), LANES), f32),
                 jax.ShapeDtypeStruct((n, LANES), f32), jax.ShapeDtypeStruct((8, LANES), i32)]
    return out_specs, out_shape


def _ev_out_body(att_ref, hf_ref, hb_ref, og_ref, x_ref, mod_ref, hg_ref, w_ref, b_ref, g2_ref, rw_ref, rb_ref,
                 h_ref, t_ref, meta_ref, cnt_ref, run_sc, *, d):
    hs = hf_ref[...] + hb_ref[...]
    og = og_ref[...]
    parts = [att_ref[...]]
    for h in range(MLSTM_HEADS):
        sl = slice(h * 128, (h + 1) * 128)
        hh = hs[:, sl]
        hn = hh * lax.rsqrt(jnp.mean(hh * hh, axis=-1, keepdims=True) + EPS) * hg_ref[:, sl]
        parts.append((jax.nn.sigmoid(og[:, sl]) * hn).astype(bf16))
    y = _dot(jnp.concatenate(parts, axis=1), w_ref[...]) + b_ref[...]
    _post_block(x_ref[...], y, mod_ref, g2_ref, rw_ref, rb_ref, run_sc, h_ref, t_ref, meta_ref, cnt_ref, d)


def _ev_out(att, hf, hb, og, x, mod, hg, w, b, g2, rw, rb, tm):
    n, d = x.shape
    row = lambda wd: pl.BlockSpec((tm, wd), lambda i: (i, 0))
    out_specs, out_shape = _post_specs(n, d, tm)
    return pl.pallas_call(
        functools.partial(_ev_out_body, d=d),
        grid=(n // tm,),
        in_specs=[row(512), row(512), row(512), row(512), row(d), _full(mod.shape), _full(hg.shape), _full(w.shape),
                  _full(b.shape), _full(g2.shape), _full(rw.shape), _full(rb.shape)],
        out_specs=out_specs, out_shape=out_shape,
        scratch_shapes=[pltpu.VMEM((1, LANES), f32)],
        compiler_params=_cp(("arbitrary",)),
        name="ev_out",
    )(att, hf, hb, og, x, mod, hg, w, b, g2, rw, rb)


ROW_UNROLL = 8


TOKEN_ROWS = 8


def _dispatch_body(dest_ref, t_ref, buf_ref, sem):
    tm = t_ref.shape[0] // TOKEN_ROWS

    def issue(rb, c):
        base = pl.multiple_of(rb * (ROW_UNROLL * TOKEN_ROWS), ROW_UNROLL * TOKEN_ROWS)
        for u in range(ROW_UNROLL):
            src = t_ref.at[pl.ds(base + u * TOKEN_ROWS, TOKEN_ROWS)]
            for k in range(TOP_K):
                dest = pl.multiple_of(dest_ref[(rb * ROW_UNROLL + u) * TOP_K + k], TOKEN_ROWS)
                pltpu.make_async_copy(src, buf_ref.at[pl.ds(dest, TOKEN_ROWS)], sem).start(priority=k % 2)
        return c

    lax.fori_loop(0, tm // ROW_UNROLL, issue, 0)
    for k in range(TOP_K):
        pltpu.make_async_copy(t_ref, buf_ref.at[pl.ds(0, tm * TOKEN_ROWS)], sem).wait()


def _dispatch(dest, t, tm):
    rows, l = t.shape
    return pl.pallas_call(
        _dispatch_body,
        grid=(rows // (tm * TOKEN_ROWS),),
        in_specs=[pl.BlockSpec((tm * TOP_K,), lambda i: (i,), memory_space=pltpu.SMEM),
                  pl.BlockSpec((tm * TOKEN_ROWS, l), lambda i: (i, 0))],
        out_specs=pl.BlockSpec(memory_space=pl.ANY),
        out_shape=jax.ShapeDtypeStruct((rows * TOP_K, l), f32),
        scratch_shapes=[pltpu.SemaphoreType.DMA(())],
        compiler_params=_cp(("arbitrary",)),
        name="moe_dispatch",
    )(dest, t)


def _expert_body(blk_ref, exp_ref, lo_ref, act_ref, chg_ref, x_ref, wgu_ref, bgu_ref, wd_ref, bd_ref, o_ref,
                 wgu_sc, wd_sc, *, de):
    i = pl.program_id(0)

    @pl.when(chg_ref[i] == 1)
    def _():
        wgu_sc[...] = wgu_ref[0, 0].astype(bf16)
        wd_sc[...] = wd_ref[0, 0].astype(bf16)

    @pl.when(act_ref[i] == 1)
    def _():
        ns = TOKEN_ROWS
        x = jnp.concatenate([x_ref[pl.ds(j, MOE_BM, stride=ns), :] for j in range(ns)], axis=1).astype(bf16)
        gu = _dot(x, wgu_sc[...]) + bgu_ref[0, 0]
        g = jnp.minimum(gu[:, :de], SWIGLU_LIMIT)
        u = jnp.clip(gu[:, de:], -SWIGLU_LIMIT, SWIGLU_LIMIT)
        a = g * jax.nn.sigmoid(SWIGLU_ALPHA * g) * (u + 1.0)
        y = _dot(a.astype(bf16), wd_sc[...]) + bd_ref[0, 0]
        lo = lo_ref[i]

        @pl.when(lo == 0)
        def _():
            for j in range(ns):
                o_ref[pl.ds(j, MOE_BM, stride=ns), :] = y[:, j * LANES:(j + 1) * LANES]

        @pl.when(lo > 0)
        def _():
            keep = lax.broadcasted_iota(i32, (y.shape[0], LANES), 0) >= lo
            for j in range(ns):
                rows = pl.ds(j, MOE_BM, stride=ns)
                o_ref[rows, :] = jnp.where(keep, y[:, j * LANES:(j + 1) * LANES], o_ref[rows, :])


def _experts(sched, buf, wgu, bgu, wd, bd, layer):
    nl, ne, d, de2 = wgu.shape
    assert d == TOKEN_ROWS * LANES
    de = de2 // 2
    n_items = sched[0].shape[0]
    wmap = lambda i, b, e, l, a, c: (layer, e[i], 0, 0)
    rows = pl.BlockSpec((MOE_BM * TOKEN_ROWS, LANES), lambda i, b, e, l, a, c: (b[i], 0))
    return pl.pallas_call(
        functools.partial(_expert_body, de=de),
        grid_spec=pltpu.PrefetchScalarGridSpec(
            num_scalar_prefetch=5,
            grid=(n_items,),
            in_specs=[rows,
                      pl.BlockSpec((1, 1, d, de2), wmap),
                      pl.BlockSpec((1, 1, 1, de2), wmap),
                      pl.BlockSpec((1, 1, de, d), wmap),
                      pl.BlockSpec((1, 1, 1, d), wmap)],
            out_specs=rows,
            scratch_shapes=[pltpu.VMEM((d, de2), bf16), pltpu.VMEM((de, d), bf16)]),
        out_shape=jax.ShapeDtypeStruct(buf.shape, f32),
        compiler_params=_cp(("arbitrary",)),
        name="moe_experts",
    )(*sched, buf, wgu, bgu.reshape(nl, ne, 1, de2), wd, bd.reshape(nl, ne, 1, d))


def _combine_body(dest_ref, eo_ref, meta_ref, h_ref, mod_ref, fg_ref, o_ref, stage, sem, *, d, final):
    tm = h_ref.shape[0]
    ns = TOKEN_ROWS

    def issue(rb, c):
        base = pl.multiple_of(rb * (ROW_UNROLL * ns), ROW_UNROLL * ns)
        for u in range(ROW_UNROLL):
            for k in range(TOP_K):
                src = pl.multiple_of(dest_ref[(rb * ROW_UNROLL + u) * TOP_K + k], ns)
                pltpu.make_async_copy(eo_ref.at[pl.ds(src, ns)], stage.at[pl.ds(k * tm * ns + base + u * ns, ns)],
                                      sem).start(priority=k % 2)
        return c

    lax.fori_loop(0, tm // ROW_UNROLL, issue, 0)
    for k in range(TOP_K):
        pltpu.make_async_copy(eo_ref.at[pl.ds(0, tm * ns)], stage.at[pl.ds(k * tm * ns, tm * ns)], sem).wait()
    meta = meta_ref[...]
    f = None
    for k in range(TOP_K):
        rows = jnp.concatenate([stage[pl.ds(k * tm * ns + j, tm, stride=ns), :] for j in range(ns)], axis=1)
        term = meta[:, 2 * TOP_K + k:2 * TOP_K + k + 1] * rows
        f = term if f is None else f + term
    h = h_ref[...] + mod_ref[0:1, 5 * d:6 * d] * f
    if final:
        h = _rms(h, fg_ref[...])
    o_ref[...] = h


def _combine(dest, eo, meta, h, mod, fg, tm, final):
    n, d = h.shape
    return pl.pallas_call(
        functools.partial(_combine_body, d=d, final=final),
        grid=(n // tm,),
        in_specs=[pl.BlockSpec((tm * TOP_K,), lambda i: (i,), memory_space=pltpu.SMEM),
                  pl.BlockSpec(memory_space=pl.ANY),
                  pl.BlockSpec((tm, LANES), lambda i: (i, 0)),
                  pl.BlockSpec((tm, d), lambda i: (i, 0)),
                  _full(mod.shape), _full(fg.shape)],
        out_specs=pl.BlockSpec((tm, d), lambda i: (i, 0)),
        out_shape=jax.ShapeDtypeStruct((n, d), f32),
        scratch_shapes=[pltpu.VMEM((TOP_K * tm * TOKEN_ROWS, LANES), f32), pltpu.SemaphoreType.DMA(())],
        compiler_params=_cp(("arbitrary",)),
        name="moe_combine",
    )(dest, eo, meta, h, mod, fg)


def _moe(t, meta, counts, h, mod, fg, wgu, bgu, wd, bd, layer, final):
    n, d = h.shape
    m = n * TOP_K
    nblk = m // MOE_BM
    n_items = nblk + N_EXPERTS - 1
    counts = counts.astype(i32)
    ends = jnp.cumsum(counts)
    starts = ends - counts
    first_blk = starts // MOE_BM
    last_blk = jnp.maximum(ends - 1, starts) // MOE_BM
    per = jnp.where(counts > 0, last_blk - first_blk + 1, 0)
    cum = jnp.cumsum(per)
    off = cum - per
    total = cum[-1]
    it = jnp.minimum(jnp.arange(n_items, dtype=i32), jnp.maximum(total - 1, 0))
    e_it = jnp.minimum(jnp.sum((cum[None, :] <= it[:, None]).astype(i32), axis=1), N_EXPERTS - 1)
    eids = jnp.arange(N_EXPERTS, dtype=i32)[None, :]
    pick = lambda v: jnp.sum(jnp.where(e_it[:, None] == eids, v[None, :], 0), axis=1)
    blk_it = pick(first_blk) + it - pick(off)
    lo_it = jnp.maximum(pick(starts) - blk_it * MOE_BM, 0)
    active = (jnp.arange(n_items, dtype=i32) < total).astype(i32)
    changed = jnp.concatenate([jnp.ones((1,), i32), (e_it[1:] != e_it[:-1]).astype(i32)])
    sched = (blk_it.astype(i32), e_it.astype(i32), lo_it.astype(i32), active, changed)
    ids = meta[:, 0:TOP_K].astype(i32)
    ranks = meta[:, TOP_K:2 * TOP_K].astype(i32)
    dest = (jnp.sum(jnp.where(ids[:, :, None] == eids[None], starts[None, None, :], 0), axis=-1) + ranks).reshape(-1)
    dest = dest * TOKEN_ROWS
    buf = _dispatch(dest, t, 256)
    eo = _experts(sched, buf, wgu, bgu, wd, bd, layer)
    return _combine(dest, eo, meta, h, mod, fg, 256, final)


def _rope_tables(n):
    rows = n // GRID_W
    row = np.repeat(np.arange(rows), GRID_W)
    col = np.tile(np.arange(GRID_W), rows)
    pos = np.stack([row, col], axis=-1).astype(np.float64)
    inv_freq = ROPE_THETA ** (-np.arange(ROPE_PAIRS, dtype=np.float64) / ROPE_PAIRS)
    ang = pos[:, :, None] * inv_freq
    c, s = np.cos(ang), np.sin(ang)
    cos = np.concatenate([c[:, 0], c[:, 0], c[:, 1], c[:, 1]], axis=-1).astype(np.float32)
    sin = np.concatenate([-s[:, 0], s[:, 0], -s[:, 1], s[:, 1]], axis=-1).astype(np.float32)
    return jnp.asarray(cos), jnp.asarray(sin)


def _router_params(router_w, router_b):
    d = router_w.shape[0]
    rw = jnp.zeros((d, LANES), f32).at[:, :N_EXPERTS].set(router_w)
    rb = jnp.zeros((1, LANES), f32).at[0, :N_EXPERTS].set(router_b)
    rw_hi = rw.astype(bf16)
    rw_lo = (rw - rw_hi.astype(f32)).astype(bf16)
    return jnp.stack([rw_hi, rw_lo]), rb


def _even_layer(x, ctx, mod, g1, g2, w_in, b_in, q_gain, k_gain, f_bias, h_gain, w_out, b_out, router_w, router_b):
    n, d = x.shape
    nctx = ctx.shape[0]
    cols = lambda a, b: w_in[:, a:b]
    wm = jnp.concatenate([cols(0, 1280), cols(1536, 2048), cols(2064, 2576)], axis=1).astype(bf16)
    bm = jnp.concatenate([b_in[0:1280], b_in[1536:2048], b_in[2064:2576]])[None, :]
    wkT = cols(1280, 1536).T.astype(bf16)
    bkT = b_in[1280:1536][:, None]
    wgate = cols(2048, 2064)
    wg = jnp.zeros((d, LANES), f32).at[:, :16].set(wgate).astype(bf16)
    bg = jnp.zeros((1, LANES), f32).at[0, :16].set(b_in[2048:2064])
    wgT = wgate.T.astype(bf16)
    bgT = b_in[2048:2064][:, None]
    wts = (wm, bm, wkT, bkT, wg, bg, wgT, bgT, q_gain[None, :], k_gain[None, :])
    cos, sin = _rope_tables(n)
    lat = _ev_in(x, mod, g1, wts, cos, sin, 0, min(512, n))
    cx = _ev_in(ctx, mod, g1, wts, jnp.ones((nctx, LANES), f32), jnp.zeros((nctx, LANES), f32), 1, nctx)
    q, k, v, mq, mkT, mv, og, gc, gT = lat
    _, kc, vc, mqc, mkTc, mvc, _, gcc, gTc = cx

    v_all = jnp.concatenate([v, vc], axis=0)
    ones = jnp.ones((n + nctx, 128), bf16)
    v_aug = jnp.concatenate([v_all[:, 0:128], ones, v_all[:, 128:256], ones], axis=1)
    att = _attention(q, jnp.concatenate([k, kc], axis=0), v_aug, min(512, n), 1280)

    fb = f_bias.reshape(-1)
    fbr = jnp.zeros((1, LANES), f32).at[0, 8:16].set(fb)
    fbc = jnp.zeros((16, LANES), f32).at[8:16, :].set(jnp.broadcast_to(fb[:, None], (8, LANES)))
    nh = 2 * MLSTM_HEADS
    c0 = jnp.zeros((nh, MLSTM_DK, 256), f32)
    m0 = jnp.zeros((nh, 1, 128), f32)
    _, _, c1, m1 = _mlstm(mqc, mkTc, mvc, gcc, gTc, fbr, fbc, c0, m0)
    hf, hb, _, _ = _mlstm(mq, mkT, mv, gc, gT, fbr, fbc, c1, m1)

    rw, rb = _router_params(router_w, router_b)
    return _ev_out(att, hf, hb, og, x, mod, h_gain[None, :], w_out.astype(bf16), b_out[None, :], g2, rw, rb,
                   min(512, n))


FFT_N2 = 128
CH_TILE = 8


def _ct_shape(ch, nblocks):
    return jax.ShapeDtypeStruct((ch // CH_TILE, nblocks * CH_TILE, FFT_N2), f32)


def _ct_block(ch, per):
    return pl.BlockSpec((ch // CH_TILE, per * CH_TILE, FFT_N2), lambda i: (0, i, 0))


def _ct_slab(nblocks):
    return pl.BlockSpec((1, nblocks * CH_TILE, FFT_N2), lambda i: (i, 0, 0))


def _ct_channel(ref, c, nblocks):
    return ref.at[0], pl.ds(c, nblocks, stride=CH_TILE)


def _hy_in_body(x_ref, mod_ref, g1_ref, wT_ref, b_ref, zT_ref, *, d):
    a = (_rms(x_ref[...], g1_ref[...]) * (1.0 + mod_ref[0:1, d:2 * d]) + mod_ref[0:1, 0:d]).astype(bf16)
    zT_ref[...] = _dot_nt(wT_ref[...], a) + b_ref[...]


def _hy_in(x, mod, g1, wT, b, tm):
    n, d = x.shape
    c3 = wT.shape[0]
    return pl.pallas_call(
        functools.partial(_hy_in_body, d=d),
        grid=(n // tm,),
        in_specs=[pl.BlockSpec((tm, d), lambda i: (i, 0)), _full(mod.shape), _full(g1.shape), _full(wT.shape),
                  _full(b.shape)],
        out_specs=pl.BlockSpec((c3, tm), lambda i: (0, i)),
        out_shape=jax.ShapeDtypeStruct((c3, n), f32),
        compiler_params=_cp(("arbitrary",)),
        name="hy_in",
    )(x, mod, g1, wT, b)


def _hy_conv_body(z_ref, zp_ref, zn_ref, cw_ref, vp_ref, x0_ref, *, ch):
    i = pl.program_id(0)
    tt = z_ref.shape[1]
    has_prev = (i > 0).astype(f32)
    has_next = (i < pl.num_programs(0) - 1).astype(f32)
    lane = lax.broadcasted_iota(i32, (ch, tt), 1)

    def conv(g):
        rows = slice(g * ch, (g + 1) * ch)
        z = z_ref[rows, :]
        pcol = zp_ref[rows, FFT_N2 - 1:FFT_N2] * has_prev
        ncol = zn_ref[rows, 0:1] * has_next
        zm1 = jnp.where(lane == 0, pcol, pltpu.roll(z, 1, 1))
        zp1 = jnp.where(lane == tt - 1, ncol, pltpu.roll(z, tt - 1, 1))
        cw = cw_ref[rows, :]
        return cw[:, 0:1] * zm1 + cw[:, 1:2] * z + cw[:, 2:3] * zp1 + cw[:, 3:4]

    x0 = conv(0)
    vp = conv(2) * conv(1)
    for j in range(tt // FFT_N2):
        rows = slice(j * CH_TILE, (j + 1) * CH_TILE)
        x0_ref[:, rows, :] = x0[:, j * FFT_N2:(j + 1) * FFT_N2].reshape(ch // CH_TILE, CH_TILE, FFT_N2)
        vp_ref[:, rows, :] = vp[:, j * FFT_N2:(j + 1) * FFT_N2].reshape(ch // CH_TILE, CH_TILE, FFT_N2)


def _hy_conv(zT, cw, tt):
    c3, n = zT.shape
    ch = c3 // 3
    nb = n // FFT_N2
    per = tt // FFT_N2
    return pl.pallas_call(
        functools.partial(_hy_conv_body, ch=ch),
        grid=(n // tt,),
        in_specs=[pl.BlockSpec((c3, tt), lambda i: (0, i)),
                  pl.BlockSpec((c3, FFT_N2), lambda i: (0, jnp.maximum(i * per - 1, 0))),
                  pl.BlockSpec((c3, FFT_N2), lambda i: (0, jnp.minimum(i * per + per, nb - 1))),
                  _full(cw.shape)],
        out_specs=[_ct_block(ch, per), _ct_block(ch, per)],
        out_shape=[_ct_shape(ch, nb), _ct_shape(ch, nb)],
        compiler_params=_cp(("arbitrary",)),
        name="hy_conv",
    )(zT, zT, zT, cw)


def _hy_filt_body(zz_ref, w1_ref, b1_ref, fr_ref, w2_ref, b2_ref, w3_ref, b3_ref, w4T_ref, dl_ref, tr_ref, mr_ref,
                  o_ref):
    hdot = lambda a, b: jnp.dot(a, b, precision=HIGHEST, preferred_element_type=f32)
    fr = fr_ref[...]
    h = jnp.sin(fr * (hdot(w1_ref[...], zz_ref[...]) + b1_ref[...]))
    h = jnp.sin(fr * (hdot(w2_ref[...], h) + b2_ref[...]))
    h = jnp.sin(fr * (hdot(w3_ref[...], h) + b3_ref[...]))
    taps = _dot(w4T_ref[0], h.astype(bf16))
    ch = taps.shape[0]
    dl = dl_ref[...]
    for j in range(tr_ref.shape[0]):
        blk = taps[:, j * FFT_N2:(j + 1) * FFT_N2] * jnp.exp(-dl * tr_ref[j]) * mr_ref[j]
        o_ref[:, j * CH_TILE:(j + 1) * CH_TILE, :] = blk.reshape(ch // CH_TILE, CH_TILE, FFT_N2)


def _hy_filt(zz, w1, b1, fr, w2, b2, w3, b3, w4T, deltas, trow, mrow):
    n2l = zz.shape[1]
    nb2 = n2l // FFT_N2
    ch = w4T.shape[1]
    per = 4 if nb2 % 8 == 0 else 1
    steps = nb2 // per
    return pl.pallas_call(
        _hy_filt_body,
        grid=(steps,),
        in_specs=[pl.BlockSpec((LANES, per * FFT_N2), lambda i: (0, i)), _full(w1.shape), _full(b1.shape),
                  _full(fr.shape), _full(w2.shape), _full(b2.shape), _full(w3.shape), _full(b3.shape),
                  pl.BlockSpec((1,) + w4T.shape[1:], lambda i: ((i >= steps // 2).astype(i32), 0, 0)),
                  _full(deltas.shape),
                  pl.BlockSpec((per, 1, FFT_N2), lambda i: (i, 0, 0)),
                  pl.BlockSpec((per, 1, FFT_N2), lambda i: (i, 0, 0))],
        out_specs=_ct_block(ch, per),
        out_shape=_ct_shape(ch, nb2),
        compiler_params=_cp(("arbitrary",)),
        name="hy_filt",
    )(zz, w1, b1, fr, w2, b2, w3, b3, w4T, deltas, trow, mrow)


def _dft_stage12(x_ref, f1_ref, tr_ref, ti_ref, r2_ref):
    ct = CH_TILE
    n1 = f1_ref.shape[0] // 2
    n_in = f1_ref.shape[1]
    xs = []
    for c in range(ct):
        slab, rows = _ct_channel(x_ref, c, n_in)
        xs.append(slab[rows, :])
    x = jnp.concatenate(xs, axis=1).astype(bf16)
    a = _dot(f1_ref[...], x)
    tr, ti = tr_ref[...], ti_ref[...]
    lhs = []
    for c in range(ct):
        ar = a[0:n1, c * FFT_N2:(c + 1) * FFT_N2]
        ai = a[n1:2 * n1, c * FFT_N2:(c + 1) * FFT_N2]
        lhs.append(jnp.concatenate([ar * tr - ai * ti, ar * ti + ai * tr], axis=1).astype(bf16))
    xf = _dot(jnp.concatenate(lhs, axis=0), r2_ref[...])
    return [(xf[c * n1:(c + 1) * n1, 0:FFT_N2], xf[c * n1:(c + 1) * n1, FFT_N2:2 * FFT_N2]) for c in range(ct)]


def _hy_gfft_body(t_ref, f1_ref, tr_ref, ti_ref, r2_ref, gr_ref, gi_ref):
    n1 = f1_ref.shape[0] // 2
    for c, (xr, xi) in enumerate(_dft_stage12(t_ref, f1_ref, tr_ref, ti_ref, r2_ref)):
        slab, rows = _ct_channel(gr_ref, c, n1)
        slab[rows, :] = xr
        slab, rows = _ct_channel(gi_ref, c, n1)
        slab[rows, :] = xi


def _hy_gfft(taps3, f1, tr, ti, r2):
    n1 = f1.shape[1]
    blk = _ct_slab(n1)
    return pl.pallas_call(
        _hy_gfft_body,
        grid=(taps3.shape[0],),
        in_specs=[blk, _full(f1.shape), _full(tr.shape), _full(ti.shape), _full(r2.shape)],
        out_specs=[blk, blk],
        out_shape=[jax.ShapeDtypeStruct(taps3.shape, f32), jax.ShapeDtypeStruct(taps3.shape, f32)],
        compiler_params=_cp(("arbitrary",)),
        name="hy_gfft",
    )(taps3, f1, tr, ti, r2)


def _hy_fftconv_body(v_ref, gr_ref, gi_ref, f1_ref, tr_ref, ti_ref, r2_ref, r2i_ref, i2_ref, y_ref):
    ct = CH_TILE
    n1 = f1_ref.shape[0] // 2
    n_out = i2_ref.shape[0]
    tr, ti = tr_ref[...], ti_ref[...]
    lhs = []
    for c, (xr, xi) in enumerate(_dft_stage12(v_ref, f1_ref, tr_ref, ti_ref, r2_ref)):
        slab, rows = _ct_channel(gr_ref, c, n1)
        gr = slab[rows, :]
        slab, rows = _ct_channel(gi_ref, c, n1)
        gi = slab[rows, :]
        lhs.append(jnp.concatenate([xr * gr - xi * gi, xr * gi + xi * gr], axis=1).astype(bf16))
    b = _dot(jnp.concatenate(lhs, axis=0), r2i_ref[...])
    brs, bis = [], []
    for c in range(ct):
        br = b[c * n1:(c + 1) * n1, 0:FFT_N2]
        bi = b[c * n1:(c + 1) * n1, FFT_N2:2 * FFT_N2]
        brs.append(br * tr + bi * ti)
        bis.append(bi * tr - br * ti)
    rhs = jnp.concatenate([jnp.concatenate(brs, axis=1), jnp.concatenate(bis, axis=1)], axis=0).astype(bf16)
    y = _dot(i2_ref[...], rhs)
    for c in range(ct):
        slab, rows = _ct_channel(y_ref, c, n_out)
        slab[rows, :] = y[:, c * FFT_N2:(c + 1) * FFT_N2]


def _hy_fftconv(vp3, gr, gi, f1, tr, ti, r2, r2i, i2):
    vblk = _ct_slab(f1.shape[1])
    gblk = _ct_slab(f1.shape[0] // 2)
    return pl.pallas_call(
        _hy_fftconv_body,
        grid=(vp3.shape[0],),
        in_specs=[vblk, gblk, gblk, _full(f1.shape), _full(tr.shape), _full(ti.shape), _full(r2.shape),
                  _full(r2i.shape), _full(i2.shape)],
        out_specs=vblk,
        out_shape=jax.ShapeDtypeStruct(vp3.shape, f32),
        compiler_params=_cp(("arbitrary",)),
        name="hy_fftconv",
    )(vp3, gr, gi, f1, tr, ti, r2, r2i, i2)


def _hy_out_body(y_ref, vp_ref, x0_ref, sk_ref, w_ref, b_ref, x_ref, mod_ref, g2_ref, rw_ref, rb_ref,
                 h_ref, t_ref, meta_ref, cnt_ref, run_sc, *, d):
    sk = sk_ref[...]
    ch = sk.shape[0]

    def block(ref, j):
        return ref[:, j * CH_TILE:(j + 1) * CH_TILE, :].reshape(ch, FFT_N2)

    g = jnp.concatenate([((block(y_ref, j) + block(vp_ref, j) * sk) * block(x0_ref, j)).astype(bf16)
                         for j in range(y_ref.shape[1] // CH_TILE)], axis=1)
    y = _dot_tn(g, w_ref[...]) + b_ref[...]
    _post_block(x_ref[...], y, mod_ref, g2_ref, rw_ref, rb_ref, run_sc, h_ref, t_ref, meta_ref, cnt_ref, d)


def _hy_out(y3, vp3, x03, skip, w, b, x, mod, g2, rw, rb, tm):
    n, d = x.shape
    blk = _ct_block(skip.shape[0], tm // FFT_N2)
    out_specs, out_shape = _post_specs(n, d, tm)
    return pl.pallas_call(
        functools.partial(_hy_out_body, d=d),
        grid=(n // tm,),
        in_specs=[blk, blk, blk, _full(skip.shape), _full(w.shape), _full(b.shape),
                  pl.BlockSpec((tm, d), lambda i: (i, 0)), _full(mod.shape), _full(g2.shape), _full(rw.shape),
                  _full(rb.shape)],
        out_specs=out_specs, out_shape=out_shape,
        scratch_shapes=[pltpu.VMEM((1, LANES), f32)],
        compiler_params=_cp(("arbitrary",)),
        name="hy_out",
    )(y3, vp3, x03, skip, w, b, x, mod, g2, rw, rb)


def _dft_constants(n):
    nn = 2 * n
    n1 = nn // FFT_N2
    k1 = np.arange(n1)[:, None]
    phi = 2.0 * np.pi * (k1 * np.arange(n1)[None, :] % n1) / n1
    f1 = np.concatenate([np.cos(phi), -np.sin(phi)], axis=0)
    th = 2.0 * np.pi * (k1 * np.arange(FFT_N2)[None, :]) / nn
    tr, ti = np.cos(th), -np.sin(th)
    ps = 2.0 * np.pi * (np.arange(FFT_N2)[:, None] * np.arange(FFT_N2)[None, :] % FFT_N2) / FFT_N2
    c2, s2 = np.cos(ps), np.sin(ps)
    r2 = np.block([[c2, -s2], [s2, c2]])
    r2i = np.block([[c2, s2], [-s2, c2]])
    phi_i = phi[: n1 // 2, :]
    i2 = np.concatenate([np.cos(phi_i), -np.sin(phi_i)], axis=1) / nn
    as32 = lambda a: jnp.asarray(a.astype(np.float32))
    return dict(f1=as32(f1).astype(bf16), f1v=as32(f1[:, : n1 // 2]).astype(bf16), tr=as32(tr), ti=as32(ti),
                r2=as32(r2).astype(bf16), r2i=as32(r2i).astype(bf16), i2=as32(i2).astype(bf16))


def _filter_tables(n):
    t = np.linspace(0.0, 1.0, n)[:, None]
    w = (2.0 * np.pi / n) * np.arange(n)[:, None]
    f = np.linspace(1e-4, FILTER_BANDS - 1, FILTER_BANDS)
    z = np.concatenate([t, np.cos(f * w), -np.sin(f * w)], axis=-1)
    pos2 = np.arange(2 * n)
    pos = np.where(pos2 < n, pos2, np.minimum(2 * n - pos2, n - 1))
    zz = np.zeros((LANES, 2 * n), np.float32)
    zz[:FILTER_EMB, :] = z[pos].T
    trow = t[pos, 0].astype(np.float32).reshape(2 * n // FFT_N2, 1, FFT_N2)
    mrow = (pos2 != n).astype(np.float32).reshape(2 * n // FFT_N2, 1, FFT_N2)
    return jnp.asarray(zz), jnp.asarray(trow), jnp.asarray(mrow)


def _odd_layer(x, mod, g1, g2, w_in, b_in, conv_w, conv_b, fw1, fb1, ffreq, fw2, fb2, fw3, fb3, fw4, skip, w_out,
               b_out, router_w, router_b):
    n, d = x.shape
    ch = w_out.shape[0]
    tm = min(512, n)
    zT = _hy_in(x, mod, g1, w_in.T.astype(bf16), b_in[:, None], tm)
    cw = jnp.zeros((3 * ch, LANES), f32).at[:, 0:3].set(conv_w.T).at[:, 3].set(conv_b)
    vp3, x03 = _hy_conv(zT, cw, tm)

    zz, trow, mrow = _filter_tables(n)
    w1T = jnp.zeros((fw1.shape[1], LANES), f32).at[:, :FILTER_EMB].set(fw1.T)
    w4T = jnp.stack([fw4[:, :ch].T, fw4[:, ch:].T]).astype(bf16)
    deltas = jnp.abs(jnp.linspace(math.log(DECAY_TARGET) / SLOW_DECAY, math.log(DECAY_TARGET) / FAST_DECAY, ch,
                                  dtype=f32))[:, None]
    taps3 = _hy_filt(zz, w1T, fb1[:, None], ffreq[:, None], fw2.T, fb2[:, None], fw3.T, fb3[:, None], w4T, deltas,
                     trow, mrow)
    k = _dft_constants(n)
    gr, gi = _hy_gfft(taps3, k["f1"], k["tr"], k["ti"], k["r2"])
    y3 = _hy_fftconv(vp3, gr, gi, k["f1v"], k["tr"], k["ti"], k["r2"], k["r2i"], k["i2"])
    rw, rb = _router_params(router_w, router_b)
    return _hy_out(y3, vp3, x03, skip[:, None], w_out.astype(bf16), b_out[None, :], x, mod, g2, rw, rb, tm)


def kernel(x, c, ctx, c_ctx, ada_w, ada_b, norm1_g, norm2_g, ev_w_in, ev_b_in, ev_q_gain, ev_k_gain, ev_f_bias,
           ev_h_gain, ev_w_out, ev_b_out, od_w_in, od_b_in, od_conv_w, od_conv_b, od_filt_w1, od_filt_b1,
           od_filt_freq, od_filt_w2, od_filt_b2, od_filt_w3, od_filt_b3, od_filt_w4, od_skip, od_w_out, od_b_out,
           router_w, router_b, moe_w_gu, moe_b_gu, moe_w_down, moe_b_down, final_g):
    bsz, n, d = x.shape
    assert bsz == 1 and c.shape[0] == 1
    depth = ada_w.shape[0]
    assert depth in (1, 2)
    cv = jnp.zeros((8, d), f32).at[0].set(c[0]).at[1].set(c_ctx)
    mods = _mods(cv, ada_w, ada_b)
    fg = final_g[None, :]

    h, t, meta, cnt = _even_layer(
        x[0], ctx[0], mods[0], norm1_g[0][None, :], norm2_g[0][None, :], ev_w_in[0], ev_b_in[0], ev_q_gain[0],
        ev_k_gain[0], ev_f_bias[0], ev_h_gain[0], ev_w_out[0], ev_b_out[0], router_w[0], router_b[0])
    h = _moe(t, meta, cnt[0, :N_EXPERTS], h, mods[0], fg, moe_w_gu, moe_b_gu, moe_w_down, moe_b_down, 0,
             final=(depth == 1))
    if depth == 2:
        h, t, meta, cnt = _odd_layer(
            h, mods[1], norm1_g[1][None, :], norm2_g[1][None, :], od_w_in[0], od_b_in[0], od_conv_w[0], od_conv_b[0],
            od_filt_w1[0], od_filt_b1[0], od_filt_freq[0], od_filt_w2[0], od_filt_b2[0], od_filt_w3[0],
            od_filt_b3[0], od_filt_w4[0], od_skip[0], od_w_out[0], od_b_out[0], router_w[1], router_b[1])
        h = _moe(t, meta, cnt[0, :N_EXPERTS], h, mods[1], fg, moe_w_gu, moe_b_gu, moe_w_down, moe_b_down, 1,
                 final=True)
    return h[None]
```

```python
import functools
import math

import numpy as np
import jax
import jax.numpy as jnp
from jax import lax
from jax.experimental import pallas as pl
from jax.experimental.pallas import tpu as pltpu

f32 = jnp.float32
bf16 = jnp.bfloat16
i32 = jnp.int32
HIGHEST = lax.Precision.HIGHEST

EPS = 1e-6
GRID_W = 64
HEAD_DIM = 128
ATTN_HEADS = 4
ATTN_KV_HEADS = 2
ROPE_THETA = 10000.0
ROPE_PAIRS = HEAD_DIM // 4
MLSTM_HEADS = 4
MLSTM_DK = 64
MLSTM_DV = 128
CHUNK = 128
N_EXPERTS = 32
TOP_K = 4
SWIGLU_LIMIT = 7.0
SWIGLU_ALPHA = 1.702
FILTER_EMB = 33
FILTER_BANDS = 16
DECAY_TARGET = 1e-2
FAST_DECAY = 0.3
SLOW_DECAY = 1.5

LANES = 128
VMEM_LIMIT = 56 * 1024 * 1024
MOE_BM = 512
MLSTM_STEP_CHUNKS = 1
ATTN_SLACK = 64.0


def _cp(sem, vmem=VMEM_LIMIT):
    return pltpu.CompilerParams(dimension_semantics=sem, vmem_limit_bytes=vmem)


def _full(shape):
    n = len(shape)
    return pl.BlockSpec(shape, lambda *a, _n=n: (0,) * _n)


def _rms(x, g):
    return x * lax.rsqrt(jnp.mean(x * x, axis=-1, keepdims=True) + EPS) * g


def _log_sigmoid(x):
    return jnp.minimum(x, 0.0) - jnp.log(1.0 + jnp.exp(-jnp.abs(x)))


def _dot(a, b):
    return jnp.dot(a, b, preferred_element_type=f32)


def _dot_nt(a, b, precision=None):
    return lax.dot_general(a, b, (((1,), (1,)), ((), ())), preferred_element_type=f32, precision=precision)


def _dot_tn(a, b):
    return lax.dot_general(a, b, (((0,), (0,)), ((), ())), preferred_element_type=f32)


def _mods_body(cv_ref, w_ref, b_ref, o_ref):
    cv = cv_ref[...]
    s = cv * jax.nn.sigmoid(cv)
    o_ref[0] = jnp.dot(s, w_ref[0], precision=HIGHEST, preferred_element_type=f32) + b_ref[0]


def _mods(cv, ada_w, ada_b):
    depth, d, nm = ada_w.shape
    tn = 1024
    return pl.pallas_call(
        _mods_body,
        grid=(depth, nm // tn),
        in_specs=[_full((8, d)),
                  pl.BlockSpec((1, d, tn), lambda i, j: (i, 0, j)),
                  pl.BlockSpec((1, 1, tn), lambda i, j: (i, 0, j))],
        out_specs=pl.BlockSpec((1, 8, tn), lambda i, j: (i, 0, j)),
        out_shape=jax.ShapeDtypeStruct((depth, 8, nm), f32),
        compiler_params=_cp(("arbitrary", "arbitrary")),
        name="mods",
    )(cv, ada_w, ada_b.reshape(depth, 1, nm))


def _ev_in_body(x_ref, mod_ref, g1_ref, wm_ref, bm_ref, wkT_ref, bkT_ref, wg_ref, bg_ref, wgT_ref, bgT_ref,
                qg_ref, kg_ref, cos_ref, sin_ref,
                q_ref, k_ref, v_ref, mq_ref, mkT_ref, mv_ref, og_ref, gc_ref, gT_ref, *, mrow, d):
    x = x_ref[...]
    shift = mod_ref[mrow:mrow + 1, 0:d]
    scale = mod_ref[mrow:mrow + 1, d:2 * d]
    a = (_rms(x, g1_ref[...]) * (1.0 + scale) + shift).astype(bf16)
    z = _dot(a, wm_ref[...]) + bm_ref[...]
    cos = cos_ref[...]
    sin = sin_ref[...]
    lane = lax.broadcasted_iota(i32, cos.shape, 1)
    first_half = (lane % 64) < 32

    def qk_head(zh, gain, mult):
        zn = zh * lax.rsqrt(jnp.mean(zh * zh, axis=-1, keepdims=True) + EPS) * gain
        partner = jnp.where(first_half, pltpu.roll(zn, 96, 1), pltpu.roll(zn, 32, 1))
        return ((zn * cos + partner * sin) * mult).astype(bf16)

    for h in range(ATTN_HEADS):
        q_ref[:, h * 128:(h + 1) * 128] = qk_head(z[:, h * 128:(h + 1) * 128], qg_ref[...],
                                                   HEAD_DIM ** -0.5 * math.log2(math.e))
    for h in range(ATTN_KV_HEADS):
        k_ref[:, h * 128:(h + 1) * 128] = qk_head(z[:, 512 + h * 128:512 + (h + 1) * 128], kg_ref[...], 1.0)
    v_ref[...] = z[:, 768:1024].astype(bf16)
    mq_ref[...] = (z[:, 1024:1280] * (MLSTM_DK ** -0.5)).astype(bf16)
    mv_ref[...] = z[:, 1280:1792].astype(bf16)
    og_ref[...] = z[:, 1792:2304]
    mkT_ref[...] = (_dot_nt(wkT_ref[...], a) + bkT_ref[...]).astype(bf16)
    gc_ref[...] = _dot(a, wg_ref[...]) + bg_ref[...]
    gT_ref[...] = _dot_nt(wgT_ref[...], a) + bgT_ref[...]


def _ev_in(x, mod, g1, wts, cos, sin, mrow, tm):
    r, d = x.shape
    wm, bm, wkT, bkT, wg, bg, wgT, bgT, qg, kg = wts
    row = lambda n: pl.BlockSpec((tm, n), lambda i: (i, 0))
    col = lambda n: pl.BlockSpec((n, tm), lambda i: (0, i))
    outs = [((r, 512), bf16, row(512)), ((r, 256), bf16, row(256)), ((r, 256), bf16, row(256)),
            ((r, 256), bf16, row(256)), ((256, r), bf16, col(256)), ((r, 512), bf16, row(512)),
            ((r, 512), f32, row(512)), ((r, LANES), f32, row(LANES)), ((16, r), f32, col(16))]
    return pl.pallas_call(
        functools.partial(_ev_in_body, mrow=mrow, d=d),
        grid=(r // tm,),
        in_specs=[row(d), _full(mod.shape), _full(g1.shape), _full(wm.shape), _full(bm.shape), _full(wkT.shape),
                  _full(bkT.shape), _full(wg.shape), _full(bg.shape), _full(wgT.shape), _full(bgT.shape),
                  _full(qg.shape), _full(kg.shape), row(LANES), row(LANES)],
        out_specs=[o[2] for o in outs],
        out_shape=[jax.ShapeDtypeStruct(o[0], o[1]) for o in outs],
        compiler_params=_cp(("arbitrary",)),
        name="ev_in",
    )(x, mod, g1, wm, bm, wkT, bkT, wg, bg, wgT, bgT, qg, kg, cos, sin)


def _attn_body(q_ref, k_ref, v_ref, o_ref, m_sc, acc_sc):
    j = pl.program_id(1)
    group = ATTN_HEADS // ATTN_KV_HEADS

    def scores(g):
        q = jnp.concatenate([q_ref[:, (g * group + i) * 128:(g * group + i + 1) * 128] for i in range(group)], axis=0)
        return _dot_nt(q, k_ref[:, g * 128:(g + 1) * 128])

    def renew_reference():
        for g in range(ATTN_KV_HEADS):
            s = scores(g)
            m_prev = m_sc[g]
            m_new = jnp.maximum(m_prev, jnp.max(s, axis=-1, keepdims=True))
            p = jnp.exp2(s - m_new).astype(bf16)
            acc_sc[g] = jnp.exp2(m_prev - m_new) * acc_sc[g] + _dot(p, v_ref[:, g * 256:(g + 1) * 256])
            m_sc[g] = m_new

    @pl.when(j == 0)
    def _():
        m_sc[...] = jnp.full(m_sc.shape, -jnp.inf, f32)
        acc_sc[...] = jnp.zeros(acc_sc.shape, f32)
        renew_reference()

    @pl.when(j > 0)
    def _():
        pvs, zmax = [], None
        for g in range(ATTN_KV_HEADS):
            z = scores(g) - m_sc[g]
            pvs.append(_dot(jnp.exp2(z).astype(bf16), v_ref[:, g * 256:(g + 1) * 256]))
            zm = jnp.max(z)
            zmax = zm if zmax is None else jnp.maximum(zmax, zm)
        in_range = zmax <= ATTN_SLACK

        @pl.when(in_range)
        def _():
            for g in range(ATTN_KV_HEADS):
                acc_sc[g] = acc_sc[g] + pvs[g]

        @pl.when(jnp.logical_not(in_range))
        def _():
            renew_reference()

    @pl.when(j == pl.num_programs(1) - 1)
    def _():
        tq = q_ref.shape[0]
        for g in range(ATTN_KV_HEADS):
            acc = acc_sc[g]
            o = (acc[:, 0:128] / acc[:, 128:256]).astype(bf16)
            for i in range(group):
                o_ref[:, (g * group + i) * 128:(g * group + i + 1) * 128] = o[i * tq:(i + 1) * tq]


def _attention(q, k, v, tq, tk):
    nq, nk = q.shape[0], k.shape[0]
    assert nq % tq == 0 and nk % tk == 0
    rows = tq * (ATTN_HEADS // ATTN_KV_HEADS)
    return pl.pallas_call(
        _attn_body,
        grid=(nq // tq, nk // tk),
        in_specs=[pl.BlockSpec((tq, 512), lambda i, j: (i, 0)),
                  pl.BlockSpec((tk, 256), lambda i, j: (j, 0)),
                  pl.BlockSpec((tk, 512), lambda i, j: (j, 0))],
        out_specs=pl.BlockSpec((tq, 512), lambda i, j: (i, 0)),
        out_shape=jax.ShapeDtypeStruct((nq, 512), bf16),
        scratch_shapes=[pltpu.VMEM((ATTN_KV_HEADS, rows, 1), f32), pltpu.VMEM((ATTN_KV_HEADS, rows, 256), f32)],
        compiler_params=_cp(("arbitrary", "arbitrary")),
        name="attention",
    )(q, k, v)


def _mlstm_body(qf_ref, kTf_ref, vf_ref, gcf_ref, gTf_ref, qb_ref, kTb_ref, vb_ref, gcb_ref, gTb_ref,
                fbr_ref, fbc_ref, c0_ref, m0_ref, hf_ref, hb_ref, c_out_ref, m_out_ref, c_sc, m_sc):
    j = pl.program_id(0)

    @pl.when(j == 0)
    def _():
        c_sc[...] = c0_ref[...]
        m_sc[...] = m0_ref[...]

    ti = lax.broadcasted_iota(i32, (CHUNK, CHUNK), 0)
    si = lax.broadcasted_iota(i32, (CHUNK, CHUNK), 1)
    ones_c = jnp.ones((CHUNK, 128), f32)
    for d in range(2):
        q_ref, kT_ref, v_ref, gc_ref, gT_ref, h_ref = ((qf_ref, kTf_ref, vf_ref, gcf_ref, gTf_ref, hf_ref) if d == 0 else
                                                      (qb_ref, kTb_ref, vb_ref, gcb_ref, gTb_ref, hb_ref))
        mask = (si <= ti) if d == 0 else (si >= ti)
        lmat = mask.astype(f32)
        umat = ((ti <= si) if d == 0 else (ti >= si)).astype(f32)
        last = CHUNK - 1 if d == 0 else 0
        nsub = gc_ref.shape[0] // CHUNK
        for u in (range(nsub) if d == 0 else range(nsub - 1, -1, -1)):
            rows = slice(u * CHUNK, (u + 1) * CHUNK)
            gc = gc_ref[rows, :]
            gT = gT_ref[:, rows]
            fcol = _log_sigmoid(gc + fbr_ref[...])
            frow = _log_sigmoid(gT + fbc_ref[...])
            bcol = jnp.dot(lmat, fcol, precision=HIGHEST, preferred_element_type=f32)
            brow = jnp.dot(frow, umat, precision=HIGHEST, preferred_element_type=f32)
            for h in range(MLSTM_HEADS):
                idx = d * MLSTM_HEADS + h
                ji, jf = idx, 8 + idx
                b_t = bcol[:, jf:jf + 1]
                i_t = gc[:, ji:ji + 1]
                logd = jnp.where(mask, b_t - brow[jf:jf + 1, :] + gT[ji:ji + 1, :], -jnp.inf)
                m_old = m_sc[idx][:, 0:1]
                m_row = jnp.maximum(b_t + m_old, jnp.max(logd, axis=-1, keepdims=True))
                w_inter = jnp.exp(b_t + m_old - m_row)
                dm = jnp.exp(logd - m_row)
                qh = q_ref[rows, h * MLSTM_DK:(h + 1) * MLSTM_DK]
                kTh = kT_ref[h * MLSTM_DK:(h + 1) * MLSTM_DK, rows]
                vaug = jnp.concatenate([v_ref[rows, h * 128:(h + 1) * 128].astype(f32), ones_c], axis=1)
                s = _dot(qh, kTh) * dm
                intra = _dot(s.astype(bf16), vaug.astype(bf16))
                c_old = c_sc[idx]
                inter = _dot(qh, c_old.astype(bf16))
                num = intra[:, 0:128] + w_inter * inter[:, 0:128]
                den = intra[:, 128:129] + w_inter * inter[:, 128:129]
                h_ref[rows, h * 128:(h + 1) * 128] = num / jnp.maximum(jnp.abs(den), jnp.exp(-m_row))
                b_last = b_t[last:last + 1, :]
                lw = b_last - b_t + i_t
                m_new = jnp.maximum(b_last + m_old, jnp.max(lw, axis=0, keepdims=True))
                w = jnp.exp(lw - m_new)
                decay = jnp.exp(b_last + m_old - m_new)
                c_sc[idx] = decay * c_old + _dot(kTh, (w * vaug).astype(bf16))
                m_sc[idx] = jnp.broadcast_to(m_new, (1, 128))

    @pl.when(j == pl.num_programs(0) - 1)
    def _():
        c_out_ref[...] = c_sc[...]
        m_out_ref[...] = m_sc[...]


def _mlstm(mq, mkT, mv, gc, gT, fbr, fbc, c0, m0):
    n = mq.shape[0]
    tb = MLSTM_STEP_CHUNKS * CHUNK
    assert n % tb == 0
    nc = n // tb
    fwd = lambda j: j
    bwd = lambda j: nc - 1 - j

    def specs(ix):
        return [pl.BlockSpec((tb, 256), lambda j: (ix(j), 0)),
                pl.BlockSpec((256, tb), lambda j: (0, ix(j))),
                pl.BlockSpec((tb, 512), lambda j: (ix(j), 0)),
                pl.BlockSpec((tb, LANES), lambda j: (ix(j), 0)),
                pl.BlockSpec((16, tb), lambda j: (0, ix(j)))]

    nh = 2 * MLSTM_HEADS
    return pl.pallas_call(
        _mlstm_body,
        grid=(nc,),
        in_specs=specs(fwd) + specs(bwd) + [_full(fbr.shape), _full(fbc.shape), _full(c0.shape), _full(m0.shape)],
        out_specs=[pl.BlockSpec((tb, 512), lambda j: (fwd(j), 0)),
                   pl.BlockSpec((tb, 512), lambda j: (bwd(j), 0)),
                   _full(c0.shape), _full(m0.shape)],
        out_shape=[jax.ShapeDtypeStruct((n, 512), f32), jax.ShapeDtypeStruct((n, 512), f32),
                   jax.ShapeDtypeStruct(c0.shape, f32), jax.ShapeDtypeStruct(m0.shape, f32)],
        scratch_shapes=[pltpu.VMEM((nh, MLSTM_DK, 256), f32), pltpu.VMEM((nh, 1, 128), f32)],
        compiler_params=_cp(("arbitrary",)),
        name="mlstm",
    )(mq, mkT, mv, gc, gT, mq, mkT, mv, gc, gT, fbr, fbc, c0, m0)


def _post_block(x, y, mod_ref, g2_ref, rw_ref, rb_ref, run_sc, h_ref, t_ref, meta_ref, cnt_ref, d):
    tm = x.shape[0]
    i = pl.program_id(0)

    @pl.when(i == 0)
    def _():
        run_sc[...] = jnp.zeros(run_sc.shape, f32)

    h1 = x + mod_ref[0:1, 2 * d:3 * d] * y
    h_ref[...] = h1
    t = _rms(h1, g2_ref[...]) * (1.0 + mod_ref[0:1, 4 * d:5 * d]) + mod_ref[0:1, 3 * d:4 * d]
    for j in range(d // LANES):
        t_ref[pl.ds(j, tm, stride=d // LANES), :] = t[:, j * LANES:(j + 1) * LANES]
    t_hi = t.astype(bf16)
    t_lo = (t - t_hi.astype(f32)).astype(bf16)
    logits = _dot(t_hi, rw_ref[0]) + (_dot(t_hi, rw_ref[1]) + _dot(t_lo, rw_ref[0])) + rb_ref[...]
    lane = lax.broadcasted_iota(i32, (tm, LANES), 1)
    lanef = lane.astype(f32)
    neg = jnp.float32(-jnp.inf)
    cur = jnp.where(lane < N_EXPERTS, logits, neg)
    tops, ids, hots = [], [], []
    for _ in range(TOP_K):
        mx = jnp.max(cur, axis=-1, keepdims=True)
        idx = jnp.min(jnp.where(cur == mx, lanef, 1e9), axis=-1, keepdims=True)
        hot = lanef == idx
        cur = jnp.where(hot, neg, cur)
        tops.append(mx)
        ids.append(idx)
        hots.append(hot)
    es = [jnp.exp(tk - tops[0]) for tk in tops]
    den = es[0] + es[1] + es[2] + es[3]
    cnt = jnp.zeros((tm, LANES), f32)
    for hot in hots:
        cnt = cnt + hot.astype(f32)
    ri = lax.broadcasted_iota(i32, (tm, tm), 0)
    ci = lax.broadcasted_iota(i32, (tm, tm), 1)
    strict = (ci < ri).astype(bf16)
    before = _dot(strict, cnt.astype(bf16)) + run_sc[...]
    meta = jnp.zeros((tm, LANES), f32)
    for k in range(TOP_K):
        rank = jnp.sum(jnp.where(hots[k], before, 0.0), axis=-1, keepdims=True)
        meta = jnp.where(lane == k, ids[k], meta)
        meta = jnp.where(lane == TOP_K + k, rank, meta)
        meta = jnp.where(lane == 2 * TOP_K + k, es[k] / den, meta)
    meta_ref[...] = meta
    run_sc[...] = run_sc[...] + jnp.sum(cnt, axis=0, keepdims=True)

    @pl.when(i == pl.num_programs(0) - 1)
    def _():
        cnt_ref[...] = jnp.broadcast_to(run_sc[...], cnt_ref.shape).astype(i32)


def _post_specs(n, d, tm):
    row = lambda w: pl.BlockSpec((tm, w), lambda i: (i, 0))
    out_specs = [row(d), pl.BlockSpec((tm * (d // LANES), LANES), lambda i: (i, 0)), row(LANES), _full((8, LANES))]
    out_shape = [jax.ShapeDtypeStruct((n, d), f32), jax.ShapeDtypeStruct((n * (d // LANES), LANES), f32),
                 jax.ShapeDtypeStruct((n, LANES), f32), jax.ShapeDtypeStruct((8, LANES), i32)]
    return out_specs, out_shape


def _ev_out_body(att_ref, hf_ref, hb_ref, og_ref, x_ref, mod_ref, hg_ref, w_ref, b_ref, g2_ref, rw_ref, rb_ref,
                 h_ref, t_ref, meta_ref, cnt_ref, run_sc, *, d):
    hs = hf_ref[...] + hb_ref[...]
    og = og_ref[...]
    parts = [att_ref[...]]
    for h in range(MLSTM_HEADS):
        sl = slice(h * 128, (h + 1) * 128)
        hh = hs[:, sl]
        hn = hh * lax.rsqrt(jnp.mean(hh * hh, axis=-1, keepdims=True) + EPS) * hg_ref[:, sl]
        parts.append((jax.nn.sigmoid(og[:, sl]) * hn).astype(bf16))
    y = _dot(jnp.concatenate(parts, axis=1), w_ref[...]) + b_ref[...]
    _post_block(x_ref[...], y, mod_ref, g2_ref, rw_ref, rb_ref, run_sc, h_ref, t_ref, meta_ref, cnt_ref, d)


def _ev_out(att, hf, hb, og, x, mod, hg, w, b, g2, rw, rb, tm):
    n, d = x.shape
    row = lambda wd: pl.BlockSpec((tm, wd), lambda i: (i, 0))
    out_specs, out_shape = _post_specs(n, d, tm)
    return pl.pallas_call(
        functools.partial(_ev_out_body, d=d),
        grid=(n // tm,),
        in_specs=[row(512), row(512), row(512), row(512), row(d), _full(mod.shape), _full(hg.shape), _full(w.shape),
                  _full(b.shape), _full(g2.shape), _full(rw.shape), _full(rb.shape)],
        out_specs=out_specs, out_shape=out_shape,
        scratch_shapes=[pltpu.VMEM((1, LANES), f32)],
        compiler_params=_cp(("arbitrary",)),
        name="ev_out",
    )(att, hf, hb, og, x, mod, hg, w, b, g2, rw, rb)


ROW_UNROLL = 8


TOKEN_ROWS = 8


def _dispatch_body(dest_ref, t_ref, buf_ref, sem):
    tm = t_ref.shape[0] // TOKEN_ROWS

    def issue(rb, c):
        base = pl.multiple_of(rb * (ROW_UNROLL * TOKEN_ROWS), ROW_UNROLL * TOKEN_ROWS)
        for u in range(ROW_UNROLL):
            src = t_ref.at[pl.ds(base + u * TOKEN_ROWS, TOKEN_ROWS)]
            for k in range(TOP_K):
                dest = pl.multiple_of(dest_ref[(rb * ROW_UNROLL + u) * TOP_K + k], TOKEN_ROWS)
                pltpu.make_async_copy(src, buf_ref.at[pl.ds(dest, TOKEN_ROWS)], sem).start(priority=k % 2)
        return c

    lax.fori_loop(0, tm // ROW_UNROLL, issue, 0)
    for k in range(TOP_K):
        pltpu.make_async_copy(t_ref, buf_ref.at[pl.ds(0, tm * TOKEN_ROWS)], sem).wait()


def _dispatch(dest, t, tm):
    rows, l = t.shape
    return pl.pallas_call(
        _dispatch_body,
        grid=(rows // (tm * TOKEN_ROWS),),
        in_specs=[pl.BlockSpec((tm * TOP_K,), lambda i: (i,), memory_space=pltpu.SMEM),
                  pl.BlockSpec((tm * TOKEN_ROWS, l), lambda i: (i, 0))],
        out_specs=pl.BlockSpec(memory_space=pl.ANY),
        out_shape=jax.ShapeDtypeStruct((rows * TOP_K, l), f32),
        scratch_shapes=[pltpu.SemaphoreType.DMA(())],
        compiler_params=_cp(("arbitrary",)),
        name="moe_dispatch",
    )(dest, t)


def _expert_body(blk_ref, exp_ref, lo_ref, act_ref, chg_ref, nxt_ref, x_ref, wgu_hbm, bgu_ref, wd_hbm, bd_ref, o_ref,
                 wgu_f32, wd_f32, wgu_sc, wd_sc, sem, *, de, layer):
    i = pl.program_id(0)

    def weight_copies(e):
        return (pltpu.make_async_copy(wgu_hbm.at[layer, e], wgu_f32, sem.at[0]),
                pltpu.make_async_copy(wd_hbm.at[layer, e], wd_f32, sem.at[1]))

    @pl.when(i == 0)
    def _():
        for cp in weight_copies(exp_ref[0]):
            cp.start()

    @pl.when(chg_ref[i] == 1)
    def _():
        for cp in weight_copies(exp_ref[i]):
            cp.wait()
        wgu_sc[...] = wgu_f32[...].astype(bf16)
        wd_sc[...] = wd_f32[...].astype(bf16)

        @pl.when(nxt_ref[i] >= 0)
        def _():
            for cp in weight_copies(nxt_ref[i]):
                cp.start()

    @pl.when(act_ref[i] == 1)
    def _():
        ns = TOKEN_ROWS
        x = jnp.concatenate([x_ref[pl.ds(j, MOE_BM, stride=ns), :] for j in range(ns)], axis=1).astype(bf16)
        gu = _dot(x, wgu_sc[...]) + bgu_ref[0, 0]
        g = jnp.minimum(gu[:, :de], SWIGLU_LIMIT)
        u = jnp.clip(gu[:, de:], -SWIGLU_LIMIT, SWIGLU_LIMIT)
        a = g * jax.nn.sigmoid(SWIGLU_ALPHA * g) * (u + 1.0)
        y = _dot(a.astype(bf16), wd_sc[...]) + bd_ref[0, 0]
        lo = lo_ref[i]

        @pl.when(lo == 0)
        def _():
            for j in range(ns):
                o_ref[pl.ds(j, MOE_BM, stride=ns), :] = y[:, j * LANES:(j + 1) * LANES]

        @pl.when(lo > 0)
        def _():
            keep = lax.broadcasted_iota(i32, (y.shape[0], LANES), 0) >= lo
            for j in range(ns):
                rows = pl.ds(j, MOE_BM, stride=ns)
                o_ref[rows, :] = jnp.where(keep, y[:, j * LANES:(j + 1) * LANES], o_ref[rows, :])


def _experts(sched, buf, wgu, bgu, wd, bd, layer):
    nl, ne, d, de2 = wgu.shape
    assert d == TOKEN_ROWS * LANES
    de = de2 // 2
    n_items = sched[0].shape[0]
    wmap = lambda i, b, e, l, a, c, n: (layer, e[i], 0, 0)
    rows = pl.BlockSpec((MOE_BM * TOKEN_ROWS, LANES), lambda i, b, e, l, a, c, n: (b[i], 0))
    return pl.pallas_call(
        functools.partial(_expert_body, de=de, layer=layer),
        grid_spec=pltpu.PrefetchScalarGridSpec(
            num_scalar_prefetch=6,
            grid=(n_items,),
            in_specs=[rows,
                      pl.BlockSpec(memory_space=pl.ANY),
                      pl.BlockSpec((1, 1, 1, de2), wmap),
                      pl.BlockSpec(memory_space=pl.ANY),
                      pl.BlockSpec((1, 1, 1, d), wmap)],
            out_specs=rows,
            scratch_shapes=[pltpu.VMEM((d, de2), f32), pltpu.VMEM((de, d), f32),
                            pltpu.VMEM((d, de2), bf16), pltpu.VMEM((de, d), bf16),
                            pltpu.SemaphoreType.DMA((2,))]),
        out_shape=jax.ShapeDtypeStruct(buf.shape, f32),
        compiler_params=_cp(("arbitrary",)),
        name="moe_experts",
    )(*sched, buf, wgu, bgu.reshape(nl, ne, 1, de2), wd, bd.reshape(nl, ne, 1, d))


def _combine_body(dest_ref, eo_ref, meta_ref, h_ref, mod_ref, fg_ref, o_ref, stage, sem, *, d, final):
    tm = h_ref.shape[0]
    ns = TOKEN_ROWS

    def issue(rb, c):
        base = pl.multiple_of(rb * (ROW_UNROLL * ns), ROW_UNROLL * ns)
        for u in range(ROW_UNROLL):
            for k in range(TOP_K):
                src = pl.multiple_of(dest_ref[(rb * ROW_UNROLL + u) * TOP_K + k], ns)
                pltpu.make_async_copy(eo_ref.at[pl.ds(src, ns)], stage.at[pl.ds(k * tm * ns + base + u * ns, ns)],
                                      sem).start(priority=k % 2)
        return c

    lax.fori_loop(0, tm // ROW_UNROLL, issue, 0)
    for k in range(TOP_K):
        pltpu.make_async_copy(eo_ref.at[pl.ds(0, tm * ns)], stage.at[pl.ds(k * tm * ns, tm * ns)], sem).wait()
    meta = meta_ref[...]
    f = None
    for k in range(TOP_K):
        rows = jnp.concatenate([stage[pl.ds(k * tm * ns + j, tm, stride=ns), :] for j in range(ns)], axis=1)
        term = meta[:, 2 * TOP_K + k:2 * TOP_K + k + 1] * rows
        f = term if f is None else f + term
    h = h_ref[...] + mod_ref[0:1, 5 * d:6 * d] * f
    if final:
        h = _rms(h, fg_ref[...])
    o_ref[...] = h


def _combine(dest, eo, meta, h, mod, fg, tm, final):
    n, d = h.shape
    return pl.pallas_call(
        functools.partial(_combine_body, d=d, final=final),
        grid=(n // tm,),
        in_specs=[pl.BlockSpec((tm * TOP_K,), lambda i: (i,), memory_space=pltpu.SMEM),
                  pl.BlockSpec(memory_space=pl.ANY),
                  pl.BlockSpec((tm, LANES), lambda i: (i, 0)),
                  pl.BlockSpec((tm, d), lambda i: (i, 0)),
                  _full(mod.shape), _full(fg.shape)],
        out_specs=pl.BlockSpec((tm, d), lambda i: (i, 0)),
        out_shape=jax.ShapeDtypeStruct((n, d), f32),
        scratch_shapes=[pltpu.VMEM((TOP_K * tm * TOKEN_ROWS, LANES), f32), pltpu.SemaphoreType.DMA(())],
        compiler_params=_cp(("arbitrary",)),
        name="moe_combine",
    )(dest, eo, meta, h, mod, fg)


def _moe(t, meta, counts, h, mod, fg, wgu, bgu, wd, bd, layer, final):
    n, d = h.shape
    m = n * TOP_K
    nblk = m // MOE_BM
    n_items = nblk + N_EXPERTS - 1
    counts = counts.astype(i32)
    ends = jnp.cumsum(counts)
    starts = ends - counts
    first_blk = starts // MOE_BM
    last_blk = jnp.maximum(ends - 1, starts) // MOE_BM
    per = jnp.where(counts > 0, last_blk - first_blk + 1, 0)
    cum = jnp.cumsum(per)
    off = cum - per
    total = cum[-1]
    it = jnp.minimum(jnp.arange(n_items, dtype=i32), jnp.maximum(total - 1, 0))
    e_it = jnp.minimum(jnp.sum((cum[None, :] <= it[:, None]).astype(i32), axis=1), N_EXPERTS - 1)
    eids = jnp.arange(N_EXPERTS, dtype=i32)[None, :]
    pick = lambda v: jnp.sum(jnp.where(e_it[:, None] == eids, v[None, :], 0), axis=1)
    blk_it = pick(first_blk) + it - pick(off)
    lo_it = jnp.maximum(pick(starts) - blk_it * MOE_BM, 0)
    active = (jnp.arange(n_items, dtype=i32) < total).astype(i32)
    changed = jnp.concatenate([jnp.ones((1,), i32), (e_it[1:] != e_it[:-1]).astype(i32)])
    later = (eids > eids.T) & (counts[None, :] > 0)
    nxt_e = jnp.min(jnp.where(later, eids, N_EXPERTS), axis=1)
    nxt_it = pick(jnp.where(nxt_e < N_EXPERTS, nxt_e, -1))
    sched = (blk_it.astype(i32), e_it.astype(i32), lo_it.astype(i32), active, changed, nxt_it.astype(i32))
    ids = meta[:, 0:TOP_K].astype(i32)
    ranks = meta[:, TOP_K:2 * TOP_K].astype(i32)
    dest = (jnp.sum(jnp.where(ids[:, :, None] == eids[None], starts[None, None, :], 0), axis=-1) + ranks).reshape(-1)
    dest = dest * TOKEN_ROWS
    buf = _dispatch(dest, t, 256)
    eo = _experts(sched, buf, wgu, bgu, wd, bd, layer)
    return _combine(dest, eo, meta, h, mod, fg, 256, final)


def _rope_tables(n):
    rows = n // GRID_W
    row = np.repeat(np.arange(rows), GRID_W)
    col = np.tile(np.arange(GRID_W), rows)
    pos = np.stack([row, col], axis=-1).astype(np.float64)
    inv_freq = ROPE_THETA ** (-np.arange(ROPE_PAIRS, dtype=np.float64) / ROPE_PAIRS)
    ang = pos[:, :, None] * inv_freq
    c, s = np.cos(ang), np.sin(ang)
    cos = np.concatenate([c[:, 0], c[:, 0], c[:, 1], c[:, 1]], axis=-1).astype(np.float32)
    sin = np.concatenate([-s[:, 0], s[:, 0], -s[:, 1], s[:, 1]], axis=-1).astype(np.float32)
    return jnp.asarray(cos), jnp.asarray(sin)


def _router_params(router_w, router_b):
    d = router_w.shape[0]
    rw = jnp.zeros((d, LANES), f32).at[:, :N_EXPERTS].set(router_w)
    rb = jnp.zeros((1, LANES), f32).at[0, :N_EXPERTS].set(router_b)
    rw_hi = rw.astype(bf16)
    rw_lo = (rw - rw_hi.astype(f32)).astype(bf16)
    return jnp.stack([rw_hi, rw_lo]), rb


def _even_layer(x, ctx, mod, g1, g2, w_in, b_in, q_gain, k_gain, f_bias, h_gain, w_out, b_out, router_w, router_b):
    n, d = x.shape
    nctx = ctx.shape[0]
    cols = lambda a, b: w_in[:, a:b]
    wm = jnp.concatenate([cols(0, 1280), cols(1536, 2048), cols(2064, 2576)], axis=1).astype(bf16)
    bm = jnp.concatenate([b_in[0:1280], b_in[1536:2048], b_in[2064:2576]])[None, :]
    wkT = cols(1280, 1536).T.astype(bf16)
    bkT = b_in[1280:1536][:, None]
    wgate = cols(2048, 2064)
    wg = jnp.zeros((d, LANES), f32).at[:, :16].set(wgate).astype(bf16)
    bg = jnp.zeros((1, LANES), f32).at[0, :16].set(b_in[2048:2064])
    wgT = wgate.T.astype(bf16)
    bgT = b_in[2048:2064][:, None]
    wts = (wm, bm, wkT, bkT, wg, bg, wgT, bgT, q_gain[None, :], k_gain[None, :])
    cos, sin = _rope_tables(n)
    lat = _ev_in(x, mod, g1, wts, cos, sin, 0, min(512, n))
    cx = _ev_in(ctx, mod, g1, wts, jnp.ones((nctx, LANES), f32), jnp.zeros((nctx, LANES), f32), 1, nctx)
    q, k, v, mq, mkT, mv, og, gc, gT = lat
    _, kc, vc, mqc, mkTc, mvc, _, gcc, gTc = cx

    v_all = jnp.concatenate([v, vc], axis=0)
    ones = jnp.ones((n + nctx, 128), bf16)
    v_aug = jnp.concatenate([v_all[:, 0:128], ones, v_all[:, 128:256], ones], axis=1)
    att = _attention(q, jnp.concatenate([k, kc], axis=0), v_aug, 1024 if n % 1024 == 0 else min(512, n), 1280)

    fb = f_bias.reshape(-1)
    fbr = jnp.zeros((1, LANES), f32).at[0, 8:16].set(fb)
    fbc = jnp.zeros((16, LANES), f32).at[8:16, :].set(jnp.broadcast_to(fb[:, None], (8, LANES)))
    nh = 2 * MLSTM_HEADS
    c0 = jnp.zeros((nh, MLSTM_DK, 256), f32)
    m0 = jnp.zeros((nh, 1, 128), f32)
    _, _, c1, m1 = _mlstm(mqc, mkTc, mvc, gcc, gTc, fbr, fbc, c0, m0)
    hf, hb, _, _ = _mlstm(mq, mkT, mv, gc, gT, fbr, fbc, c1, m1)

    rw, rb = _router_params(router_w, router_b)
    return _ev_out(att, hf, hb, og, x, mod, h_gain[None, :], w_out.astype(bf16), b_out[None, :], g2, rw, rb,
                   min(512, n))


FFT_N2 = 128
CH_TILE = 8


def _ct_shape(ch, nblocks):
    return jax.ShapeDtypeStruct((ch // CH_TILE, nblocks * CH_TILE, FFT_N2), f32)


def _ct_block(ch, per):
    return pl.BlockSpec((ch // CH_TILE, per * CH_TILE, FFT_N2), lambda i: (0, i, 0))


def _ct_slab(nblocks):
    return pl.BlockSpec((1, nblocks * CH_TILE, FFT_N2), lambda i: (i, 0, 0))


def _ct_channel(ref, c, nblocks):
    return ref.at[0], pl.ds(c, nblocks, stride=CH_TILE)


def _hy_in_body(x_ref, mod_ref, g1_ref, wT_ref, b_ref, zT_ref, *, d):
    a = (_rms(x_ref[...], g1_ref[...]) * (1.0 + mod_ref[0:1, d:2 * d]) + mod_ref[0:1, 0:d]).astype(bf16)
    zT_ref[...] = _dot_nt(wT_ref[...], a) + b_ref[...]


def _hy_in(x, mod, g1, wT, b, tm):
    n, d = x.shape
    c3 = wT.shape[0]
    return pl.pallas_call(
        functools.partial(_hy_in_body, d=d),
        grid=(n // tm,),
        in_specs=[pl.BlockSpec((tm, d), lambda i: (i, 0)), _full(mod.shape), _full(g1.shape), _full(wT.shape),
                  _full(b.shape)],
        out_specs=pl.BlockSpec((c3, tm), lambda i: (0, i)),
        out_shape=jax.ShapeDtypeStruct((c3, n), f32),
        compiler_params=_cp(("arbitrary",)),
        name="hy_in",
    )(x, mod, g1, wT, b)


def _hy_conv_body(z_ref, zp_ref, zn_ref, cw_ref, vp_ref, x0_ref, *, ch):
    i = pl.program_id(0)
    tt = z_ref.shape[1]
    has_prev = (i > 0).astype(f32)
    has_next = (i < pl.num_programs(0) - 1).astype(f32)
    lane = lax.broadcasted_iota(i32, (ch, tt), 1)

    def conv(g):
        rows = slice(g * ch, (g + 1) * ch)
        z = z_ref[rows, :]
        pcol = zp_ref[rows, FFT_N2 - 1:FFT_N2] * has_prev
        ncol = zn_ref[rows, 0:1] * has_next
        zm1 = jnp.where(lane == 0, pcol, pltpu.roll(z, 1, 1))
        zp1 = jnp.where(lane == tt - 1, ncol, pltpu.roll(z, tt - 1, 1))
        cw = cw_ref[rows, :]
        return cw[:, 0:1] * zm1 + cw[:, 1:2] * z + cw[:, 2:3] * zp1 + cw[:, 3:4]

    x0 = conv(0)
    vp = conv(2) * conv(1)
    for j in range(tt // FFT_N2):
        rows = slice(j * CH_TILE, (j + 1) * CH_TILE)
        x0_ref[:, rows, :] = x0[:, j * FFT_N2:(j + 1) * FFT_N2].reshape(ch // CH_TILE, CH_TILE, FFT_N2)
        vp_ref[:, rows, :] = vp[:, j * FFT_N2:(j + 1) * FFT_N2].reshape(ch // CH_TILE, CH_TILE, FFT_N2)


def _hy_conv(zT, cw, tt):
    c3, n = zT.shape
    ch = c3 // 3
    nb = n // FFT_N2
    per = tt // FFT_N2
    return pl.pallas_call(
        functools.partial(_hy_conv_body, ch=ch),
        grid=(n // tt,),
        in_specs=[pl.BlockSpec((c3, tt), lambda i: (0, i)),
                  pl.BlockSpec((c3, FFT_N2), lambda i: (0, jnp.maximum(i * per - 1, 0))),
                  pl.BlockSpec((c3, FFT_N2), lambda i: (0, jnp.minimum(i * per + per, nb - 1))),
                  _full(cw.shape)],
        out_specs=[_ct_block(ch, per), _ct_block(ch, per)],
        out_shape=[_ct_shape(ch, nb), _ct_shape(ch, nb)],
        compiler_params=_cp(("arbitrary",)),
        name="hy_conv",
    )(zT, zT, zT, cw)


def _hy_filt_body(zz_ref, w1_ref, b1_ref, fr_ref, w2_ref, b2_ref, w3_ref, b3_ref, w4T_ref, dl_ref, tr_ref, mr_ref,
                  o_ref):
    hdot = lambda a, b: jnp.dot(a, b, precision=HIGHEST, preferred_element_type=f32)
    fr = fr_ref[...]
    h = jnp.sin(fr * (hdot(w1_ref[...], zz_ref[...]) + b1_ref[...]))
    h = jnp.sin(fr * (hdot(w2_ref[...], h) + b2_ref[...]))
    h = jnp.sin(fr * (hdot(w3_ref[...], h) + b3_ref[...]))
    taps = _dot(w4T_ref[0], h.astype(bf16))
    ch = taps.shape[0]
    dl = dl_ref[...]
    for j in range(tr_ref.shape[0]):
        blk = taps[:, j * FFT_N2:(j + 1) * FFT_N2] * jnp.exp(-dl * tr_ref[j]) * mr_ref[j]
        o_ref[:, j * CH_TILE:(j + 1) * CH_TILE, :] = blk.reshape(ch // CH_TILE, CH_TILE, FFT_N2)


def _hy_filt(zz, w1, b1, fr, w2, b2, w3, b3, w4T, deltas, trow, mrow):
    n2l = zz.shape[1]
    nb2 = n2l // FFT_N2
    ch = w4T.shape[1]
    per = 4 if nb2 % 8 == 0 else 1
    steps = nb2 // per
    return pl.pallas_call(
        _hy_filt_body,
        grid=(steps,),
        in_specs=[pl.BlockSpec((LANES, per * FFT_N2), lambda i: (0, i)), _full(w1.shape), _full(b1.shape),
                  _full(fr.shape), _full(w2.shape), _full(b2.shape), _full(w3.shape), _full(b3.shape),
                  pl.BlockSpec((1,) + w4T.shape[1:], lambda i: ((i >= steps // 2).astype(i32), 0, 0)),
                  _full(deltas.shape),
                  pl.BlockSpec((per, 1, FFT_N2), lambda i: (i, 0, 0)),
                  pl.BlockSpec((per, 1, FFT_N2), lambda i: (i, 0, 0))],
        out_specs=_ct_block(ch, per),
        out_shape=_ct_shape(ch, nb2),
        compiler_params=_cp(("arbitrary",)),
        name="hy_filt",
    )(zz, w1, b1, fr, w2, b2, w3, b3, w4T, deltas, trow, mrow)


def _dft_stage12(x_ref, f1_ref, tr_ref, ti_ref, r2_ref):
    ct = CH_TILE
    n1 = f1_ref.shape[0] // 2
    n_in = f1_ref.shape[1]
    xs = []
    for c in range(ct):
        slab, rows = _ct_channel(x_ref, c, n_in)
        xs.append(slab[rows, :])
    x = jnp.concatenate(xs, axis=1).astype(bf16)
    a = _dot(f1_ref[...], x)
    tr, ti = tr_ref[...], ti_ref[...]
    lhs = []
    for c in range(ct):
        ar = a[0:n1, c * FFT_N2:(c + 1) * FFT_N2]
        ai = a[n1:2 * n1, c * FFT_N2:(c + 1) * FFT_N2]
        lhs.append(jnp.concatenate([ar * tr - ai * ti, ar * ti + ai * tr], axis=1).astype(bf16))
    xf = _dot(jnp.concatenate(lhs, axis=0), r2_ref[...])
    return [(xf[c * n1:(c + 1) * n1, 0:FFT_N2], xf[c * n1:(c + 1) * n1, FFT_N2:2 * FFT_N2]) for c in range(ct)]


def _hy_gfft_body(t_ref, f1_ref, tr_ref, ti_ref, r2_ref, gr_ref, gi_ref):
    n1 = f1_ref.shape[0] // 2
    for c, (xr, xi) in enumerate(_dft_stage12(t_ref, f1_ref, tr_ref, ti_ref, r2_ref)):
        slab, rows = _ct_channel(gr_ref, c, n1)
        slab[rows, :] = xr
        slab, rows = _ct_channel(gi_ref, c, n1)
        slab[rows, :] = xi


def _hy_gfft(taps3, f1, tr, ti, r2):
    n1 = f1.shape[1]
    blk = _ct_slab(n1)
    return pl.pallas_call(
        _hy_gfft_body,
        grid=(taps3.shape[0],),
        in_specs=[blk, _full(f1.shape), _full(tr.shape), _full(ti.shape), _full(r2.shape)],
        out_specs=[blk, blk],
        out_shape=[jax.ShapeDtypeStruct(taps3.shape, f32), jax.ShapeDtypeStruct(taps3.shape, f32)],
        compiler_params=_cp(("arbitrary",)),
        name="hy_gfft",
    )(taps3, f1, tr, ti, r2)


def _hy_fftconv_body(v_ref, gr_ref, gi_ref, f1_ref, tr_ref, ti_ref, r2_ref, r2i_ref, i2_ref, y_ref):
    ct = CH_TILE
    n1 = f1_ref.shape[0] // 2
    n_out = i2_ref.shape[0]
    tr, ti = tr_ref[...], ti_ref[...]
    lhs = []
    for c, (xr, xi) in enumerate(_dft_stage12(v_ref, f1_ref, tr_ref, ti_ref, r2_ref)):
        slab, rows = _ct_channel(gr_ref, c, n1)
        gr = slab[rows, :]
        slab, rows = _ct_channel(gi_ref, c, n1)
        gi = slab[rows, :]
        lhs.append(jnp.concatenate([xr * gr - xi * gi, xr * gi + xi * gr], axis=1).astype(bf16))
    b = _dot(jnp.concatenate(lhs, axis=0), r2i_ref[...])
    brs, bis = [], []
    for c in range(ct):
        br = b[c * n1:(c + 1) * n1, 0:FFT_N2]
        bi = b[c * n1:(c + 1) * n1, FFT_N2:2 * FFT_N2]
        brs.append(br * tr + bi * ti)
        bis.append(bi * tr - br * ti)
    rhs = jnp.concatenate([jnp.concatenate(brs, axis=1), jnp.concatenate(bis, axis=1)], axis=0).astype(bf16)
    y = _dot(i2_ref[...], rhs)
    for c in range(ct):
        slab, rows = _ct_channel(y_ref, c, n_out)
        slab[rows, :] = y[:, c * FFT_N2:(c + 1) * FFT_N2]


def _hy_fftconv(vp3, gr, gi, f1, tr, ti, r2, r2i, i2):
    vblk = _ct_slab(f1.shape[1])
    gblk = _ct_slab(f1.shape[0] // 2)
    return pl.pallas_call(
        _hy_fftconv_body,
        grid=(vp3.shape[0],),
        in_specs=[vblk, gblk, gblk, _full(f1.shape), _full(tr.shape), _full(ti.shape), _full(r2.shape),
                  _full(r2i.shape), _full(i2.shape)],
        out_specs=vblk,
        out_shape=jax.ShapeDtypeStruct(vp3.shape, f32),
        compiler_params=_cp(("arbitrary",)),
        name="hy_fftconv",
    )(vp3, gr, gi, f1, tr, ti, r2, r2i, i2)


def _hy_out_body(y_ref, vp_ref, x0_ref, sk_ref, w_ref, b_ref, x_ref, mod_ref, g2_ref, rw_ref, rb_ref,
                 h_ref, t_ref, meta_ref, cnt_ref, run_sc, *, d):
    sk = sk_ref[...]
    ch = sk.shape[0]

    def block(ref, j):
        return ref[:, j * CH_TILE:(j + 1) * CH_TILE, :].reshape(ch, FFT_N2)

    g = jnp.concatenate([((block(y_ref, j) + block(vp_ref, j) * sk) * block(x0_ref, j)).astype(bf16)
                         for j in range(y_ref.shape[1] // CH_TILE)], axis=1)
    y = _dot_tn(g, w_ref[...]) + b_ref[...]
    _post_block(x_ref[...], y, mod_ref, g2_ref, rw_ref, rb_ref, run_sc, h_ref, t_ref, meta_ref, cnt_ref, d)


def _hy_out(y3, vp3, x03, skip, w, b, x, mod, g2, rw, rb, tm):
    n, d = x.shape
    blk = _ct_block(skip.shape[0], tm // FFT_N2)
    out_specs, out_shape = _post_specs(n, d, tm)
    return pl.pallas_call(
        functools.partial(_hy_out_body, d=d),
        grid=(n // tm,),
        in_specs=[blk, blk, blk, _full(skip.shape), _full(w.shape), _full(b.shape),
                  pl.BlockSpec((tm, d), lambda i: (i, 0)), _full(mod.shape), _full(g2.shape), _full(rw.shape),
                  _full(rb.shape)],
        out_specs=out_specs, out_shape=out_shape,
        scratch_shapes=[pltpu.VMEM((1, LANES), f32)],
        compiler_params=_cp(("arbitrary",)),
        name="hy_out",
    )(y3, vp3, x03, skip, w, b, x, mod, g2, rw, rb)


def _dft_constants(n):
    nn = 2 * n
    n1 = nn // FFT_N2
    k1 = np.arange(n1)[:, None]
    phi = 2.0 * np.pi * (k1 * np.arange(n1)[None, :] % n1) / n1
    f1 = np.concatenate([np.cos(phi), -np.sin(phi)], axis=0)
    th = 2.0 * np.pi * (k1 * np.arange(FFT_N2)[None, :]) / nn
    tr, ti = np.cos(th), -np.sin(th)
    ps = 2.0 * np.pi * (np.arange(FFT_N2)[:, None] * np.arange(FFT_N2)[None, :] % FFT_N2) / FFT_N2
    c2, s2 = np.cos(ps), np.sin(ps)
    r2 = np.block([[c2, -s2], [s2, c2]])
    r2i = np.block([[c2, s2], [-s2, c2]])
    phi_i = phi[: n1 // 2, :]
    i2 = np.concatenate([np.cos(phi_i), -np.sin(phi_i)], axis=1) / nn
    as32 = lambda a: jnp.asarray(a.astype(np.float32))
    return dict(f1=as32(f1).astype(bf16), f1v=as32(f1[:, : n1 // 2]).astype(bf16), tr=as32(tr), ti=as32(ti),
                r2=as32(r2).astype(bf16), r2i=as32(r2i).astype(bf16), i2=as32(i2).astype(bf16))


def _filter_tables(n):
    t = np.linspace(0.0, 1.0, n)[:, None]
    w = (2.0 * np.pi / n) * np.arange(n)[:, None]
    f = np.linspace(1e-4, FILTER_BANDS - 1, FILTER_BANDS)
    z = np.concatenate([t, np.cos(f * w), -np.sin(f * w)], axis=-1)
    pos2 = np.arange(2 * n)
    pos = np.where(pos2 < n, pos2, np.minimum(2 * n - pos2, n - 1))
    zz = np.zeros((LANES, 2 * n), np.float32)
    zz[:FILTER_EMB, :] = z[pos].T
    trow = t[pos, 0].astype(np.float32).reshape(2 * n // FFT_N2, 1, FFT_N2)
    mrow = (pos2 != n).astype(np.float32).reshape(2 * n // FFT_N2, 1, FFT_N2)
    return jnp.asarray(zz), jnp.asarray(trow), jnp.asarray(mrow)


def _odd_layer(x, mod, g1, g2, w_in, b_in, conv_w, conv_b, fw1, fb1, ffreq, fw2, fb2, fw3, fb3, fw4, skip, w_out,
               b_out, router_w, router_b):
    n, d = x.shape
    ch = w_out.shape[0]
    tm = min(512, n)
    zT = _hy_in(x, mod, g1, w_in.T.astype(bf16), b_in[:, None], tm)
    cw = jnp.zeros((3 * ch, LANES), f32).at[:, 0:3].set(conv_w.T).at[:, 3].set(conv_b)
    vp3, x03 = _hy_conv(zT, cw, tm)

    zz, trow, mrow = _filter_tables(n)
    w1T = jnp.zeros((fw1.shape[1], LANES), f32).at[:, :FILTER_EMB].set(fw1.T)
    w4T = jnp.stack([fw4[:, :ch].T, fw4[:, ch:].T]).astype(bf16)
    deltas = jnp.abs(jnp.linspace(math.log(DECAY_TARGET) / SLOW_DECAY, math.log(DECAY_TARGET) / FAST_DECAY, ch,
                                  dtype=f32))[:, None]
    taps3 = _hy_filt(zz, w1T, fb1[:, None], ffreq[:, None], fw2.T, fb2[:, None], fw3.T, fb3[:, None], w4T, deltas,
                     trow, mrow)
    k = _dft_constants(n)
    gr, gi = _hy_gfft(taps3, k["f1"], k["tr"], k["ti"], k["r2"])
    y3 = _hy_fftconv(vp3, gr, gi, k["f1v"], k["tr"], k["ti"], k["r2"], k["r2i"], k["i2"])
    rw, rb = _router_params(router_w, router_b)
    return _hy_out(y3, vp3, x03, skip[:, None], w_out.astype(bf16), b_out[None, :], x, mod, g2, rw, rb, tm)


def kernel(x, c, ctx, c_ctx, ada_w, ada_b, norm1_g, norm2_g, ev_w_in, ev_b_in, ev_q_gain, ev_k_gain, ev_f_bias,
           ev_h_gain, ev_w_out, ev_b_out, od_w_in, od_b_in, od_conv_w, od_conv_b, od_filt_w1, od_filt_b1,
           od_filt_freq, od_filt_w2, od_filt_b2, od_filt_w3, od_filt_b3, od_filt_w4, od_skip, od_w_out, od_b_out,
           router_w, router_b, moe_w_gu, moe_b_gu, moe_w_down, moe_b_down, final_g):
    bsz, n, d = x.shape
    assert bsz == 1 and c.shape[0] == 1
    depth = ada_w.shape[0]
    assert depth in (1, 2)
    cv = jnp.zeros((8, d), f32).at[0].set(c[0]).at[1].set(c_ctx)
    mods = _mods(cv, ada_w, ada_b)
    fg = final_g[None, :]

    h, t, meta, cnt = _even_layer(
        x[0], ctx[0], mods[0], norm1_g[0][None, :], norm2_g[0][None, :], ev_w_in[0], ev_b_in[0], ev_q_gain[0],
        ev_k_gain[0], ev_f_bias[0], ev_h_gain[0], ev_w_out[0], ev_b_out[0], router_w[0], router_b[0])
    h = _moe(t, meta, cnt[0, :N_EXPERTS], h, mods[0], fg, moe_w_gu, moe_b_gu, moe_w_down, moe_b_down, 0,
             final=(depth == 1))
    if depth == 2:
        h, t, meta, cnt = _odd_layer(
            h, mods[1], norm1_g[1][None, :], norm2_g[1][None, :], od_w_in[0], od_b_in[0], od_conv_w[0], od_conv_b[0],
            od_filt_w1[0], od_filt_b1[0], od_filt_freq[0], od_filt_w2[0], od_filt_b2[0], od_filt_w3[0],
            od_filt_b3[0], od_filt_w4[0], od_skip[0], od_w_out[0], od_b_out[0], router_w[1], router_b[1])
        h = _moe(t, meta, cnt[0, :N_EXPERTS], h, mods[1], fg, moe_w_gu, moe_b_gu, moe_w_down, moe_b_down, 1,
                 final=True)
    return h[None]
```

```python
import functools
import math

import numpy as np
import jax
import jax.numpy as jnp
from jax import lax
from jax.experimental import pallas as pl
from jax.experimental.pallas import tpu as pltpu

f32 = jnp.float32
bf16 = jnp.bfloat16
i32 = jnp.int32
HIGHEST = lax.Precision.HIGHEST

EPS = 1e-6
GRID_W = 64
HEAD_DIM = 128
ATTN_HEADS = 4
ATTN_KV_HEADS = 2
ROPE_THETA = 10000.0
ROPE_PAIRS = HEAD_DIM // 4
MLSTM_HEADS = 4
MLSTM_DK = 64
MLSTM_DV = 128
CHUNK = 128
N_EXPERTS = 32
TOP_K = 4
SWIGLU_LIMIT = 7.0
SWIGLU_ALPHA = 1.702
FILTER_EMB = 33
FILTER_BANDS = 16
DECAY_TARGET = 1e-2
FAST_DECAY = 0.3
SLOW_DECAY = 1.5

LANES = 128
VMEM_LIMIT = 56 * 1024 * 1024
MOE_BM = 512
MLSTM_STEP_CHUNKS = 1
ATTN_PROBE = 256
ATTN_SLACK = 64.0


def _cp(sem, vmem=VMEM_LIMIT):
    return pltpu.CompilerParams(dimension_semantics=sem, vmem_limit_bytes=vmem)


def _full(shape):
    n = len(shape)
    return pl.BlockSpec(shape, lambda *a, _n=n: (0,) * _n)


def _rms(x, g):
    return x * lax.rsqrt(jnp.mean(x * x, axis=-1, keepdims=True) + EPS) * g


def _log_sigmoid(x):
    return jnp.minimum(x, 0.0) - jnp.log(1.0 + jnp.exp(-jnp.abs(x)))


def _dot(a, b):
    return jnp.dot(a, b, preferred_element_type=f32)


def _dot_nt(a, b, precision=None):
    return lax.dot_general(a, b, (((1,), (1,)), ((), ())), preferred_element_type=f32, precision=precision)


def _dot_tn(a, b):
    return lax.dot_general(a, b, (((0,), (0,)), ((), ())), preferred_element_type=f32)


def _mods_body(cv_ref, w_ref, b_ref, o_ref):
    cv = cv_ref[...]
    s = cv * jax.nn.sigmoid(cv)
    o_ref[0] = jnp.dot(s, w_ref[0], precision=HIGHEST, preferred_element_type=f32) + b_ref[0]


def _mods(cv, ada_w, ada_b):
    depth, d, nm = ada_w.shape
    tn = 1024
    return pl.pallas_call(
        _mods_body,
        grid=(depth, nm // tn),
        in_specs=[_full((8, d)),
                  pl.BlockSpec((1, d, tn), lambda i, j: (i, 0, j)),
                  pl.BlockSpec((1, 1, tn), lambda i, j: (i, 0, j))],
        out_specs=pl.BlockSpec((1, 8, tn), lambda i, j: (i, 0, j)),
        out_shape=jax.ShapeDtypeStruct((depth, 8, nm), f32),
        compiler_params=_cp(("arbitrary", "arbitrary")),
        name="mods",
    )(cv, ada_w, ada_b.reshape(depth, 1, nm))


def _ev_in_body(x_ref, mod_ref, g1_ref, wm_ref, bm_ref, wkT_ref, bkT_ref, wg_ref, bg_ref, wgT_ref, bgT_ref,
                qg_ref, kg_ref, cos_ref, sin_ref,
                q_ref, k_ref, v_ref, mq_ref, mkT_ref, mv_ref, og_ref, gc_ref, gT_ref, *, mrow, d):
    x = x_ref[...]
    shift = mod_ref[mrow:mrow + 1, 0:d]
    scale = mod_ref[mrow:mrow + 1, d:2 * d]
    a = (_rms(x, g1_ref[...]) * (1.0 + scale) + shift).astype(bf16)
    z = _dot(a, wm_ref[...]) + bm_ref[...]
    cos = cos_ref[...]
    sin = sin_ref[...]
    lane = lax.broadcasted_iota(i32, cos.shape, 1)
    first_half = (lane % 64) < 32

    def qk_head(zh, gain, mult):
        zn = zh * lax.rsqrt(jnp.mean(zh * zh, axis=-1, keepdims=True) + EPS) * gain
        partner = jnp.where(first_half, pltpu.roll(zn, 96, 1), pltpu.roll(zn, 32, 1))
        return ((zn * cos + partner * sin) * mult).astype(bf16)

    for h in range(ATTN_HEADS):
        q_ref[:, h * 128:(h + 1) * 128] = qk_head(z[:, h * 128:(h + 1) * 128], qg_ref[...],
                                                   HEAD_DIM ** -0.5 * math.log2(math.e))
    for h in range(ATTN_KV_HEADS):
        k_ref[:, h * 128:(h + 1) * 128] = qk_head(z[:, 512 + h * 128:512 + (h + 1) * 128], kg_ref[...], 1.0)
    v_ref[...] = z[:, 768:1024].astype(bf16)
    mq_ref[...] = (z[:, 1024:1280] * (MLSTM_DK ** -0.5)).astype(bf16)
    mv_ref[...] = z[:, 1280:1792].astype(bf16)
    og_ref[...] = z[:, 1792:2304]
    mkT_ref[...] = (_dot_nt(wkT_ref[...], a) + bkT_ref[...]).astype(bf16)
    gc_ref[...] = _dot(a, wg_ref[...]) + bg_ref[...]
    gT_ref[...] = _dot_nt(wgT_ref[...], a) + bgT_ref[...]


def _ev_in(x, mod, g1, wts, cos, sin, mrow, tm):
    r, d = x.shape
    wm, bm, wkT, bkT, wg, bg, wgT, bgT, qg, kg = wts
    row = lambda n: pl.BlockSpec((tm, n), lambda i: (i, 0))
    col = lambda n: pl.BlockSpec((n, tm), lambda i: (0, i))
    outs = [((r, 512), bf16, row(512)), ((r, 256), bf16, row(256)), ((r, 256), bf16, row(256)),
            ((r, 256), bf16, row(256)), ((256, r), bf16, col(256)), ((r, 512), bf16, row(512)),
            ((r, 512), f32, row(512)), ((r, LANES), f32, row(LANES)), ((16, r), f32, col(16))]
    return pl.pallas_call(
        functools.partial(_ev_in_body, mrow=mrow, d=d),
        grid=(r // tm,),
        in_specs=[row(d), _full(mod.shape), _full(g1.shape), _full(wm.shape), _full(bm.shape), _full(wkT.shape),
                  _full(bkT.shape), _full(wg.shape), _full(bg.shape), _full(wgT.shape), _full(bgT.shape),
                  _full(qg.shape), _full(kg.shape), row(LANES), row(LANES)],
        out_specs=[o[2] for o in outs],
        out_shape=[jax.ShapeDtypeStruct(o[0], o[1]) for o in outs],
        compiler_params=_cp(("arbitrary",)),
        name="ev_in",
    )(x, mod, g1, wm, bm, wkT, bkT, wg, bg, wgT, bgT, qg, kg, cos, sin)


def _attn_body(q_ref, k_ref, v_ref, o_ref, m_sc, acc_sc):
    j = pl.program_id(1)
    group = ATTN_HEADS // ATTN_KV_HEADS

    def scores(g):
        q = jnp.concatenate([q_ref[:, (g * group + i) * 128:(g * group + i + 1) * 128] for i in range(group)], axis=0)
        return _dot_nt(q, k_ref[:, g * 128:(g + 1) * 128])

    def renew_reference():
        for g in range(ATTN_KV_HEADS):
            s = scores(g)
            m_prev = m_sc[g]
            m_new = jnp.maximum(m_prev, jnp.max(s, axis=-1, keepdims=True))
            p = jnp.exp2(s - m_new).astype(bf16)
            acc_sc[g] = jnp.exp2(m_prev - m_new) * acc_sc[g] + _dot(p, v_ref[:, g * 256:(g + 1) * 256])
            m_sc[g] = m_new

    @pl.when(j == 0)
    def _():
        acc_sc[...] = jnp.zeros(acc_sc.shape, f32)
        for g in range(ATTN_KV_HEADS):
            q = jnp.concatenate([q_ref[:, (g * group + i) * 128:(g * group + i + 1) * 128] for i in range(group)],
                                axis=0)
            m_sc[g] = jnp.max(_dot_nt(q, k_ref[0:ATTN_PROBE, g * 128:(g + 1) * 128]), axis=-1, keepdims=True)

    pvs, zmax = [], None
    for g in range(ATTN_KV_HEADS):
        z = scores(g) - m_sc[g]
        pvs.append(_dot(jnp.exp2(z).astype(bf16), v_ref[:, g * 256:(g + 1) * 256]))
        zm = jnp.max(z)
        zmax = zm if zmax is None else jnp.maximum(zmax, zm)
    in_range = zmax <= ATTN_SLACK

    @pl.when(in_range)
    def _():
        for g in range(ATTN_KV_HEADS):
            acc_sc[g] = acc_sc[g] + pvs[g]

    @pl.when(jnp.logical_not(in_range))
    def _():
        renew_reference()

    @pl.when(j == pl.num_programs(1) - 1)
    def _():
        tq = q_ref.shape[0]
        for g in range(ATTN_KV_HEADS):
            acc = acc_sc[g]
            o = (acc[:, 0:128] / acc[:, 128:256]).astype(bf16)
            for i in range(group):
                o_ref[:, (g * group + i) * 128:(g * group + i + 1) * 128] = o[i * tq:(i + 1) * tq]


def _attention(q, k, v, tq, tk):
    nq, nk = q.shape[0], k.shape[0]
    assert nq % tq == 0 and nk % tk == 0
    rows = tq * (ATTN_HEADS // ATTN_KV_HEADS)
    return pl.pallas_call(
        _attn_body,
        grid=(nq // tq, nk // tk),
        in_specs=[pl.BlockSpec((tq, 512), lambda i, j: (i, 0)),
                  pl.BlockSpec((tk, 256), lambda i, j: (j, 0)),
                  pl.BlockSpec((tk, 512), lambda i, j: (j, 0))],
        out_specs=pl.BlockSpec((tq, 512), lambda i, j: (i, 0)),
        out_shape=jax.ShapeDtypeStruct((nq, 512), bf16),
        scratch_shapes=[pltpu.VMEM((ATTN_KV_HEADS, rows, 1), f32), pltpu.VMEM((ATTN_KV_HEADS, rows, 256), f32)],
        compiler_params=_cp(("arbitrary", "arbitrary")),
        name="attention",
    )(q, k, v)


def _mlstm_body(qf_ref, kTf_ref, vf_ref, gcf_ref, gTf_ref, qb_ref, kTb_ref, vb_ref, gcb_ref, gTb_ref,
                fbr_ref, fbc_ref, c0_ref, m0_ref, hf_ref, hb_ref, c_out_ref, m_out_ref, c_sc, m_sc):
    j = pl.program_id(0)

    @pl.when(j == 0)
    def _():
        c_sc[...] = c0_ref[...]
        m_sc[...] = m0_ref[...]

    ti = lax.broadcasted_iota(i32, (CHUNK, CHUNK), 0)
    si = lax.broadcasted_iota(i32, (CHUNK, CHUNK), 1)
    ones_c = jnp.ones((CHUNK, 128), f32)
    for d in range(2):
        q_ref, kT_ref, v_ref, gc_ref, gT_ref, h_ref = ((qf_ref, kTf_ref, vf_ref, gcf_ref, gTf_ref, hf_ref) if d == 0 else
                                                      (qb_ref, kTb_ref, vb_ref, gcb_ref, gTb_ref, hb_ref))
        mask = (si <= ti) if d == 0 else (si >= ti)
        lmat = mask.astype(f32)
        umat = ((ti <= si) if d == 0 else (ti >= si)).astype(f32)
        last = CHUNK - 1 if d == 0 else 0
        nsub = gc_ref.shape[0] // CHUNK
        for u in (range(nsub) if d == 0 else range(nsub - 1, -1, -1)):
            rows = slice(u * CHUNK, (u + 1) * CHUNK)
            gc = gc_ref[rows, :]
            gT = gT_ref[:, rows]
            fcol = _log_sigmoid(gc + fbr_ref[...])
            frow = _log_sigmoid(gT + fbc_ref[...])
            bcol = jnp.dot(lmat, fcol, precision=HIGHEST, preferred_element_type=f32)
            brow = jnp.dot(frow, umat, precision=HIGHEST, preferred_element_type=f32)
            for h in range(MLSTM_HEADS):
                idx = d * MLSTM_HEADS + h
                ji, jf = idx, 8 + idx
                b_t = bcol[:, jf:jf + 1]
                i_t = gc[:, ji:ji + 1]
                logd = jnp.where(mask, b_t - brow[jf:jf + 1, :] + gT[ji:ji + 1, :], -jnp.inf)
                m_old = m_sc[idx][:, 0:1]
                m_row = jnp.maximum(b_t + m_old, jnp.max(logd, axis=-1, keepdims=True))
                w_inter = jnp.exp(b_t + m_old - m_row)
                dm = jnp.exp(logd - m_row)
                qh = q_ref[rows, h * MLSTM_DK:(h + 1) * MLSTM_DK]
                kTh = kT_ref[h * MLSTM_DK:(h + 1) * MLSTM_DK, rows]
                vaug = jnp.concatenate([v_ref[rows, h * 128:(h + 1) * 128].astype(f32), ones_c], axis=1)
                s = _dot(qh, kTh) * dm
                intra = _dot(s.astype(bf16), vaug.astype(bf16))
                c_old = c_sc[idx]
                inter = _dot(qh, c_old.astype(bf16))
                num = intra[:, 0:128] + w_inter * inter[:, 0:128]
                den = intra[:, 128:129] + w_inter * inter[:, 128:129]
                h_ref[rows, h * 128:(h + 1) * 128] = num / jnp.maximum(jnp.abs(den), jnp.exp(-m_row))
                b_last = b_t[last:last + 1, :]
                lw = b_last - b_t + i_t
                m_new = jnp.maximum(b_last + m_old, jnp.max(lw, axis=0, keepdims=True))
                w = jnp.exp(lw - m_new)
                decay = jnp.exp(b_last + m_old - m_new)
                c_sc[idx] = decay * c_old + _dot(kTh, (w * vaug).astype(bf16))
                m_sc[idx] = jnp.broadcast_to(m_new, (1, 128))

    @pl.when(j == pl.num_programs(0) - 1)
    def _():
        c_out_ref[...] = c_sc[...]
        m_out_ref[...] = m_sc[...]


def _mlstm(mq, mkT, mv, gc, gT, fbr, fbc, c0, m0):
    n = mq.shape[0]
    tb = MLSTM_STEP_CHUNKS * CHUNK
    assert n % tb == 0
    nc = n // tb
    fwd = lambda j: j
    bwd = lambda j: nc - 1 - j

    def specs(ix):
        return [pl.BlockSpec((tb, 256), lambda j: (ix(j), 0)),
                pl.BlockSpec((256, tb), lambda j: (0, ix(j))),
                pl.BlockSpec((tb, 512), lambda j: (ix(j), 0)),
                pl.BlockSpec((tb, LANES), lambda j: (ix(j), 0)),
                pl.BlockSpec((16, tb), lambda j: (0, ix(j)))]

    nh = 2 * MLSTM_HEADS
    return pl.pallas_call(
        _mlstm_body,
        grid=(nc,),
        in_specs=specs(fwd) + specs(bwd) + [_full(fbr.shape), _full(fbc.shape), _full(c0.shape), _full(m0.shape)],
        out_specs=[pl.BlockSpec((tb, 512), lambda j: (fwd(j), 0)),
                   pl.BlockSpec((tb, 512), lambda j: (bwd(j), 0)),
                   _full(c0.shape), _full(m0.shape)],
        out_shape=[jax.ShapeDtypeStruct((n, 512), f32), jax.ShapeDtypeStruct((n, 512), f32),
                   jax.ShapeDtypeStruct(c0.shape, f32), jax.ShapeDtypeStruct(m0.shape, f32)],
        scratch_shapes=[pltpu.VMEM((nh, MLSTM_DK, 256), f32), pltpu.VMEM((nh, 1, 128), f32)],
        compiler_params=_cp(("arbitrary",)),
        name="mlstm",
    )(mq, mkT, mv, gc, gT, mq, mkT, mv, gc, gT, fbr, fbc, c0, m0)


def _post_block(x, y, mod_ref, g2_ref, rw_ref, rb_ref, run_sc, h_ref, t_ref, meta_ref, cnt_ref, d):
    tm = x.shape[0]
    i = pl.program_id(0)

    @pl.when(i == 0)
    def _():
        run_sc[...] = jnp.zeros(run_sc.shape, f32)

    h1 = x + mod_ref[0:1, 2 * d:3 * d] * y
    h_ref[...] = h1
    t = _rms(h1, g2_ref[...]) * (1.0 + mod_ref[0:1, 4 * d:5 * d]) + mod_ref[0:1, 3 * d:4 * d]
    for j in range(d // LANES):
        t_ref[pl.ds(j, tm, stride=d // LANES), :] = t[:, j * LANES:(j + 1) * LANES]
    t_hi = t.astype(bf16)
    t_lo = (t - t_hi.astype(f32)).astype(bf16)
    logits = _dot(t_hi, rw_ref[0]) + (_dot(t_hi, rw_ref[1]) + _dot(t_lo, rw_ref[0])) + rb_ref[...]
    lane = lax.broadcasted_iota(i32, (tm, LANES), 1)
    lanef = lane.astype(f32)
    neg = jnp.float32(-jnp.inf)
    cur = jnp.where(lane < N_EXPERTS, logits, neg)
    tops, ids, hots = [], [], []
    for _ in range(TOP_K):
        mx = jnp.max(cur, axis=-1, keepdims=True)
        idx = jnp.min(jnp.where(cur == mx, lanef, 1e9), axis=-1, keepdims=True)
        hot = lanef == idx
        cur = jnp.where(hot, neg, cur)
        tops.append(mx)
        ids.append(idx)
        hots.append(hot)
    es = [jnp.exp(tk - tops[0]) for tk in tops]
    den = es[0] + es[1] + es[2] + es[3]
    cnt = jnp.zeros((tm, LANES), f32)
    for hot in hots:
        cnt = cnt + hot.astype(f32)
    ri = lax.broadcasted_iota(i32, (tm, tm), 0)
    ci = lax.broadcasted_iota(i32, (tm, tm), 1)
    strict = (ci < ri).astype(bf16)
    before = _dot(strict, cnt.astype(bf16)) + run_sc[...]
    meta = jnp.zeros((tm, LANES), f32)
    for k in range(TOP_K):
        rank = jnp.sum(jnp.where(hots[k], before, 0.0), axis=-1, keepdims=True)
        meta = jnp.where(lane == k, ids[k], meta)
        meta = jnp.where(lane == TOP_K + k, rank, meta)
        meta = jnp.where(lane == 2 * TOP_K + k, es[k] / den, meta)
    meta_ref[...] = meta
    run_sc[...] = run_sc[...] + jnp.sum(cnt, axis=0, keepdims=True)

    @pl.when(i == pl.num_programs(0) - 1)
    def _():
        cnt_ref[...] = jnp.broadcast_to(run_sc[...], cnt_ref.shape).astype(i32)


def _post_specs(n, d, tm):
    row = lambda w: pl.BlockSpec((tm, w), lambda i: (i, 0))
    out_specs = [row(d), pl.BlockSpec((tm * (d // LANES), LANES), lambda i: (i, 0)), row(LANES), _full((8, LANES))]
    out_shape = [jax.ShapeDtypeStruct((n, d), f32), jax.ShapeDtypeStruct((n * (d // LANES), LANES), f32),
                 jax.ShapeDtypeStruct((n, LANES), f32), jax.ShapeDtypeStruct((8, LANES), i32)]
    return out_specs, out_shape


def _ev_out_body(att_ref, hf_ref, hb_ref, og_ref, x_ref, mod_ref, hg_ref, w_ref, b_ref, g2_ref, rw_ref, rb_ref,
                 h_ref, t_ref, meta_ref, cnt_ref, run_sc, *, d):
    hs = hf_ref[...] + hb_ref[...]
    og = og_ref[...]
    parts = [att_ref[...]]
    for h in range(MLSTM_HEADS):
        sl = slice(h * 128, (h + 1) * 128)
        hh = hs[:, sl]
        hn = hh * lax.rsqrt(jnp.mean(hh * hh, axis=-1, keepdims=True) + EPS) * hg_ref[:, sl]
        parts.append((jax.nn.sigmoid(og[:, sl]) * hn).astype(bf16))
    y = _dot(jnp.concatenate(parts, axis=1), w_ref[...]) + b_ref[...]
    _post_block(x_ref[...], y, mod_ref, g2_ref, rw_ref, rb_ref, run_sc, h_ref, t_ref, meta_ref, cnt_ref, d)


def _ev_out(att, hf, hb, og, x, mod, hg, w, b, g2, rw, rb, tm):
    n, d = x.shape
    row = lambda wd: pl.BlockSpec((tm, wd), lambda i: (i, 0))
    out_specs, out_shape = _post_specs(n, d, tm)
    return pl.pallas_call(
        functools.partial(_ev_out_body, d=d),
        grid=(n // tm,),
        in_specs=[row(512), row(512), row(512), row(512), row(d), _full(mod.shape), _full(hg.shape), _full(w.shape),
                  _full(b.shape), _full(g2.shape), _full(rw.shape), _full(rb.shape)],
        out_specs=out_specs, out_shape=out_shape,
        scratch_shapes=[pltpu.VMEM((1, LANES), f32)],
        compiler_params=_cp(("arbitrary",)),
        name="ev_out",
    )(att, hf, hb, og, x, mod, hg, w, b, g2, rw, rb)


ROW_UNROLL = 8


TOKEN_ROWS = 8


def _dispatch_body(dest_ref, t_ref, buf_ref, sem):
    tm = t_ref.shape[0] // TOKEN_ROWS

    def issue(rb, c):
        base = pl.multiple_of(rb * (ROW_UNROLL * TOKEN_ROWS), ROW_UNROLL * TOKEN_ROWS)
        for u in range(ROW_UNROLL):
            src = t_ref.at[pl.ds(base + u * TOKEN_ROWS, TOKEN_ROWS)]
            for k in range(TOP_K):
                dest = pl.multiple_of(dest_ref[(rb * ROW_UNROLL + u) * TOP_K + k], TOKEN_ROWS)
                pltpu.make_async_copy(src, buf_ref.at[pl.ds(dest, TOKEN_ROWS)], sem).start(priority=k % 2)
        return c

    lax.fori_loop(0, tm // ROW_UNROLL, issue, 0)
    for k in range(TOP_K):
        pltpu.make_async_copy(t_ref, buf_ref.at[pl.ds(0, tm * TOKEN_ROWS)], sem).wait()


def _dispatch(dest, t, tm):
    rows, l = t.shape
    return pl.pallas_call(
        _dispatch_body,
        grid=(rows // (tm * TOKEN_ROWS),),
        in_specs=[pl.BlockSpec((tm * TOP_K,), lambda i: (i,), memory_space=pltpu.SMEM),
                  pl.BlockSpec((tm * TOKEN_ROWS, l), lambda i: (i, 0))],
        out_specs=pl.BlockSpec(memory_space=pl.ANY),
        out_shape=jax.ShapeDtypeStruct((rows * TOP_K, l), f32),
        scratch_shapes=[pltpu.SemaphoreType.DMA(())],
        compiler_params=_cp(("arbitrary",)),
        name="moe_dispatch",
    )(dest, t)


def _expert_body(blk_ref, exp_ref, lo_ref, act_ref, chg_ref, nxt_ref, x_ref, wgu_hbm, bgu_ref, wd_hbm, bd_ref, o_ref,
                 wgu_f32, wd_f32, wgu_sc, wd_sc, sem, *, de, layer):
    i = pl.program_id(0)

    def weight_copies(e):
        return (pltpu.make_async_copy(wgu_hbm.at[layer, e], wgu_f32, sem.at[0]),
                pltpu.make_async_copy(wd_hbm.at[layer, e], wd_f32, sem.at[1]))

    @pl.when(i == 0)
    def _():
        for cp in weight_copies(exp_ref[0]):
            cp.start()

    @pl.when(chg_ref[i] == 1)
    def _():
        for cp in weight_copies(exp_ref[i]):
            cp.wait()
        wgu_sc[...] = wgu_f32[...].astype(bf16)
        wd_sc[...] = wd_f32[...].astype(bf16)

        @pl.when(nxt_ref[i] >= 0)
        def _():
            for cp in weight_copies(nxt_ref[i]):
                cp.start()

    ns = TOKEN_ROWS
    lo = lo_ref[i]

    def run(m, row0):
        base = row0 * ns
        x = jnp.concatenate([x_ref[pl.ds(base + j, m, stride=ns), :] for j in range(ns)], axis=1).astype(bf16)
        gu = _dot(x, wgu_sc[...]) + bgu_ref[0, 0]
        g = jnp.minimum(gu[:, :de], SWIGLU_LIMIT)
        u = jnp.clip(gu[:, de:], -SWIGLU_LIMIT, SWIGLU_LIMIT)
        a = g * jax.nn.sigmoid(SWIGLU_ALPHA * g) * (u + 1.0)
        y = _dot(a.astype(bf16), wd_sc[...]) + bd_ref[0, 0]

        @pl.when(lo == 0)
        def _():
            for j in range(ns):
                o_ref[pl.ds(j, m, stride=ns), :] = y[:, j * LANES:(j + 1) * LANES]
            if m < MOE_BM:
                o_ref[m * ns:MOE_BM * ns, :] = jnp.zeros(((MOE_BM - m) * ns, LANES), f32)

        @pl.when(lo > 0)
        def _():
            keep = (lax.broadcasted_iota(i32, (m, LANES), 0) + row0) >= lo
            for j in range(ns):
                rows = pl.ds(base + j, m, stride=ns)
                o_ref[rows, :] = jnp.where(keep, y[:, j * LANES:(j + 1) * LANES], o_ref[rows, :])

    @pl.when(act_ref[i] == 1)
    def _():
        run(MOE_BM, 0)

    @pl.when(act_ref[i] >= 2)
    def _():
        run(MOE_BM // 2, pl.multiple_of((act_ref[i] - 2) * (MOE_BM // 2), MOE_BM // 2))


def _experts(sched, buf, wgu, bgu, wd, bd, layer):
    nl, ne, d, de2 = wgu.shape
    assert d == TOKEN_ROWS * LANES
    de = de2 // 2
    n_items = sched[0].shape[0]
    wmap = lambda i, b, e, l, a, c, n: (layer, e[i], 0, 0)
    rows = pl.BlockSpec((MOE_BM * TOKEN_ROWS, LANES), lambda i, b, e, l, a, c, n: (b[i], 0))
    return pl.pallas_call(
        functools.partial(_expert_body, de=de, layer=layer),
        grid_spec=pltpu.PrefetchScalarGridSpec(
            num_scalar_prefetch=6,
            grid=(n_items,),
            in_specs=[rows,
                      pl.BlockSpec(memory_space=pl.ANY),
                      pl.BlockSpec((1, 1, 1, de2), wmap),
                      pl.BlockSpec(memory_space=pl.ANY),
                      pl.BlockSpec((1, 1, 1, d), wmap)],
            out_specs=rows,
            scratch_shapes=[pltpu.VMEM((d, de2), f32), pltpu.VMEM((de, d), f32),
                            pltpu.VMEM((d, de2), bf16), pltpu.VMEM((de, d), bf16),
                            pltpu.SemaphoreType.DMA((2,))]),
        out_shape=jax.ShapeDtypeStruct(buf.shape, f32),
        compiler_params=_cp(("arbitrary",)),
        name="moe_experts",
    )(*sched, buf, wgu, bgu.reshape(nl, ne, 1, de2), wd, bd.reshape(nl, ne, 1, d))


def _combine_body(dest_ref, eo_ref, meta_ref, h_ref, mod_ref, fg_ref, o_ref, stage, sem, *, d, final):
    tm = h_ref.shape[0]
    ns = TOKEN_ROWS

    def issue(rb, c):
        base = pl.multiple_of(rb * (ROW_UNROLL * ns), ROW_UNROLL * ns)
        for u in range(ROW_UNROLL):
            for k in range(TOP_K):
                src = pl.multiple_of(dest_ref[(rb * ROW_UNROLL + u) * TOP_K + k], ns)
                pltpu.make_async_copy(eo_ref.at[pl.ds(src, ns)], stage.at[pl.ds(k * tm * ns + base + u * ns, ns)],
                                      sem).start(priority=k % 2)
        return c

    lax.fori_loop(0, tm // ROW_UNROLL, issue, 0)
    for k in range(TOP_K):
        pltpu.make_async_copy(eo_ref.at[pl.ds(0, tm * ns)], stage.at[pl.ds(k * tm * ns, tm * ns)], sem).wait()
    meta = meta_ref[...]
    f = None
    for k in range(TOP_K):
        rows = jnp.concatenate([stage[pl.ds(k * tm * ns + j, tm, stride=ns), :] for j in range(ns)], axis=1)
        term = meta[:, 2 * TOP_K + k:2 * TOP_K + k + 1] * rows
        f = term if f is None else f + term
    h = h_ref[...] + mod_ref[0:1, 5 * d:6 * d] * f
    if final:
        h = _rms(h, fg_ref[...])
    o_ref[...] = h


def _combine(dest, eo, meta, h, mod, fg, tm, final):
    n, d = h.shape
    return pl.pallas_call(
        functools.partial(_combine_body, d=d, final=final),
        grid=(n // tm,),
        in_specs=[pl.BlockSpec((tm * TOP_K,), lambda i: (i,), memory_space=pltpu.SMEM),
                  pl.BlockSpec(memory_space=pl.ANY),
                  pl.BlockSpec((tm, LANES), lambda i: (i, 0)),
                  pl.BlockSpec((tm, d), lambda i: (i, 0)),
                  _full(mod.shape), _full(fg.shape)],
        out_specs=pl.BlockSpec((tm, d), lambda i: (i, 0)),
        out_shape=jax.ShapeDtypeStruct((n, d), f32),
        scratch_shapes=[pltpu.VMEM((TOP_K * tm * TOKEN_ROWS, LANES), f32), pltpu.SemaphoreType.DMA(())],
        compiler_params=_cp(("arbitrary",)),
        name="moe_combine",
    )(dest, eo, meta, h, mod, fg)


def _moe(t, meta, counts, h, mod, fg, wgu, bgu, wd, bd, layer, final):
    n, d = h.shape
    m = n * TOP_K
    nblk = m // MOE_BM
    n_items = nblk + N_EXPERTS - 1
    counts = counts.astype(i32)
    ends = jnp.cumsum(counts)
    starts = ends - counts
    first_blk = starts // MOE_BM
    last_blk = jnp.maximum(ends - 1, starts) // MOE_BM
    per = jnp.where(counts > 0, last_blk - first_blk + 1, 0)
    cum = jnp.cumsum(per)
    off = cum - per
    total = cum[-1]
    it = jnp.minimum(jnp.arange(n_items, dtype=i32), jnp.maximum(total - 1, 0))
    e_it = jnp.minimum(jnp.sum((cum[None, :] <= it[:, None]).astype(i32), axis=1), N_EXPERTS - 1)
    eids = jnp.arange(N_EXPERTS, dtype=i32)[None, :]
    pick = lambda v: jnp.sum(jnp.where(e_it[:, None] == eids, v[None, :], 0), axis=1)
    blk_it = pick(first_blk) + it - pick(off)
    lo_it = jnp.maximum(pick(starts) - blk_it * MOE_BM, 0)
    hi_it = jnp.minimum(pick(ends) - blk_it * MOE_BM, MOE_BM)
    half = MOE_BM // 2
    mode = jnp.where(hi_it <= half, 2, jnp.where(lo_it >= half, 3, 1))
    active = jnp.where(jnp.arange(n_items, dtype=i32) < total, mode, 0).astype(i32)
    changed = jnp.concatenate([jnp.ones((1,), i32), (e_it[1:] != e_it[:-1]).astype(i32)])
    later = (eids > eids.T) & (counts[None, :] > 0)
    nxt_e = jnp.min(jnp.where(later, eids, N_EXPERTS), axis=1)
    nxt_it = pick(jnp.where(nxt_e < N_EXPERTS, nxt_e, -1))
    sched = (blk_it.astype(i32), e_it.astype(i32), lo_it.astype(i32), active, changed, nxt_it.astype(i32))
    ids = meta[:, 0:TOP_K].astype(i32)
    ranks = meta[:, TOP_K:2 * TOP_K].astype(i32)
    dest = (jnp.sum(jnp.where(ids[:, :, None] == eids[None], starts[None, None, :], 0), axis=-1) + ranks).reshape(-1)
    dest = dest * TOKEN_ROWS
    buf = _dispatch(dest, t, 256)
    eo = _experts(sched, buf, wgu, bgu, wd, bd, layer)
    return _combine(dest, eo, meta, h, mod, fg, 256, final)


def _rope_tables(n):
    rows = n // GRID_W
    row = np.repeat(np.arange(rows), GRID_W)
    col = np.tile(np.arange(GRID_W), rows)
    pos = np.stack([row, col], axis=-1).astype(np.float64)
    inv_freq = ROPE_THETA ** (-np.arange(ROPE_PAIRS, dtype=np.float64) / ROPE_PAIRS)
    ang = pos[:, :, None] * inv_freq
    c, s = np.cos(ang), np.sin(ang)
    cos = np.concatenate([c[:, 0], c[:, 0], c[:, 1], c[:, 1]], axis=-1).astype(np.float32)
    sin = np.concatenate([-s[:, 0], s[:, 0], -s[:, 1], s[:, 1]], axis=-1).astype(np.float32)
    return jnp.asarray(cos), jnp.asarray(sin)


def _router_params(router_w, router_b):
    d = router_w.shape[0]
    rw = jnp.zeros((d, LANES), f32).at[:, :N_EXPERTS].set(router_w)
    rb = jnp.zeros((1, LANES), f32).at[0, :N_EXPERTS].set(router_b)
    rw_hi = rw.astype(bf16)
    rw_lo = (rw - rw_hi.astype(f32)).astype(bf16)
    return jnp.stack([rw_hi, rw_lo]), rb


def _even_layer(x, ctx, mod, g1, g2, w_in, b_in, q_gain, k_gain, f_bias, h_gain, w_out, b_out, router_w, router_b):
    n, d = x.shape
    nctx = ctx.shape[0]
    cols = lambda a, b: w_in[:, a:b]
    wm = jnp.concatenate([cols(0, 1280), cols(1536, 2048), cols(2064, 2576)], axis=1).astype(bf16)
    bm = jnp.concatenate([b_in[0:1280], b_in[1536:2048], b_in[2064:2576]])[None, :]
    wkT = cols(1280, 1536).T.astype(bf16)
    bkT = b_in[1280:1536][:, None]
    wgate = cols(2048, 2064)
    wg = jnp.zeros((d, LANES), f32).at[:, :16].set(wgate).astype(bf16)
    bg = jnp.zeros((1, LANES), f32).at[0, :16].set(b_in[2048:2064])
    wgT = wgate.T.astype(bf16)
    bgT = b_in[2048:2064][:, None]
    wts = (wm, bm, wkT, bkT, wg, bg, wgT, bgT, q_gain[None, :], k_gain[None, :])
    cos, sin = _rope_tables(n)
    lat = _ev_in(x, mod, g1, wts, cos, sin, 0, min(512, n))
    cx = _ev_in(ctx, mod, g1, wts, jnp.ones((nctx, LANES), f32), jnp.zeros((nctx, LANES), f32), 1, nctx)
    q, k, v, mq, mkT, mv, og, gc, gT = lat
    _, kc, vc, mqc, mkTc, mvc, _, gcc, gTc = cx

    v_all = jnp.concatenate([v, vc], axis=0)
    ones = jnp.ones((n + nctx, 128), bf16)
    v_aug = jnp.concatenate([v_all[:, 0:128], ones, v_all[:, 128:256], ones], axis=1)
    att = _attention(q, jnp.concatenate([k, kc], axis=0), v_aug, 1024 if n % 1024 == 0 else min(512, n), 1280)

    fb = f_bias.reshape(-1)
    fbr = jnp.zeros((1, LANES), f32).at[0, 8:16].set(fb)
    fbc = jnp.zeros((16, LANES), f32).at[8:16, :].set(jnp.broadcast_to(fb[:, None], (8, LANES)))
    nh = 2 * MLSTM_HEADS
    c0 = jnp.zeros((nh, MLSTM_DK, 256), f32)
    m0 = jnp.zeros((nh, 1, 128), f32)
    _, _, c1, m1 = _mlstm(mqc, mkTc, mvc, gcc, gTc, fbr, fbc, c0, m0)
    hf, hb, _, _ = _mlstm(mq, mkT, mv, gc, gT, fbr, fbc, c1, m1)

    rw, rb = _router_params(router_w, router_b)
    return _ev_out(att, hf, hb, og, x, mod, h_gain[None, :], w_out.astype(bf16), b_out[None, :], g2, rw, rb,
                   min(512, n))


FFT_N2 = 128
CH_TILE = 8


def _ct_shape(ch, nblocks):
    return jax.ShapeDtypeStruct((ch // CH_TILE, nblocks * CH_TILE, FFT_N2), f32)


def _ct_block(ch, per):
    return pl.BlockSpec((ch // CH_TILE, per * CH_TILE, FFT_N2), lambda i: (0, i, 0))


def _ct_slab(nblocks):
    return pl.BlockSpec((1, nblocks * CH_TILE, FFT_N2), lambda i: (i, 0, 0))


def _ct_channel(ref, c, nblocks):
    return ref.at[0], pl.ds(c, nblocks, stride=CH_TILE)


def _hy_in_body(x_ref, mod_ref, g1_ref, wT_ref, b_ref, zT_ref, *, d):
    a = (_rms(x_ref[...], g1_ref[...]) * (1.0 + mod_ref[0:1, d:2 * d]) + mod_ref[0:1, 0:d]).astype(bf16)
    zT_ref[...] = _dot_nt(wT_ref[...], a) + b_ref[...]


def _hy_in(x, mod, g1, wT, b, tm):
    n, d = x.shape
    c3 = wT.shape[0]
    return pl.pallas_call(
        functools.partial(_hy_in_body, d=d),
        grid=(n // tm,),
        in_specs=[pl.BlockSpec((tm, d), lambda i: (i, 0)), _full(mod.shape), _full(g1.shape), _full(wT.shape),
                  _full(b.shape)],
        out_specs=pl.BlockSpec((c3, tm), lambda i: (0, i)),
        out_shape=jax.ShapeDtypeStruct((c3, n), f32),
        compiler_params=_cp(("arbitrary",)),
        name="hy_in",
    )(x, mod, g1, wT, b)


def _hy_conv_body(z_ref, zp_ref, zn_ref, cw_ref, vp_ref, x0_ref, *, ch):
    i = pl.program_id(0)
    tt = z_ref.shape[1]
    has_prev = (i > 0).astype(f32)
    has_next = (i < pl.num_programs(0) - 1).astype(f32)
    lane = lax.broadcasted_iota(i32, (ch, tt), 1)

    def conv(g):
        rows = slice(g * ch, (g + 1) * ch)
        z = z_ref[rows, :]
        pcol = zp_ref[rows, FFT_N2 - 1:FFT_N2] * has_prev
        ncol = zn_ref[rows, 0:1] * has_next
        zm1 = jnp.where(lane == 0, pcol, pltpu.roll(z, 1, 1))
        zp1 = jnp.where(lane == tt - 1, ncol, pltpu.roll(z, tt - 1, 1))
        cw = cw_ref[rows, :]
        return cw[:, 0:1] * zm1 + cw[:, 1:2] * z + cw[:, 2:3] * zp1 + cw[:, 3:4]

    x0 = conv(0)
    vp = conv(2) * conv(1)
    for j in range(tt // FFT_N2):
        rows = slice(j * CH_TILE, (j + 1) * CH_TILE)
        x0_ref[:, rows, :] = x0[:, j * FFT_N2:(j + 1) * FFT_N2].reshape(ch // CH_TILE, CH_TILE, FFT_N2)
        vp_ref[:, rows, :] = vp[:, j * FFT_N2:(j + 1) * FFT_N2].reshape(ch // CH_TILE, CH_TILE, FFT_N2)


def _hy_conv(zT, cw, tt):
    c3, n = zT.shape
    ch = c3 // 3
    nb = n // FFT_N2
    per = tt // FFT_N2
    return pl.pallas_call(
        functools.partial(_hy_conv_body, ch=ch),
        grid=(n // tt,),
        in_specs=[pl.BlockSpec((c3, tt), lambda i: (0, i)),
                  pl.BlockSpec((c3, FFT_N2), lambda i: (0, jnp.maximum(i * per - 1, 0))),
                  pl.BlockSpec((c3, FFT_N2), lambda i: (0, jnp.minimum(i * per + per, nb - 1))),
                  _full(cw.shape)],
        out_specs=[_ct_block(ch, per), _ct_block(ch, per)],
        out_shape=[_ct_shape(ch, nb), _ct_shape(ch, nb)],
        compiler_params=_cp(("arbitrary",)),
        name="hy_conv",
    )(zT, zT, zT, cw)


def _hy_filt_body(zz_ref, w1_ref, b1_ref, fr_ref, w2_ref, b2_ref, w3_ref, b3_ref, w4T_ref, dl_ref, tr_ref, mr_ref,
                  o_ref):
    hdot = lambda a, b: jnp.dot(a, b, precision=HIGHEST, preferred_element_type=f32)
    fr = fr_ref[...]
    h = jnp.sin(fr * (hdot(w1_ref[...], zz_ref[...]) + b1_ref[...]))
    h = jnp.sin(fr * (hdot(w2_ref[...], h) + b2_ref[...]))
    h = jnp.sin(fr * (hdot(w3_ref[...], h) + b3_ref[...]))
    taps = _dot(w4T_ref[0], h.astype(bf16))
    ch = taps.shape[0]
    dl = dl_ref[...]
    for j in range(tr_ref.shape[0]):
        blk = taps[:, j * FFT_N2:(j + 1) * FFT_N2] * jnp.exp(-dl * tr_ref[j]) * mr_ref[j]
        o_ref[:, j * CH_TILE:(j + 1) * CH_TILE, :] = blk.reshape(ch // CH_TILE, CH_TILE, FFT_N2)


def _hy_filt(zz, w1, b1, fr, w2, b2, w3, b3, w4T, deltas, trow, mrow):
    n2l = zz.shape[1]
    nb2 = n2l // FFT_N2
    ch = w4T.shape[1]
    per = 4 if nb2 % 8 == 0 else 1
    steps = nb2 // per
    return pl.pallas_call(
        _hy_filt_body,
        grid=(steps,),
        in_specs=[pl.BlockSpec((LANES, per * FFT_N2), lambda i: (0, i)), _full(w1.shape), _full(b1.shape),
                  _full(fr.shape), _full(w2.shape), _full(b2.shape), _full(w3.shape), _full(b3.shape),
                  pl.BlockSpec((1,) + w4T.shape[1:], lambda i: ((i >= steps // 2).astype(i32), 0, 0)),
                  _full(deltas.shape),
                  pl.BlockSpec((per, 1, FFT_N2), lambda i: (i, 0, 0)),
                  pl.BlockSpec((per, 1, FFT_N2), lambda i: (i, 0, 0))],
        out_specs=_ct_block(ch, per),
        out_shape=_ct_shape(ch, nb2),
        compiler_params=_cp(("arbitrary",)),
        name="hy_filt",
    )(zz, w1, b1, fr, w2, b2, w3, b3, w4T, deltas, trow, mrow)


def _dft_stage12(x_ref, f1, tr_ref, ti_ref, r2_ref):
    ct = CH_TILE
    n1 = f1.shape[0] // 2
    n_in = f1.shape[1]
    xs = []
    for c in range(ct):
        slab, rows = _ct_channel(x_ref, c, n_in)
        xs.append(slab[rows, :])
    x = jnp.concatenate(xs, axis=1).astype(bf16)
    a = _dot(f1, x)
    tr, ti = tr_ref[...], ti_ref[...]
    lhs = []
    for c in range(ct):
        ar = a[0:n1, c * FFT_N2:(c + 1) * FFT_N2]
        ai = a[n1:2 * n1, c * FFT_N2:(c + 1) * FFT_N2]
        lhs.append(jnp.concatenate([ar * tr - ai * ti, ar * ti + ai * tr], axis=1).astype(bf16))
    xf = _dot(jnp.concatenate(lhs, axis=0), r2_ref[...])
    return [(xf[c * n1:(c + 1) * n1, 0:FFT_N2], xf[c * n1:(c + 1) * n1, FFT_N2:2 * FFT_N2]) for c in range(ct)]


def _hy_fftconv_body(v_ref, t_ref, f1_ref, tr_ref, ti_ref, r2_ref, r2i_ref, i2_ref, y_ref):
    ct = CH_TILE
    n1 = f1_ref.shape[0] // 2
    n_out = i2_ref.shape[0]
    tr, ti = tr_ref[...], ti_ref[...]
    f1 = f1_ref[...]
    g = _dft_stage12(t_ref, f1, tr_ref, ti_ref, r2_ref)
    x = _dft_stage12(v_ref, f1[:, 0:n_out], tr_ref, ti_ref, r2_ref)
    lhs = []
    for (xr, xi), (gr, gi) in zip(x, g):
        lhs.append(jnp.concatenate([xr * gr - xi * gi, xr * gi + xi * gr], axis=1).astype(bf16))
    b = _dot(jnp.concatenate(lhs, axis=0), r2i_ref[...])
    brs, bis = [], []
    for c in range(ct):
        br = b[c * n1:(c + 1) * n1, 0:FFT_N2]
        bi = b[c * n1:(c + 1) * n1, FFT_N2:2 * FFT_N2]
        brs.append(br * tr + bi * ti)
        bis.append(bi * tr - br * ti)
    rhs = jnp.concatenate([jnp.concatenate(brs, axis=1), jnp.concatenate(bis, axis=1)], axis=0).astype(bf16)
    y = _dot(i2_ref[...], rhs)
    for c in range(ct):
        slab, rows = _ct_channel(y_ref, c, n_out)
        slab[rows, :] = y[:, c * FFT_N2:(c + 1) * FFT_N2]


def _hy_fftconv(vp3, taps3, f1, tr, ti, r2, r2i, i2):
    vblk = _ct_slab(i2.shape[0])
    tblk = _ct_slab(f1.shape[1])
    return pl.pallas_call(
        _hy_fftconv_body,
        grid=(vp3.shape[0],),
        in_specs=[vblk, tblk, _full(f1.shape), _full(tr.shape), _full(ti.shape), _full(r2.shape),
                  _full(r2i.shape), _full(i2.shape)],
        out_specs=vblk,
        out_shape=jax.ShapeDtypeStruct(vp3.shape, f32),
        compiler_params=_cp(("arbitrary",)),
        name="hy_fftconv",
    )(vp3, taps3, f1, tr, ti, r2, r2i, i2)


def _hy_out_body(y_ref, vp_ref, x0_ref, sk_ref, w_ref, b_ref, x_ref, mod_ref, g2_ref, rw_ref, rb_ref,
                 h_ref, t_ref, meta_ref, cnt_ref, run_sc, *, d):
    sk = sk_ref[...]
    ch = sk.shape[0]

    def block(ref, j):
        return ref[:, j * CH_TILE:(j + 1) * CH_TILE, :].reshape(ch, FFT_N2)

    g = jnp.concatenate([((block(y_ref, j) + block(vp_ref, j) * sk) * block(x0_ref, j)).astype(bf16)
                         for j in range(y_ref.shape[1] // CH_TILE)], axis=1)
    y = _dot_tn(g, w_ref[...]) + b_ref[...]
    _post_block(x_ref[...], y, mod_ref, g2_ref, rw_ref, rb_ref, run_sc, h_ref, t_ref, meta_ref, cnt_ref, d)


def _hy_out(y3, vp3, x03, skip, w, b, x, mod, g2, rw, rb, tm):
    n, d = x.shape
    blk = _ct_block(skip.shape[0], tm // FFT_N2)
    out_specs, out_shape = _post_specs(n, d, tm)
    return pl.pallas_call(
        functools.partial(_hy_out_body, d=d),
        grid=(n // tm,),
        in_specs=[blk, blk, blk, _full(skip.shape), _full(w.shape), _full(b.shape),
                  pl.BlockSpec((tm, d), lambda i: (i, 0)), _full(mod.shape), _full(g2.shape), _full(rw.shape),
                  _full(rb.shape)],
        out_specs=out_specs, out_shape=out_shape,
        scratch_shapes=[pltpu.VMEM((1, LANES), f32)],
        compiler_params=_cp(("arbitrary",)),
        name="hy_out",
    )(y3, vp3, x03, skip, w, b, x, mod, g2, rw, rb)


def _dft_constants(n):
    nn = 2 * n
    n1 = nn // FFT_N2
    k1 = np.arange(n1)[:, None]
    phi = 2.0 * np.pi * (k1 * np.arange(n1)[None, :] % n1) / n1
    f1 = np.concatenate([np.cos(phi), -np.sin(phi)], axis=0)
    th = 2.0 * np.pi * (k1 * np.arange(FFT_N2)[None, :]) / nn
    tr, ti = np.cos(th), -np.sin(th)
    ps = 2.0 * np.pi * (np.arange(FFT_N2)[:, None] * np.arange(FFT_N2)[None, :] % FFT_N2) / FFT_N2
    c2, s2 = np.cos(ps), np.sin(ps)
    r2 = np.block([[c2, -s2], [s2, c2]])
    r2i = np.block([[c2, s2], [-s2, c2]])
    phi_i = phi[: n1 // 2, :]
    i2 = np.concatenate([np.cos(phi_i), -np.sin(phi_i)], axis=1) / nn
    as32 = lambda a: jnp.asarray(a.astype(np.float32))
    return dict(f1=as32(f1).astype(bf16), tr=as32(tr), ti=as32(ti),
                r2=as32(r2).astype(bf16), r2i=as32(r2i).astype(bf16), i2=as32(i2).astype(bf16))


def _filter_tables(n):
    t = np.linspace(0.0, 1.0, n)[:, None]
    w = (2.0 * np.pi / n) * np.arange(n)[:, None]
    f = np.linspace(1e-4, FILTER_BANDS - 1, FILTER_BANDS)
    z = np.concatenate([t, np.cos(f * w), -np.sin(f * w)], axis=-1)
    pos2 = np.arange(2 * n)
    pos = np.where(pos2 < n, pos2, np.minimum(2 * n - pos2, n - 1))
    zz = np.zeros((LANES, 2 * n), np.float32)
    zz[:FILTER_EMB, :] = z[pos].T
    trow = t[pos, 0].astype(np.float32).reshape(2 * n // FFT_N2, 1, FFT_N2)
    mrow = (pos2 != n).astype(np.float32).reshape(2 * n // FFT_N2, 1, FFT_N2)
    return jnp.asarray(zz), jnp.asarray(trow), jnp.asarray(mrow)


def _odd_layer(x, mod, g1, g2, w_in, b_in, conv_w, conv_b, fw1, fb1, ffreq, fw2, fb2, fw3, fb3, fw4, skip, w_out,
               b_out, router_w, router_b):
    n, d = x.shape
    ch = w_out.shape[0]
    tm = min(512, n)
    zT = _hy_in(x, mod, g1, w_in.T.astype(bf16), b_in[:, None], tm)
    cw = jnp.zeros((3 * ch, LANES), f32).at[:, 0:3].set(conv_w.T).at[:, 3].set(conv_b)
    vp3, x03 = _hy_conv(zT, cw, tm)

    zz, trow, mrow = _filter_tables(n)
    w1T = jnp.zeros((fw1.shape[1], LANES), f32).at[:, :FILTER_EMB].set(fw1.T)
    w4T = jnp.stack([fw4[:, :ch].T, fw4[:, ch:].T]).astype(bf16)
    deltas = jnp.abs(jnp.linspace(math.log(DECAY_TARGET) / SLOW_DECAY, math.log(DECAY_TARGET) / FAST_DECAY, ch,
                                  dtype=f32))[:, None]
    taps3 = _hy_filt(zz, w1T, fb1[:, None], ffreq[:, None], fw2.T, fb2[:, None], fw3.T, fb3[:, None], w4T, deltas,
                     trow, mrow)
    k = _dft_constants(n)
    y3 = _hy_fftconv(vp3, taps3, k["f1"], k["tr"], k["ti"], k["r2"], k["r2i"], k["i2"])
    rw, rb = _router_params(router_w, router_b)
    return _hy_out(y3, vp3, x03, skip[:, None], w_out.astype(bf16), b_out[None, :], x, mod, g2, rw, rb, tm)


def kernel(x, c, ctx, c_ctx, ada_w, ada_b, norm1_g, norm2_g, ev_w_in, ev_b_in, ev_q_gain, ev_k_gain, ev_f_bias,
           ev_h_gain, ev_w_out, ev_b_out, od_w_in, od_b_in, od_conv_w, od_conv_b, od_filt_w1, od_filt_b1,
           od_filt_freq, od_filt_w2, od_filt_b2, od_filt_w3, od_filt_b3, od_filt_w4, od_skip, od_w_out, od_b_out,
           router_w, router_b, moe_w_gu, moe_b_gu, moe_w_down, moe_b_down, final_g):
    bsz, n, d = x.shape
    assert bsz == 1 and c.shape[0] == 1
    depth = ada_w.shape[0]
    assert depth in (1, 2)
    cv = jnp.zeros((8, d), f32).at[0].set(c[0]).at[1].set(c_ctx)
    mods = _mods(cv, ada_w, ada_b)
    fg = final_g[None, :]

    h, t, meta, cnt = _even_layer(
        x[0], ctx[0], mods[0], norm1_g[0][None, :], norm2_g[0][None, :], ev_w_in[0], ev_b_in[0], ev_q_gain[0],
        ev_k_gain[0], ev_f_bias[0], ev_h_gain[0], ev_w_out[0], ev_b_out[0], router_w[0], router_b[0])
    h = _moe(t, meta, cnt[0, :N_EXPERTS], h, mods[0], fg, moe_w_gu, moe_b_gu, moe_w_down, moe_b_down, 0,
             final=(depth == 1))
    if depth == 2:
        h, t, meta, cnt = _odd_layer(
            h, mods[1], norm1_g[1][None, :], norm2_g[1][None, :], od_w_in[0], od_b_in[0], od_conv_w[0], od_conv_b[0],
            od_filt_w1[0], od_filt_b1[0], od_filt_freq[0], od_filt_w2[0], od_filt_b2[0], od_filt_w3[0],
            od_filt_b3[0], od_filt_w4[0], od_skip[0], od_w_out[0], od_b_out[0], router_w[1], router_b[1])
        h = _moe(t, meta, cnt[0, :N_EXPERTS], h, mods[1], fg, moe_w_gu, moe_b_gu, moe_w_down, moe_b_down, 1,
                 final=True)
    return h[None]
```

```python
import functools
import math

import numpy as np
import jax
import jax.numpy as jnp
from jax import lax
from jax.experimental import pallas as pl
from jax.experimental.pallas import tpu as pltpu

f32 = jnp.float32
bf16 = jnp.bfloat16
i32 = jnp.int32
HIGHEST = lax.Precision.HIGHEST

EPS = 1e-6
GRID_W = 64
HEAD_DIM = 128
ATTN_HEADS = 4
ATTN_KV_HEADS = 2
ROPE_THETA = 10000.0
ROPE_PAIRS = HEAD_DIM // 4
MLSTM_HEADS = 4
MLSTM_DK = 64
MLSTM_DV = 128
CHUNK = 128
N_EXPERTS = 32
TOP_K = 4
SWIGLU_LIMIT = 7.0
SWIGLU_ALPHA = 1.702
FILTER_EMB = 33
FILTER_BANDS = 16
DECAY_TARGET = 1e-2
FAST_DECAY = 0.3
SLOW_DECAY = 1.5

LANES = 128
VMEM_LIMIT = 56 * 1024 * 1024
MOE_BM = 512
MLSTM_STEP_CHUNKS = 1
EXPERT_CHUNK = 512
ATTN_PROBE = 256
ATTN_SLACK = 64.0


def _cp(sem, vmem=VMEM_LIMIT):
    return pltpu.CompilerParams(dimension_semantics=sem, vmem_limit_bytes=vmem)


def _full(shape):
    n = len(shape)
    return pl.BlockSpec(shape, lambda *a, _n=n: (0,) * _n)


def _rms(x, g):
    return x * lax.rsqrt(jnp.mean(x * x, axis=-1, keepdims=True) + EPS) * g


def _log_sigmoid(x):
    return jnp.minimum(x, 0.0) - jnp.log(1.0 + jnp.exp(-jnp.abs(x)))


def _dot(a, b):
    return jnp.dot(a, b, preferred_element_type=f32)


def _dot_nt(a, b, precision=None):
    return lax.dot_general(a, b, (((1,), (1,)), ((), ())), preferred_element_type=f32, precision=precision)


def _dot_tn(a, b):
    return lax.dot_general(a, b, (((0,), (0,)), ((), ())), preferred_element_type=f32)


def _mods_body(cv_ref, w_ref, b_ref, o_ref):
    cv = cv_ref[...]
    s = cv * jax.nn.sigmoid(cv)
    o_ref[0] = jnp.dot(s, w_ref[0], precision=HIGHEST, preferred_element_type=f32) + b_ref[0]


def _mods(cv, ada_w, ada_b):
    depth, d, nm = ada_w.shape
    tn = 1024
    return pl.pallas_call(
        _mods_body,
        grid=(depth, nm // tn),
        in_specs=[_full((8, d)),
                  pl.BlockSpec((1, d, tn), lambda i, j: (i, 0, j)),
                  pl.BlockSpec((1, 1, tn), lambda i, j: (i, 0, j))],
        out_specs=pl.BlockSpec((1, 8, tn), lambda i, j: (i, 0, j)),
        out_shape=jax.ShapeDtypeStruct((depth, 8, nm), f32),
        compiler_params=_cp(("arbitrary", "arbitrary")),
        name="mods",
    )(cv, ada_w, ada_b.reshape(depth, 1, nm))


def _ev_in_body(x_ref, mod_ref, g1_ref, wm_ref, bm_ref, wkT_ref, bkT_ref, wg_ref, bg_ref, wgT_ref, bgT_ref,
                qg_ref, kg_ref, cos_ref, sin_ref,
                q_ref, k_ref, v_ref, mq_ref, mkT_ref, mv_ref, og_ref, gc_ref, gT_ref, *, mrow, d):
    x = x_ref[...]
    shift = mod_ref[mrow:mrow + 1, 0:d]
    scale = mod_ref[mrow:mrow + 1, d:2 * d]
    a = (_rms(x, g1_ref[...]) * (1.0 + scale) + shift).astype(bf16)
    z = _dot(a, wm_ref[...]) + bm_ref[...]
    cos = cos_ref[...]
    sin = sin_ref[...]
    lane = lax.broadcasted_iota(i32, cos.shape, 1)
    first_half = (lane % 64) < 32

    def qk_head(zh, gain, mult):
        zn = zh * lax.rsqrt(jnp.mean(zh * zh, axis=-1, keepdims=True) + EPS) * gain
        partner = jnp.where(first_half, pltpu.roll(zn, 96, 1), pltpu.roll(zn, 32, 1))
        return ((zn * cos + partner * sin) * mult).astype(bf16)

    for h in range(ATTN_HEADS):
        q_ref[:, h * 128:(h + 1) * 128] = qk_head(z[:, h * 128:(h + 1) * 128], qg_ref[...],
                                                   HEAD_DIM ** -0.5 * math.log2(math.e))
    for h in range(ATTN_KV_HEADS):
        k_ref[:, h * 128:(h + 1) * 128] = qk_head(z[:, 512 + h * 128:512 + (h + 1) * 128], kg_ref[...], 1.0)
    v_ref[...] = z[:, 768:1024].astype(bf16)
    mq_ref[...] = (z[:, 1024:1280] * (MLSTM_DK ** -0.5)).astype(bf16)
    mv_ref[...] = z[:, 1280:1792].astype(bf16)
    og_ref[...] = z[:, 1792:2304]
    mkT_ref[...] = (_dot_nt(wkT_ref[...], a) + bkT_ref[...]).astype(bf16)
    gc_ref[...] = _dot(a, wg_ref[...]) + bg_ref[...]
    gT_ref[...] = _dot_nt(wgT_ref[...], a) + bgT_ref[...]


def _ev_in(x, mod, g1, wts, cos, sin, mrow, tm):
    r, d = x.shape
    wm, bm, wkT, bkT, wg, bg, wgT, bgT, qg, kg = wts
    row = lambda n: pl.BlockSpec((tm, n), lambda i: (i, 0))
    col = lambda n: pl.BlockSpec((n, tm), lambda i: (0, i))
    outs = [((r, 512), bf16, row(512)), ((r, 256), bf16, row(256)), ((r, 256), bf16, row(256)),
            ((r, 256), bf16, row(256)), ((256, r), bf16, col(256)), ((r, 512), bf16, row(512)),
            ((r, 512), f32, row(512)), ((r, LANES), f32, row(LANES)), ((16, r), f32, col(16))]
    return pl.pallas_call(
        functools.partial(_ev_in_body, mrow=mrow, d=d),
        grid=(r // tm,),
        in_specs=[row(d), _full(mod.shape), _full(g1.shape), _full(wm.shape), _full(bm.shape), _full(wkT.shape),
                  _full(bkT.shape), _full(wg.shape), _full(bg.shape), _full(wgT.shape), _full(bgT.shape),
                  _full(qg.shape), _full(kg.shape), row(LANES), row(LANES)],
        out_specs=[o[2] for o in outs],
        out_shape=[jax.ShapeDtypeStruct(o[0], o[1]) for o in outs],
        compiler_params=_cp(("arbitrary",)),
        name="ev_in",
    )(x, mod, g1, wm, bm, wkT, bkT, wg, bg, wgT, bgT, qg, kg, cos, sin)


def _attn_body(q_ref, k_ref, v_ref, o_ref, m_sc, acc_sc):
    j = pl.program_id(1)
    group = ATTN_HEADS // ATTN_KV_HEADS

    def scores(g):
        q = jnp.concatenate([q_ref[:, (g * group + i) * 128:(g * group + i + 1) * 128] for i in range(group)], axis=0)
        return _dot_nt(q, k_ref[:, g * 128:(g + 1) * 128])

    def renew_reference():
        for g in range(ATTN_KV_HEADS):
            s = scores(g)
            m_prev = m_sc[g]
            m_new = jnp.maximum(m_prev, jnp.max(s, axis=-1, keepdims=True))
            p = jnp.exp2(s - m_new).astype(bf16)
            acc_sc[g] = jnp.exp2(m_prev - m_new) * acc_sc[g] + _dot(p, v_ref[:, g * 256:(g + 1) * 256])
            m_sc[g] = m_new

    @pl.when(j == 0)
    def _():
        acc_sc[...] = jnp.zeros(acc_sc.shape, f32)
        for g in range(ATTN_KV_HEADS):
            q = jnp.concatenate([q_ref[:, (g * group + i) * 128:(g * group + i + 1) * 128] for i in range(group)],
                                axis=0)
            m_sc[g] = jnp.max(_dot_nt(q, k_ref[0:ATTN_PROBE, g * 128:(g + 1) * 128]), axis=-1, keepdims=True)

    pvs, zmax = [], None
    for g in range(ATTN_KV_HEADS):
        z = scores(g) - m_sc[g]
        pvs.append(_dot(jnp.exp2(z).astype(bf16), v_ref[:, g * 256:(g + 1) * 256]))
        zm = jnp.max(z)
        zmax = zm if zmax is None else jnp.maximum(zmax, zm)
    in_range = zmax <= ATTN_SLACK

    @pl.when(in_range)
    def _():
        for g in range(ATTN_KV_HEADS):
            acc_sc[g] = acc_sc[g] + pvs[g]

    @pl.when(jnp.logical_not(in_range))
    def _():
        renew_reference()

    @pl.when(j == pl.num_programs(1) - 1)
    def _():
        tq = q_ref.shape[0]
        for g in range(ATTN_KV_HEADS):
            acc = acc_sc[g]
            o = (acc[:, 0:128] / acc[:, 128:256]).astype(bf16)
            for i in range(group):
                o_ref[:, (g * group + i) * 128:(g * group + i + 1) * 128] = o[i * tq:(i + 1) * tq]


def _attention(q, k, v, tq, tk):
    nq, nk = q.shape[0], k.shape[0]
    assert nq % tq == 0 and nk % tk == 0
    rows = tq * (ATTN_HEADS // ATTN_KV_HEADS)
    return pl.pallas_call(
        _attn_body,
        grid=(nq // tq, nk // tk),
        in_specs=[pl.BlockSpec((tq, 512), lambda i, j: (i, 0)),
                  pl.BlockSpec((tk, 256), lambda i, j: (j, 0)),
                  pl.BlockSpec((tk, 512), lambda i, j: (j, 0))],
        out_specs=pl.BlockSpec((tq, 512), lambda i, j: (i, 0)),
        out_shape=jax.ShapeDtypeStruct((nq, 512), bf16),
        scratch_shapes=[pltpu.VMEM((ATTN_KV_HEADS, rows, 1), f32), pltpu.VMEM((ATTN_KV_HEADS, rows, 256), f32)],
        compiler_params=_cp(("arbitrary", "arbitrary")),
        name="attention",
    )(q, k, v)


def _mlstm_body(qf_ref, kTf_ref, vf_ref, gcf_ref, gTf_ref, qb_ref, kTb_ref, vb_ref, gcb_ref, gTb_ref,
                fbr_ref, fbc_ref, c0_ref, m0_ref, hf_ref, hb_ref, c_out_ref, m_out_ref, c_sc, m_sc):
    j = pl.program_id(0)

    @pl.when(j == 0)
    def _():
        c_sc[...] = c0_ref[...]
        m_sc[...] = m0_ref[...]

    ti = lax.broadcasted_iota(i32, (CHUNK, CHUNK), 0)
    si = lax.broadcasted_iota(i32, (CHUNK, CHUNK), 1)
    ones_c = jnp.ones((CHUNK, 128), f32)
    for d in range(2):
        q_ref, kT_ref, v_ref, gc_ref, gT_ref, h_ref = ((qf_ref, kTf_ref, vf_ref, gcf_ref, gTf_ref, hf_ref) if d == 0 else
                                                      (qb_ref, kTb_ref, vb_ref, gcb_ref, gTb_ref, hb_ref))
        mask = (si <= ti) if d == 0 else (si >= ti)
        lmat = mask.astype(f32)
        umat = ((ti <= si) if d == 0 else (ti >= si)).astype(f32)
        last = CHUNK - 1 if d == 0 else 0
        nsub = gc_ref.shape[0] // CHUNK
        for u in (range(nsub) if d == 0 else range(nsub - 1, -1, -1)):
            rows = slice(u * CHUNK, (u + 1) * CHUNK)
            gc = gc_ref[rows, :]
            gT = gT_ref[:, rows]
            fcol = _log_sigmoid(gc + fbr_ref[...])
            frow = _log_sigmoid(gT + fbc_ref[...])
            bcol = jnp.dot(lmat, fcol, precision=HIGHEST, preferred_element_type=f32)
            brow = jnp.dot(frow, umat, precision=HIGHEST, preferred_element_type=f32)
            for h in range(MLSTM_HEADS):
                idx = d * MLSTM_HEADS + h
                ji, jf = idx, 8 + idx
                b_t = bcol[:, jf:jf + 1]
                logd = jnp.where(mask, b_t - brow[jf:jf + 1, :] + gT[ji:ji + 1, :], -jnp.inf)
                m_old = m_sc[idx][:, 0:1]
                m_row = jnp.maximum(b_t + m_old, jnp.max(logd, axis=-1, keepdims=True))
                w_inter = jnp.exp(b_t + m_old - m_row)
                dm = jnp.exp(logd - m_row)
                qh = q_ref[rows, h * MLSTM_DK:(h + 1) * MLSTM_DK]
                kTh = kT_ref[h * MLSTM_DK:(h + 1) * MLSTM_DK, rows]
                vaug = jnp.concatenate([v_ref[rows, h * 128:(h + 1) * 128].astype(f32), ones_c], axis=1)
                s = _dot(qh, kTh) * dm
                intra = _dot(s.astype(bf16), vaug.astype(bf16))
                c_old = c_sc[idx]
                inter = _dot(qh, c_old.astype(bf16))
                num = intra[:, 0:128] + w_inter * inter[:, 0:128]
                den = intra[:, 128:129] + w_inter * inter[:, 128:129]
                h_ref[rows, h * 128:(h + 1) * 128] = num / jnp.maximum(jnp.abs(den), jnp.exp(-m_row))
                b_last = b_t[last:last + 1, :]
                lw = b_last - brow[jf:jf + 1, :] + gT[ji:ji + 1, :]
                m_new = jnp.maximum(b_last + m_old, jnp.max(lw, axis=-1, keepdims=True))
                kw = (kTh.astype(f32) * jnp.exp(lw - m_new)).astype(bf16)
                decay = jnp.exp(b_last + m_old - m_new)
                c_sc[idx] = decay * c_old + _dot(kw, vaug.astype(bf16))
                m_sc[idx] = jnp.broadcast_to(m_new, (1, 128))

    @pl.when(j == pl.num_programs(0) - 1)
    def _():
        c_out_ref[...] = c_sc[...]
        m_out_ref[...] = m_sc[...]


def _mlstm(mq, mkT, mv, gc, gT, fbr, fbc, c0, m0):
    n = mq.shape[0]
    tb = MLSTM_STEP_CHUNKS * CHUNK
    assert n % tb == 0
    nc = n // tb
    fwd = lambda j: j
    bwd = lambda j: nc - 1 - j

    def specs(ix):
        return [pl.BlockSpec((tb, 256), lambda j: (ix(j), 0)),
                pl.BlockSpec((256, tb), lambda j: (0, ix(j))),
                pl.BlockSpec((tb, 512), lambda j: (ix(j), 0)),
                pl.BlockSpec((tb, LANES), lambda j: (ix(j), 0)),
                pl.BlockSpec((16, tb), lambda j: (0, ix(j)))]

    nh = 2 * MLSTM_HEADS
    return pl.pallas_call(
        _mlstm_body,
        grid=(nc,),
        in_specs=specs(fwd) + specs(bwd) + [_full(fbr.shape), _full(fbc.shape), _full(c0.shape), _full(m0.shape)],
        out_specs=[pl.BlockSpec((tb, 512), lambda j: (fwd(j), 0)),
                   pl.BlockSpec((tb, 512), lambda j: (bwd(j), 0)),
                   _full(c0.shape), _full(m0.shape)],
        out_shape=[jax.ShapeDtypeStruct((n, 512), f32), jax.ShapeDtypeStruct((n, 512), f32),
                   jax.ShapeDtypeStruct(c0.shape, f32), jax.ShapeDtypeStruct(m0.shape, f32)],
        scratch_shapes=[pltpu.VMEM((nh, MLSTM_DK, 256), f32), pltpu.VMEM((nh, 1, 128), f32)],
        compiler_params=_cp(("arbitrary",)),
        name="mlstm",
    )(mq, mkT, mv, gc, gT, mq, mkT, mv, gc, gT, fbr, fbc, c0, m0)


def _post_block(x, y, mod_ref, g2_ref, rw_ref, rb_ref, run_sc, h_ref, t_ref, meta_ref, cnt_ref, d):
    tm = x.shape[0]
    i = pl.program_id(0)

    @pl.when(i == 0)
    def _():
        run_sc[...] = jnp.zeros(run_sc.shape, f32)

    h1 = x + mod_ref[0:1, 2 * d:3 * d] * y
    h_ref[...] = h1
    t = _rms(h1, g2_ref[...]) * (1.0 + mod_ref[0:1, 4 * d:5 * d]) + mod_ref[0:1, 3 * d:4 * d]
    for j in range(d // LANES):
        t_ref[pl.ds(j, tm, stride=d // LANES), :] = t[:, j * LANES:(j + 1) * LANES]
    t_hi = t.astype(bf16)
    t_lo = (t - t_hi.astype(f32)).astype(bf16)
    logits = _dot(t_hi, rw_ref[0]) + (_dot(t_hi, rw_ref[1]) + _dot(t_lo, rw_ref[0])) + rb_ref[...]
    lane = lax.broadcasted_iota(i32, (tm, LANES), 1)
    lanef = lane.astype(f32)
    neg = jnp.float32(-jnp.inf)
    cur = jnp.where(lane < N_EXPERTS, logits, neg)
    tops, ids, hots = [], [], []
    for _ in range(TOP_K):
        mx = jnp.max(cur, axis=-1, keepdims=True)
        idx = jnp.min(jnp.where(cur == mx, lanef, 1e9), axis=-1, keepdims=True)
        hot = lanef == idx
        cur = jnp.where(hot, neg, cur)
        tops.append(mx)
        ids.append(idx)
        hots.append(hot)
    es = [jnp.exp(tk - tops[0]) for tk in tops]
    den = es[0] + es[1] + es[2] + es[3]
    cnt = jnp.zeros((tm, LANES), f32)
    for hot in hots:
        cnt = cnt + hot.astype(f32)
    ri = lax.broadcasted_iota(i32, (tm, tm), 0)
    ci = lax.broadcasted_iota(i32, (tm, tm), 1)
    strict = (ci < ri).astype(bf16)
    before = _dot(strict, cnt.astype(bf16)) + run_sc[...]
    meta = jnp.zeros((tm, LANES), f32)
    for k in range(TOP_K):
        rank = jnp.sum(jnp.where(hots[k], before, 0.0), axis=-1, keepdims=True)
        meta = jnp.where(lane == k, ids[k], meta)
        meta = jnp.where(lane == TOP_K + k, rank, meta)
        meta = jnp.where(lane == 2 * TOP_K + k, es[k] / den, meta)
    meta_ref[...] = meta
    run_sc[...] = run_sc[...] + jnp.sum(cnt, axis=0, keepdims=True)

    @pl.when(i == pl.num_programs(0) - 1)
    def _():
        cnt_ref[...] = jnp.broadcast_to(run_sc[...], cnt_ref.shape).astype(i32)


def _post_specs(n, d, tm):
    row = lambda w: pl.BlockSpec((tm, w), lambda i: (i, 0))
    out_specs = [row(d), pl.BlockSpec((tm * (d // LANES), LANES), lambda i: (i, 0)), row(LANES), _full((8, LANES))]
    out_shape = [jax.ShapeDtypeStruct((n, d), f32), jax.ShapeDtypeStruct((n * (d // LANES), LANES), f32),
                 jax.ShapeDtypeStruct((n, LANES), f32), jax.ShapeDtypeStruct((8, LANES), i32)]
    return out_specs, out_shape


def _ev_out_body(att_ref, hf_ref, hb_ref, og_ref, x_ref, mod_ref, hg_ref, w_ref, b_ref, g2_ref, rw_ref, rb_ref,
                 h_ref, t_ref, meta_ref, cnt_ref, run_sc, *, d):
    hs = hf_ref[...] + hb_ref[...]
    og = og_ref[...]
    parts = [att_ref[...]]
    for h in range(MLSTM_HEADS):
        sl = slice(h * 128, (h + 1) * 128)
        hh = hs[:, sl]
        hn = hh * lax.rsqrt(jnp.mean(hh * hh, axis=-1, keepdims=True) + EPS) * hg_ref[:, sl]
        parts.append((jax.nn.sigmoid(og[:, sl]) * hn).astype(bf16))
    y = _dot(jnp.concatenate(parts, axis=1), w_ref[...]) + b_ref[...]
    _post_block(x_ref[...], y, mod_ref, g2_ref, rw_ref, rb_ref, run_sc, h_ref, t_ref, meta_ref, cnt_ref, d)


def _ev_out(att, hf, hb, og, x, mod, hg, w, b, g2, rw, rb, tm):
    n, d = x.shape
    row = lambda wd: pl.BlockSpec((tm, wd), lambda i: (i, 0))
    out_specs, out_shape = _post_specs(n, d, tm)
    return pl.pallas_call(
        functools.partial(_ev_out_body, d=d),
        grid=(n // tm,),
        in_specs=[row(512), row(512), row(512), row(512), row(d), _full(mod.shape), _full(hg.shape), _full(w.shape),
                  _full(b.shape), _full(g2.shape), _full(rw.shape), _full(rb.shape)],
        out_specs=out_specs, out_shape=out_shape,
        scratch_shapes=[pltpu.VMEM((1, LANES), f32)],
        compiler_params=_cp(("arbitrary",)),
        name="ev_out",
    )(att, hf, hb, og, x, mod, hg, w, b, g2, rw, rb)


ROW_UNROLL = 8


TOKEN_ROWS = 8


def _dispatch_body(dest_ref, t_ref, buf_ref, sem):
    tm = t_ref.shape[0] // TOKEN_ROWS

    def issue(rb, c):
        base = pl.multiple_of(rb * (ROW_UNROLL * TOKEN_ROWS), ROW_UNROLL * TOKEN_ROWS)
        for u in range(ROW_UNROLL):
            src = t_ref.at[pl.ds(base + u * TOKEN_ROWS, TOKEN_ROWS)]
            for k in range(TOP_K):
                dest = pl.multiple_of(dest_ref[(rb * ROW_UNROLL + u) * TOP_K + k], TOKEN_ROWS)
                pltpu.make_async_copy(src, buf_ref.at[pl.ds(dest, TOKEN_ROWS)], sem).start(priority=k % 2)
        return c

    lax.fori_loop(0, tm // ROW_UNROLL, issue, 0)
    for k in range(TOP_K):
        pltpu.make_async_copy(t_ref, buf_ref.at[pl.ds(0, tm * TOKEN_ROWS)], sem).wait()


def _dispatch(dest, t, tm):
    rows, l = t.shape
    return pl.pallas_call(
        _dispatch_body,
        grid=(rows // (tm * TOKEN_ROWS),),
        in_specs=[pl.BlockSpec((tm * TOP_K,), lambda i: (i,), memory_space=pltpu.SMEM),
                  pl.BlockSpec((tm * TOKEN_ROWS, l), lambda i: (i, 0))],
        out_specs=pl.BlockSpec(memory_space=pl.ANY),
        out_shape=jax.ShapeDtypeStruct((rows * TOP_K, l), f32),
        scratch_shapes=[pltpu.SemaphoreType.DMA(())],
        compiler_params=_cp(("arbitrary",)),
        name="moe_dispatch",
    )(dest, t)


def _expert_body(blk_ref, exp_ref, lo_ref, act_ref, chg_ref, nxt_ref, x_ref, wgu_hbm, bgu_ref, wd_hbm, bd_ref, o_ref,
                 wgu_f32, wd_f32, wgu_sc, wd_sc, sem, *, de, layer):
    i = pl.program_id(0)

    def weight_copies(e):
        return (pltpu.make_async_copy(wgu_hbm.at[layer, e], wgu_f32, sem.at[0]),
                pltpu.make_async_copy(wd_hbm.at[layer, e], wd_f32, sem.at[1]))

    @pl.when(i == 0)
    def _():
        for cp in weight_copies(exp_ref[0]):
            cp.start()

    @pl.when(chg_ref[i] == 1)
    def _():
        for cp in weight_copies(exp_ref[i]):
            cp.wait()
        wgu_sc[...] = wgu_f32[...].astype(bf16)
        wd_sc[...] = wd_f32[...].astype(bf16)

        @pl.when(nxt_ref[i] >= 0)
        def _():
            for cp in weight_copies(nxt_ref[i]):
                cp.start()

    ns = TOKEN_ROWS
    lo = lo_ref[i]

    def run(m, row0):
        base = row0 * ns
        x = jnp.concatenate([x_ref[pl.ds(base + j, m, stride=ns), :] for j in range(ns)], axis=1).astype(bf16)
        y = bd_ref[0, 0]
        for c0 in range(0, de, EXPERT_CHUNK):
            cols = slice(c0, c0 + EXPERT_CHUNK)
            ucols = slice(de + c0, de + c0 + EXPERT_CHUNK)
            g = jnp.minimum(_dot(x, wgu_sc[:, cols]) + bgu_ref[0, 0][:, cols], SWIGLU_LIMIT)
            u = jnp.clip(_dot(x, wgu_sc[:, ucols]) + bgu_ref[0, 0][:, ucols], -SWIGLU_LIMIT, SWIGLU_LIMIT)
            a = g * jax.nn.sigmoid(SWIGLU_ALPHA * g) * (u + 1.0)
            y = y + _dot(a.astype(bf16), wd_sc[cols, :])

        @pl.when(lo == 0)
        def _():
            for j in range(ns):
                o_ref[pl.ds(j, m, stride=ns), :] = y[:, j * LANES:(j + 1) * LANES]
            if m < MOE_BM:
                o_ref[m * ns:MOE_BM * ns, :] = jnp.zeros(((MOE_BM - m) * ns, LANES), f32)

        @pl.when(lo > 0)
        def _():
            keep = (lax.broadcasted_iota(i32, (m, LANES), 0) + row0) >= lo
            for j in range(ns):
                rows = pl.ds(base + j, m, stride=ns)
                o_ref[rows, :] = jnp.where(keep, y[:, j * LANES:(j + 1) * LANES], o_ref[rows, :])

    @pl.when(act_ref[i] == 1)
    def _():
        run(MOE_BM, 0)

    @pl.when(act_ref[i] >= 2)
    def _():
        run(MOE_BM // 2, pl.multiple_of((act_ref[i] - 2) * (MOE_BM // 2), MOE_BM // 2))


def _experts(sched, buf, wgu, bgu, wd, bd, layer):
    nl, ne, d, de2 = wgu.shape
    assert d == TOKEN_ROWS * LANES
    de = de2 // 2
    n_items = sched[0].shape[0]
    wmap = lambda i, b, e, l, a, c, n: (layer, e[i], 0, 0)
    rows = pl.BlockSpec((MOE_BM * TOKEN_ROWS, LANES), lambda i, b, e, l, a, c, n: (b[i], 0))
    return pl.pallas_call(
        functools.partial(_expert_body, de=de, layer=layer),
        grid_spec=pltpu.PrefetchScalarGridSpec(
            num_scalar_prefetch=6,
            grid=(n_items,),
            in_specs=[rows,
                      pl.BlockSpec(memory_space=pl.ANY),
                      pl.BlockSpec((1, 1, 1, de2), wmap),
                      pl.BlockSpec(memory_space=pl.ANY),
                      pl.BlockSpec((1, 1, 1, d), wmap)],
            out_specs=rows,
            scratch_shapes=[pltpu.VMEM((d, de2), f32), pltpu.VMEM((de, d), f32),
                            pltpu.VMEM((d, de2), bf16), pltpu.VMEM((de, d), bf16),
                            pltpu.SemaphoreType.DMA((2,))]),
        out_shape=jax.ShapeDtypeStruct(buf.shape, f32),
        compiler_params=_cp(("arbitrary",)),
        name="moe_experts",
    )(*sched, buf, wgu, bgu.reshape(nl, ne, 1, de2), wd, bd.reshape(nl, ne, 1, d))


def _combine_body(dest_ref, eo_ref, meta_ref, h_ref, mod_ref, fg_ref, o_ref, stage, sem, *, d, final):
    tm = h_ref.shape[0]
    ns = TOKEN_ROWS

    def issue(rb, c):
        base = pl.multiple_of(rb * (ROW_UNROLL * ns), ROW_UNROLL * ns)
        for u in range(ROW_UNROLL):
            for k in range(TOP_K):
                src = pl.multiple_of(dest_ref[(rb * ROW_UNROLL + u) * TOP_K + k], ns)
                pltpu.make_async_copy(eo_ref.at[pl.ds(src, ns)], stage.at[pl.ds(k * tm * ns + base + u * ns, ns)],
                                      sem).start(priority=k % 2)
        return c

    lax.fori_loop(0, tm // ROW_UNROLL, issue, 0)
    for k in range(TOP_K):
        pltpu.make_async_copy(eo_ref.at[pl.ds(0, tm * ns)], stage.at[pl.ds(k * tm * ns, tm * ns)], sem).wait()
    meta = meta_ref[...]
    f = None
    for k in range(TOP_K):
        rows = jnp.concatenate([stage[pl.ds(k * tm * ns + j, tm, stride=ns), :] for j in range(ns)], axis=1)
        term = meta[:, 2 * TOP_K + k:2 * TOP_K + k + 1] * rows
        f = term if f is None else f + term
    h = h_ref[...] + mod_ref[0:1, 5 * d:6 * d] * f
    if final:
        h = _rms(h, fg_ref[...])
    o_ref[...] = h


def _combine(dest, eo, meta, h, mod, fg, tm, final):
    n, d = h.shape
    return pl.pallas_call(
        functools.partial(_combine_body, d=d, final=final),
        grid=(n // tm,),
        in_specs=[pl.BlockSpec((tm * TOP_K,), lambda i: (i,), memory_space=pltpu.SMEM),
                  pl.BlockSpec(memory_space=pl.ANY),
                  pl.BlockSpec((tm, LANES), lambda i: (i, 0)),
                  pl.BlockSpec((tm, d), lambda i: (i, 0)),
                  _full(mod.shape), _full(fg.shape)],
        out_specs=pl.BlockSpec((tm, d), lambda i: (i, 0)),
        out_shape=jax.ShapeDtypeStruct((n, d), f32),
        scratch_shapes=[pltpu.VMEM((TOP_K * tm * TOKEN_ROWS, LANES), f32), pltpu.SemaphoreType.DMA(())],
        compiler_params=_cp(("arbitrary",)),
        name="moe_combine",
    )(dest, eo, meta, h, mod, fg)


def _moe(t, meta, counts, h, mod, fg, wgu, bgu, wd, bd, layer, final):
    n, d = h.shape
    m = n * TOP_K
    nblk = m // MOE_BM
    n_items = nblk + N_EXPERTS - 1
    counts = counts.astype(i32)
    ends = jnp.cumsum(counts)
    starts = ends - counts
    first_blk = starts // MOE_BM
    last_blk = jnp.maximum(ends - 1, starts) // MOE_BM
    per = jnp.where(counts > 0, last_blk - first_blk + 1, 0)
    cum = jnp.cumsum(per)
    off = cum - per
    total = cum[-1]
    it = jnp.minimum(jnp.arange(n_items, dtype=i32), jnp.maximum(total - 1, 0))
    e_it = jnp.minimum(jnp.sum((cum[None, :] <= it[:, None]).astype(i32), axis=1), N_EXPERTS - 1)
    eids = jnp.arange(N_EXPERTS, dtype=i32)[None, :]
    pick = lambda v: jnp.sum(jnp.where(e_it[:, None] == eids, v[None, :], 0), axis=1)
    blk_it = pick(first_blk) + it - pick(off)
    lo_it = jnp.maximum(pick(starts) - blk_it * MOE_BM, 0)
    hi_it = jnp.minimum(pick(ends) - blk_it * MOE_BM, MOE_BM)
    half = MOE_BM // 2
    mode = jnp.where(hi_it <= half, 2, jnp.where(lo_it >= half, 3, 1))
    active = jnp.where(jnp.arange(n_items, dtype=i32) < total, mode, 0).astype(i32)
    changed = jnp.concatenate([jnp.ones((1,), i32), (e_it[1:] != e_it[:-1]).astype(i32)])
    later = (eids > eids.T) & (counts[None, :] > 0)
    nxt_e = jnp.min(jnp.where(later, eids, N_EXPERTS), axis=1)
    nxt_it = pick(jnp.where(nxt_e < N_EXPERTS, nxt_e, -1))
    sched = (blk_it.astype(i32), e_it.astype(i32), lo_it.astype(i32), active, changed, nxt_it.astype(i32))
    ids = meta[:, 0:TOP_K].astype(i32)
    ranks = meta[:, TOP_K:2 * TOP_K].astype(i32)
    dest = (jnp.sum(jnp.where(ids[:, :, None] == eids[None], starts[None, None, :], 0), axis=-1) + ranks).reshape(-1)
    dest = dest * TOKEN_ROWS
    buf = _dispatch(dest, t, 256)
    eo = _experts(sched, buf, wgu, bgu, wd, bd, layer)
    return _combine(dest, eo, meta, h, mod, fg, 256, final)


def _rope_tables(n):
    rows = n // GRID_W
    row = np.repeat(np.arange(rows), GRID_W)
    col = np.tile(np.arange(GRID_W), rows)
    pos = np.stack([row, col], axis=-1).astype(np.float64)
    inv_freq = ROPE_THETA ** (-np.arange(ROPE_PAIRS, dtype=np.float64) / ROPE_PAIRS)
    ang = pos[:, :, None] * inv_freq
    c, s = np.cos(ang), np.sin(ang)
    cos = np.concatenate([c[:, 0], c[:, 0], c[:, 1], c[:, 1]], axis=-1).astype(np.float32)
    sin = np.concatenate([-s[:, 0], s[:, 0], -s[:, 1], s[:, 1]], axis=-1).astype(np.float32)
    return jnp.asarray(cos), jnp.asarray(sin)


def _router_params(router_w, router_b):
    d = router_w.shape[0]
    rw = jnp.zeros((d, LANES), f32).at[:, :N_EXPERTS].set(router_w)
    rb = jnp.zeros((1, LANES), f32).at[0, :N_EXPERTS].set(router_b)
    rw_hi = rw.astype(bf16)
    rw_lo = (rw - rw_hi.astype(f32)).astype(bf16)
    return jnp.stack([rw_hi, rw_lo]), rb


def _even_layer(x, ctx, mod, g1, g2, w_in, b_in, q_gain, k_gain, f_bias, h_gain, w_out, b_out, router_w, router_b):
    n, d = x.shape
    nctx = ctx.shape[0]
    cols = lambda a, b: w_in[:, a:b]
    wm = jnp.concatenate([cols(0, 1280), cols(1536, 2048), cols(2064, 2576)], axis=1).astype(bf16)
    bm = jnp.concatenate([b_in[0:1280], b_in[1536:2048], b_in[2064:2576]])[None, :]
    wkT = cols(1280, 1536).T.astype(bf16)
    bkT = b_in[1280:1536][:, None]
    wgate = cols(2048, 2064)
    wg = jnp.zeros((d, LANES), f32).at[:, :16].set(wgate).astype(bf16)
    bg = jnp.zeros((1, LANES), f32).at[0, :16].set(b_in[2048:2064])
    wgT = wgate.T.astype(bf16)
    bgT = b_in[2048:2064][:, None]
    wts = (wm, bm, wkT, bkT, wg, bg, wgT, bgT, q_gain[None, :], k_gain[None, :])
    cos, sin = _rope_tables(n)
    lat = _ev_in(x, mod, g1, wts, cos, sin, 0, min(512, n))
    cx = _ev_in(ctx, mod, g1, wts, jnp.ones((nctx, LANES), f32), jnp.zeros((nctx, LANES), f32), 1, nctx)
    q, k, v, mq, mkT, mv, og, gc, gT = lat
    _, kc, vc, mqc, mkTc, mvc, _, gcc, gTc = cx

    v_all = jnp.concatenate([v, vc], axis=0)
    ones = jnp.ones((n + nctx, 128), bf16)
    v_aug = jnp.concatenate([v_all[:, 0:128], ones, v_all[:, 128:256], ones], axis=1)
    att = _attention(q, jnp.concatenate([k, kc], axis=0), v_aug, 1024 if n % 1024 == 0 else min(512, n), 1280)

    fb = f_bias.reshape(-1)
    fbr = jnp.zeros((1, LANES), f32).at[0, 8:16].set(fb)
    fbc = jnp.zeros((16, LANES), f32).at[8:16, :].set(jnp.broadcast_to(fb[:, None], (8, LANES)))
    nh = 2 * MLSTM_HEADS
    c0 = jnp.zeros((nh, MLSTM_DK, 256), f32)
    m0 = jnp.zeros((nh, 1, 128), f32)
    _, _, c1, m1 = _mlstm(mqc, mkTc, mvc, gcc, gTc, fbr, fbc, c0, m0)
    hf, hb, _, _ = _mlstm(mq, mkT, mv, gc, gT, fbr, fbc, c1, m1)

    rw, rb = _router_params(router_w, router_b)
    return _ev_out(att, hf, hb, og, x, mod, h_gain[None, :], w_out.astype(bf16), b_out[None, :], g2, rw, rb,
                   min(512, n))


FFT_N2 = 128
CH_TILE = 8


def _ct_shape(ch, nblocks):
    return jax.ShapeDtypeStruct((ch // CH_TILE, nblocks * CH_TILE, FFT_N2), f32)


def _ct_block(ch, per):
    return pl.BlockSpec((ch // CH_TILE, per * CH_TILE, FFT_N2), lambda i: (0, i, 0))


def _ct_slab(nblocks):
    return pl.BlockSpec((1, nblocks * CH_TILE, FFT_N2), lambda i: (i, 0, 0))


def _ct_channel(ref, c, nblocks):
    return ref.at[0], pl.ds(c, nblocks, stride=CH_TILE)


def _hy_in_body(x_ref, mod_ref, g1_ref, wT_ref, b_ref, zT_ref, *, d):
    a = (_rms(x_ref[...], g1_ref[...]) * (1.0 + mod_ref[0:1, d:2 * d]) + mod_ref[0:1, 0:d]).astype(bf16)
    zT_ref[...] = _dot_nt(wT_ref[...], a) + b_ref[...]


def _hy_in(x, mod, g1, wT, b, tm):
    n, d = x.shape
    c3 = wT.shape[0]
    return pl.pallas_call(
        functools.partial(_hy_in_body, d=d),
        grid=(n // tm,),
        in_specs=[pl.BlockSpec((tm, d), lambda i: (i, 0)), _full(mod.shape), _full(g1.shape), _full(wT.shape),
                  _full(b.shape)],
        out_specs=pl.BlockSpec((c3, tm), lambda i: (0, i)),
        out_shape=jax.ShapeDtypeStruct((c3, n), f32),
        compiler_params=_cp(("arbitrary",)),
        name="hy_in",
    )(x, mod, g1, wT, b)


def _hy_conv_body(z_ref, zp_ref, zn_ref, cw_ref, vp_ref, x0_ref, *, ch):
    i = pl.program_id(0)
    tt = z_ref.shape[1]
    has_prev = (i > 0).astype(f32)
    has_next = (i < pl.num_programs(0) - 1).astype(f32)
    lane = lax.broadcasted_iota(i32, (ch, tt), 1)
    first, last = lane == 0, lane == tt - 1

    def conv(g):
        rows = slice(g * ch, (g + 1) * ch)
        z = z_ref[rows, :]
        pcol = zp_ref[rows, FFT_N2 - 1:FFT_N2] * has_prev
        ncol = zn_ref[rows, 0:1] * has_next
        zm1 = jnp.where(first, pcol, pltpu.roll(z, 1, 1))
        zp1 = jnp.where(last, ncol, pltpu.roll(z, tt - 1, 1))
        cw = cw_ref[rows, :]
        coef = lambda k: jnp.concatenate([cw[:, k * LANES:(k + 1) * LANES]] * (tt // LANES), axis=1)
        return coef(0) * zm1 + coef(1) * z + coef(2) * zp1 + coef(3)

    x0 = conv(0)
    vp = conv(2) * conv(1)
    for j in range(tt // FFT_N2):
        rows = slice(j * CH_TILE, (j + 1) * CH_TILE)
        x0_ref[:, rows, :] = x0[:, j * FFT_N2:(j + 1) * FFT_N2].reshape(ch // CH_TILE, CH_TILE, FFT_N2)
        vp_ref[:, rows, :] = vp[:, j * FFT_N2:(j + 1) * FFT_N2].reshape(ch // CH_TILE, CH_TILE, FFT_N2)


def _hy_conv(zT, cw, tt):
    c3, n = zT.shape
    ch = c3 // 3
    nb = n // FFT_N2
    per = tt // FFT_N2
    return pl.pallas_call(
        functools.partial(_hy_conv_body, ch=ch),
        grid=(n // tt,),
        in_specs=[pl.BlockSpec((c3, tt), lambda i: (0, i)),
                  pl.BlockSpec((c3, FFT_N2), lambda i: (0, jnp.maximum(i * per - 1, 0))),
                  pl.BlockSpec((c3, FFT_N2), lambda i: (0, jnp.minimum(i * per + per, nb - 1))),
                  _full(cw.shape)],
        out_specs=[_ct_block(ch, per), _ct_block(ch, per)],
        out_shape=[_ct_shape(ch, nb), _ct_shape(ch, nb)],
        compiler_params=_cp(("arbitrary",)),
        name="hy_conv",
    )(zT, zT, zT, cw)


def _hy_filt_body(zz_ref, w1_ref, b1_ref, fr_ref, w2_ref, b2_ref, w3_ref, b3_ref, w4T_ref, dl_ref, tr_ref, mr_ref,
                  o_ref):
    hdot = lambda a, b: jnp.dot(a, b, precision=HIGHEST, preferred_element_type=f32)
    fr = fr_ref[...]
    h = jnp.sin(fr * (hdot(w1_ref[...], zz_ref[...]) + b1_ref[...]))
    h = jnp.sin(fr * (hdot(w2_ref[...], h) + b2_ref[...]))
    h = jnp.sin(fr * (hdot(w3_ref[...], h) + b3_ref[...]))
    taps = _dot(w4T_ref[0], h.astype(bf16))
    ch = taps.shape[0]
    dl = dl_ref[...]
    for j in range(tr_ref.shape[0]):
        blk = taps[:, j * FFT_N2:(j + 1) * FFT_N2] * jnp.exp(-dl * tr_ref[j]) * mr_ref[j]
        o_ref[:, j * CH_TILE:(j + 1) * CH_TILE, :] = blk.reshape(ch // CH_TILE, CH_TILE, FFT_N2)


def _hy_filt(zz, w1, b1, fr, w2, b2, w3, b3, w4T, deltas, trow, mrow):
    n2l = zz.shape[1]
    nb2 = n2l // FFT_N2
    ch = w4T.shape[1]
    per = 4 if nb2 % 8 == 0 else 1
    steps = nb2 // per
    return pl.pallas_call(
        _hy_filt_body,
        grid=(steps,),
        in_specs=[pl.BlockSpec((LANES, per * FFT_N2), lambda i: (0, i)), _full(w1.shape), _full(b1.shape),
                  _full(fr.shape), _full(w2.shape), _full(b2.shape), _full(w3.shape), _full(b3.shape),
                  pl.BlockSpec((1,) + w4T.shape[1:], lambda i: ((i >= steps // 2).astype(i32), 0, 0)),
                  _full(deltas.shape),
                  pl.BlockSpec((per, 1, FFT_N2), lambda i: (i, 0, 0)),
                  pl.BlockSpec((per, 1, FFT_N2), lambda i: (i, 0, 0))],
        out_specs=_ct_block(ch, per),
        out_shape=_ct_shape(ch, nb2),
        compiler_params=_cp(("arbitrary",)),
        name="hy_filt",
    )(zz, w1, b1, fr, w2, b2, w3, b3, w4T, deltas, trow, mrow)


def _dft_stage12(x_ref, f1, tr_ref, ti_ref, r2_ref):
    ct = CH_TILE
    n1 = f1.shape[0] // 2
    n_in = f1.shape[1]
    xs = []
    for c in range(ct):
        slab, rows = _ct_channel(x_ref, c, n_in)
        xs.append(slab[rows, :])
    x = jnp.concatenate(xs, axis=1).astype(bf16)
    a = _dot(f1, x)
    tr, ti = tr_ref[...], ti_ref[...]
    lhs = []
    for c in range(ct):
        ar = a[0:n1, c * FFT_N2:(c + 1) * FFT_N2]
        ai = a[n1:2 * n1, c * FFT_N2:(c + 1) * FFT_N2]
        lhs.append(jnp.concatenate([ar * tr - ai * ti, ar * ti + ai * tr], axis=1).astype(bf16))
    xf = _dot(jnp.concatenate(lhs, axis=0), r2_ref[...])
    return [(xf[c * n1:(c + 1) * n1, 0:FFT_N2], xf[c * n1:(c + 1) * n1, FFT_N2:2 * FFT_N2]) for c in range(ct)]


def _hy_fftconv_body(v_ref, t_ref, f1_ref, tr_ref, ti_ref, r2_ref, r2i_ref, i2_ref, y_ref):
    ct = CH_TILE
    n1 = f1_ref.shape[0] // 2
    n_out = i2_ref.shape[0]
    tr, ti = tr_ref[...], ti_ref[...]
    f1 = f1_ref[...]
    g = _dft_stage12(t_ref, f1, tr_ref, ti_ref, r2_ref)
    x = _dft_stage12(v_ref, f1[:, 0:n_out], tr_ref, ti_ref, r2_ref)
    lhs = []
    for (xr, xi), (gr, gi) in zip(x, g):
        lhs.append(jnp.concatenate([xr * gr - xi * gi, xr * gi + xi * gr], axis=1).astype(bf16))
    b = _dot(jnp.concatenate(lhs, axis=0), r2i_ref[...])
    brs, bis = [], []
    for c in range(ct):
        br = b[c * n1:(c + 1) * n1, 0:FFT_N2]
        bi = b[c * n1:(c + 1) * n1, FFT_N2:2 * FFT_N2]
        brs.append(br * tr + bi * ti)
        bis.append(bi * tr - br * ti)
    rhs = jnp.concatenate([jnp.concatenate(brs, axis=1), jnp.concatenate(bis, axis=1)], axis=0).astype(bf16)
    y = _dot(i2_ref[...], rhs)
    for c in range(ct):
        slab, rows = _ct_channel(y_ref, c, n_out)
        slab[rows, :] = y[:, c * FFT_N2:(c + 1) * FFT_N2]


def _hy_fftconv(vp3, taps3, f1, tr, ti, r2, r2i, i2):
    vblk = _ct_slab(i2.shape[0])
    tblk = _ct_slab(f1.shape[1])
    return pl.pallas_call(
        _hy_fftconv_body,
        grid=(vp3.shape[0],),
        in_specs=[vblk, tblk, _full(f1.shape), _full(tr.shape), _full(ti.shape), _full(r2.shape),
                  _full(r2i.shape), _full(i2.shape)],
        out_specs=vblk,
        out_shape=jax.ShapeDtypeStruct(vp3.shape, f32),
        compiler_params=_cp(("arbitrary",)),
        name="hy_fftconv",
    )(vp3, taps3, f1, tr, ti, r2, r2i, i2)


def _hy_out_body(y_ref, vp_ref, x0_ref, sk_ref, w_ref, b_ref, x_ref, mod_ref, g2_ref, rw_ref, rb_ref,
                 h_ref, t_ref, meta_ref, cnt_ref, run_sc, *, d):
    sk = sk_ref[...]
    ch = sk.shape[0]

    def block(ref, j):
        return ref[:, j * CH_TILE:(j + 1) * CH_TILE, :].reshape(ch, FFT_N2)

    g = jnp.concatenate([((block(y_ref, j) + block(vp_ref, j) * sk) * block(x0_ref, j)).astype(bf16)
                         for j in range(y_ref.shape[1] // CH_TILE)], axis=1)
    y = _dot_tn(g, w_ref[...]) + b_ref[...]
    _post_block(x_ref[...], y, mod_ref, g2_ref, rw_ref, rb_ref, run_sc, h_ref, t_ref, meta_ref, cnt_ref, d)


def _hy_out(y3, vp3, x03, skip, w, b, x, mod, g2, rw, rb, tm):
    n, d = x.shape
    blk = _ct_block(skip.shape[0], tm // FFT_N2)
    out_specs, out_shape = _post_specs(n, d, tm)
    return pl.pallas_call(
        functools.partial(_hy_out_body, d=d),
        grid=(n // tm,),
        in_specs=[blk, blk, blk, _full(skip.shape), _full(w.shape), _full(b.shape),
                  pl.BlockSpec((tm, d), lambda i: (i, 0)), _full(mod.shape), _full(g2.shape), _full(rw.shape),
                  _full(rb.shape)],
        out_specs=out_specs, out_shape=out_shape,
        scratch_shapes=[pltpu.VMEM((1, LANES), f32)],
        compiler_params=_cp(("arbitrary",)),
        name="hy_out",
    )(y3, vp3, x03, skip, w, b, x, mod, g2, rw, rb)


def _dft_constants(n):
    nn = 2 * n
    n1 = nn // FFT_N2
    k1 = np.arange(n1)[:, None]
    phi = 2.0 * np.pi * (k1 * np.arange(n1)[None, :] % n1) / n1
    f1 = np.concatenate([np.cos(phi), -np.sin(phi)], axis=0)
    th = 2.0 * np.pi * (k1 * np.arange(FFT_N2)[None, :]) / nn
    tr, ti = np.cos(th), -np.sin(th)
    ps = 2.0 * np.pi * (np.arange(FFT_N2)[:, None] * np.arange(FFT_N2)[None, :] % FFT_N2) / FFT_N2
    c2, s2 = np.cos(ps), np.sin(ps)
    r2 = np.block([[c2, -s2], [s2, c2]])
    r2i = np.block([[c2, s2], [-s2, c2]])
    phi_i = phi[: n1 // 2, :]
    i2 = np.concatenate([np.cos(phi_i), -np.sin(phi_i)], axis=1) / nn
    as32 = lambda a: jnp.asarray(a.astype(np.float32))
    return dict(f1=as32(f1).astype(bf16), tr=as32(tr), ti=as32(ti),
                r2=as32(r2).astype(bf16), r2i=as32(r2i).astype(bf16), i2=as32(i2).astype(bf16))


def _filter_tables(n):
    t = np.linspace(0.0, 1.0, n)[:, None]
    w = (2.0 * np.pi / n) * np.arange(n)[:, None]
    f = np.linspace(1e-4, FILTER_BANDS - 1, FILTER_BANDS)
    z = np.concatenate([t, np.cos(f * w), -np.sin(f * w)], axis=-1)
    pos2 = np.arange(2 * n)
    pos = np.where(pos2 < n, pos2, np.minimum(2 * n - pos2, n - 1))
    zz = np.zeros((LANES, 2 * n), np.float32)
    zz[:FILTER_EMB, :] = z[pos].T
    trow = t[pos, 0].astype(np.float32).reshape(2 * n // FFT_N2, 1, FFT_N2)
    mrow = (pos2 != n).astype(np.float32).reshape(2 * n // FFT_N2, 1, FFT_N2)
    return jnp.asarray(zz), jnp.asarray(trow), jnp.asarray(mrow)


def _odd_layer(x, mod, g1, g2, w_in, b_in, conv_w, conv_b, fw1, fb1, ffreq, fw2, fb2, fw3, fb3, fw4, skip, w_out,
               b_out, router_w, router_b):
    n, d = x.shape
    ch = w_out.shape[0]
    tm = min(512, n)
    zT = _hy_in(x, mod, g1, w_in.T.astype(bf16), b_in[:, None], tm)
    cw = jnp.repeat(jnp.concatenate([conv_w.T, conv_b[:, None]], axis=1), LANES, axis=1)
    vp3, x03 = _hy_conv(zT, cw, tm)

    zz, trow, mrow = _filter_tables(n)
    w1T = jnp.zeros((fw1.shape[1], LANES), f32).at[:, :FILTER_EMB].set(fw1.T)
    w4T = jnp.stack([fw4[:, :ch].T, fw4[:, ch:].T]).astype(bf16)
    deltas = jnp.abs(jnp.linspace(math.log(DECAY_TARGET) / SLOW_DECAY, math.log(DECAY_TARGET) / FAST_DECAY, ch,
                                  dtype=f32))[:, None]
    taps3 = _hy_filt(zz, w1T, fb1[:, None], ffreq[:, None], fw2.T, fb2[:, None], fw3.T, fb3[:, None], w4T, deltas,
                     trow, mrow)
    k = _dft_constants(n)
    y3 = _hy_fftconv(vp3, taps3, k["f1"], k["tr"], k["ti"], k["r2"], k["r2i"], k["i2"])
    rw, rb = _router_params(router_w, router_b)
    return _hy_out(y3, vp3, x03, skip[:, None], w_out.astype(bf16), b_out[None, :], x, mod, g2, rw, rb, tm)


def kernel(x, c, ctx, c_ctx, ada_w, ada_b, norm1_g, norm2_g, ev_w_in, ev_b_in, ev_q_gain, ev_k_gain, ev_f_bias,
           ev_h_gain, ev_w_out, ev_b_out, od_w_in, od_b_in, od_conv_w, od_conv_b, od_filt_w1, od_filt_b1,
           od_filt_freq, od_filt_w2, od_filt_b2, od_filt_w3, od_filt_b3, od_filt_w4, od_skip, od_w_out, od_b_out,
           router_w, router_b, moe_w_gu, moe_b_gu, moe_w_down, moe_b_down, final_g):
    bsz, n, d = x.shape
    assert bsz == 1 and c.shape[0] == 1
    depth = ada_w.shape[0]
    assert depth in (1, 2)
    cv = jnp.zeros((8, d), f32).at[0].set(c[0]).at[1].set(c_ctx)
    mods = _mods(cv, ada_w, ada_b)
    fg = final_g[None, :]

    h, t, meta, cnt = _even_layer(
        x[0], ctx[0], mods[0], norm1_g[0][None, :], norm2_g[0][None, :], ev_w_in[0], ev_b_in[0], ev_q_gain[0],
        ev_k_gain[0], ev_f_bias[0], ev_h_gain[0], ev_w_out[0], ev_b_out[0], router_w[0], router_b[0])
    h = _moe(t, meta, cnt[0, :N_EXPERTS], h, mods[0], fg, moe_w_gu, moe_b_gu, moe_w_down, moe_b_down, 0,
             final=(depth == 1))
    if depth == 2:
        h, t, meta, cnt = _odd_layer(
            h, mods[1], norm1_g[1][None, :], norm2_g[1][None, :], od_w_in[0], od_b_in[0], od_conv_w[0], od_conv_b[0],
            od_filt_w1[0], od_filt_b1[0], od_filt_freq[0], od_filt_w2[0], od_filt_b2[0], od_filt_w3[0],
            od_filt_b3[0], od_filt_w4[0], od_skip[0], od_w_out[0], od_b_out[0], router_w[1], router_b[1])
        h = _moe(t, meta, cnt[0, :N_EXPERTS], h, mods[1], fg, moe_w_gu, moe_b_gu, moe_w_down, moe_b_down, 1,
                 final=True)
    return h[None]
```

```python
import functools
import math

import numpy as np
import jax
import jax.numpy as jnp
from jax import lax
from jax.experimental import pallas as pl
from jax.experimental.pallas import tpu as pltpu

f32 = jnp.float32
bf16 = jnp.bfloat16
i32 = jnp.int32
HIGHEST = lax.Precision.HIGHEST

EPS = 1e-6
GRID_W = 64
HEAD_DIM = 128
ATTN_HEADS = 4
ATTN_KV_HEADS = 2
ROPE_THETA = 10000.0
ROPE_PAIRS = HEAD_DIM // 4
MLSTM_HEADS = 4
MLSTM_DK = 64
MLSTM_DV = 128
CHUNK = 128
N_EXPERTS = 32
TOP_K = 4
SWIGLU_LIMIT = 7.0
SWIGLU_ALPHA = 1.702
FILTER_EMB = 33
FILTER_BANDS = 16
DECAY_TARGET = 1e-2
FAST_DECAY = 0.3
SLOW_DECAY = 1.5

LANES = 128
VMEM_LIMIT = 56 * 1024 * 1024
MOE_BM = 512
MLSTM_STEP_CHUNKS = 1
EXPERT_CHUNK = 1024
ATTN_PROBE = 256
ATTN_SLACK = 64.0


def _cp(sem, vmem=VMEM_LIMIT):
    return pltpu.CompilerParams(dimension_semantics=sem, vmem_limit_bytes=vmem)


def _full(shape):
    n = len(shape)
    return pl.BlockSpec(shape, lambda *a, _n=n: (0,) * _n)


def _rms(x, g):
    return x * lax.rsqrt(jnp.mean(x * x, axis=-1, keepdims=True) + EPS) * g


def _log_sigmoid(x):
    return jnp.minimum(x, 0.0) - jnp.log(1.0 + jnp.exp(-jnp.abs(x)))


def _dot(a, b):
    return jnp.dot(a, b, preferred_element_type=f32)


def _dot_nt(a, b, precision=None):
    return lax.dot_general(a, b, (((1,), (1,)), ((), ())), preferred_element_type=f32, precision=precision)


def _dot_tn(a, b):
    return lax.dot_general(a, b, (((0,), (0,)), ((), ())), preferred_element_type=f32)


def _mods_body(cv_ref, w_ref, b_ref, o_ref):
    cv = cv_ref[...]
    s = cv * jax.nn.sigmoid(cv)
    o_ref[0] = jnp.dot(s, w_ref[0], precision=HIGHEST, preferred_element_type=f32) + b_ref[0]


def _mods(cv, ada_w, ada_b):
    depth, d, nm = ada_w.shape
    tn = 1024
    return pl.pallas_call(
        _mods_body,
        grid=(depth, nm // tn),
        in_specs=[_full((8, d)),
                  pl.BlockSpec((1, d, tn), lambda i, j: (i, 0, j)),
                  pl.BlockSpec((1, 1, tn), lambda i, j: (i, 0, j))],
        out_specs=pl.BlockSpec((1, 8, tn), lambda i, j: (i, 0, j)),
        out_shape=jax.ShapeDtypeStruct((depth, 8, nm), f32),
        compiler_params=_cp(("arbitrary", "arbitrary")),
        name="mods",
    )(cv, ada_w, ada_b.reshape(depth, 1, nm))


def _ev_in_body(x_ref, mod_ref, g1_ref, wm_ref, bm_ref, wkT_ref, bkT_ref, wg_ref, bg_ref, wgT_ref, bgT_ref,
                qg_ref, kg_ref, cos_ref, sin_ref,
                q_ref, k_ref, v_ref, mq_ref, mkT_ref, mv_ref, og_ref, gc_ref, gT_ref, *, mrow, d):
    x = x_ref[...]
    shift = mod_ref[mrow:mrow + 1, 0:d]
    scale = mod_ref[mrow:mrow + 1, d:2 * d]
    a = (_rms(x, g1_ref[...]) * (1.0 + scale) + shift).astype(bf16)
    z = _dot(a, wm_ref[...]) + bm_ref[...]
    cos = cos_ref[...]
    sin = sin_ref[...]
    lane = lax.broadcasted_iota(i32, cos.shape, 1)
    first_half = (lane % 64) < 32

    def qk_head(zh, gain, mult):
        zn = zh * lax.rsqrt(jnp.mean(zh * zh, axis=-1, keepdims=True) + EPS) * gain
        partner = jnp.where(first_half, pltpu.roll(zn, 96, 1), pltpu.roll(zn, 32, 1))
        return ((zn * cos + partner * sin) * mult).astype(bf16)

    for h in range(ATTN_HEADS):
        q_ref[:, h * 128:(h + 1) * 128] = qk_head(z[:, h * 128:(h + 1) * 128], qg_ref[...],
                                                   HEAD_DIM ** -0.5 * math.log2(math.e))
    for h in range(ATTN_KV_HEADS):
        k_ref[:, h * 128:(h + 1) * 128] = qk_head(z[:, 512 + h * 128:512 + (h + 1) * 128], kg_ref[...], 1.0)
    v_ref[...] = z[:, 768:1024].astype(bf16)
    mq_ref[...] = (z[:, 1024:1280] * (MLSTM_DK ** -0.5)).astype(bf16)
    mv_ref[...] = z[:, 1280:1792].astype(bf16)
    og_ref[...] = z[:, 1792:2304]
    mkT_ref[...] = (_dot_nt(wkT_ref[...], a) + bkT_ref[...]).astype(bf16)
    gc_ref[...] = _dot(a, wg_ref[...]) + bg_ref[...]
    gT_ref[...] = _dot_nt(wgT_ref[...], a) + bgT_ref[...]


def _ev_in(x, mod, g1, wts, cos, sin, mrow, tm):
    r, d = x.shape
    wm, bm, wkT, bkT, wg, bg, wgT, bgT, qg, kg = wts
    row = lambda n: pl.BlockSpec((tm, n), lambda i: (i, 0))
    col = lambda n: pl.BlockSpec((n, tm), lambda i: (0, i))
    outs = [((r, 512), bf16, row(512)), ((r, 256), bf16, row(256)), ((r, 256), bf16, row(256)),
            ((r, 256), bf16, row(256)), ((256, r), bf16, col(256)), ((r, 512), bf16, row(512)),
            ((r, 512), f32, row(512)), ((r, LANES), f32, row(LANES)), ((16, r), f32, col(16))]
    return pl.pallas_call(
        functools.partial(_ev_in_body, mrow=mrow, d=d),
        grid=(r // tm,),
        in_specs=[row(d), _full(mod.shape), _full(g1.shape), _full(wm.shape), _full(bm.shape), _full(wkT.shape),
                  _full(bkT.shape), _full(wg.shape), _full(bg.shape), _full(wgT.shape), _full(bgT.shape),
                  _full(qg.shape), _full(kg.shape), row(LANES), row(LANES)],
        out_specs=[o[2] for o in outs],
        out_shape=[jax.ShapeDtypeStruct(o[0], o[1]) for o in outs],
        compiler_params=_cp(("arbitrary",)),
        name="ev_in",
    )(x, mod, g1, wm, bm, wkT, bkT, wg, bg, wgT, bgT, qg, kg, cos, sin)


def _attn_body(q_ref, k_ref, v_ref, o_ref, m_sc, acc_sc):
    j = pl.program_id(1)
    group = ATTN_HEADS // ATTN_KV_HEADS

    def scores(g):
        q = jnp.concatenate([q_ref[:, (g * group + i) * 128:(g * group + i + 1) * 128] for i in range(group)], axis=0)
        return _dot_nt(q, k_ref[:, g * 128:(g + 1) * 128])

    def renew_reference():
        for g in range(ATTN_KV_HEADS):
            s = scores(g)
            m_prev = m_sc[g]
            m_new = jnp.maximum(m_prev, jnp.max(s, axis=-1, keepdims=True))
            p = jnp.exp2(s - m_new).astype(bf16)
            acc_sc[g] = jnp.exp2(m_prev - m_new) * acc_sc[g] + _dot(p, v_ref[:, g * 256:(g + 1) * 256])
            m_sc[g] = m_new

    @pl.when(j == 0)
    def _():
        acc_sc[...] = jnp.zeros(acc_sc.shape, f32)
        for g in range(ATTN_KV_HEADS):
            q = jnp.concatenate([q_ref[:, (g * group + i) * 128:(g * group + i + 1) * 128] for i in range(group)],
                                axis=0)
            m_sc[g] = jnp.max(_dot_nt(q, k_ref[0:ATTN_PROBE, g * 128:(g + 1) * 128]), axis=-1, keepdims=True)

    pvs, zmax = [], None
    for g in range(ATTN_KV_HEADS):
        z = scores(g) - m_sc[g]
        pvs.append(_dot(jnp.exp2(z).astype(bf16), v_ref[:, g * 256:(g + 1) * 256]))
        zm = jnp.max(z)
        zmax = zm if zmax is None else jnp.maximum(zmax, zm)
    in_range = zmax <= ATTN_SLACK

    @pl.when(in_range)
    def _():
        for g in range(ATTN_KV_HEADS):
            acc_sc[g] = acc_sc[g] + pvs[g]

    @pl.when(jnp.logical_not(in_range))
    def _():
        renew_reference()

    @pl.when(j == pl.num_programs(1) - 1)
    def _():
        tq = q_ref.shape[0]
        for g in range(ATTN_KV_HEADS):
            acc = acc_sc[g]
            o = (acc[:, 0:128] / acc[:, 128:256]).astype(bf16)
            for i in range(group):
                o_ref[:, (g * group + i) * 128:(g * group + i + 1) * 128] = o[i * tq:(i + 1) * tq]


def _attention(q, k, v, tq, tk):
    nq, nk = q.shape[0], k.shape[0]
    assert nq % tq == 0 and nk % tk == 0
    rows = tq * (ATTN_HEADS // ATTN_KV_HEADS)
    return pl.pallas_call(
        _attn_body,
        grid=(nq // tq, nk // tk),
        in_specs=[pl.BlockSpec((tq, 512), lambda i, j: (i, 0)),
                  pl.BlockSpec((tk, 256), lambda i, j: (j, 0)),
                  pl.BlockSpec((tk, 512), lambda i, j: (j, 0))],
        out_specs=pl.BlockSpec((tq, 512), lambda i, j: (i, 0)),
        out_shape=jax.ShapeDtypeStruct((nq, 512), bf16),
        scratch_shapes=[pltpu.VMEM((ATTN_KV_HEADS, rows, 1), f32), pltpu.VMEM((ATTN_KV_HEADS, rows, 256), f32)],
        compiler_params=_cp(("arbitrary", "arbitrary")),
        name="attention",
    )(q, k, v)


def _mlstm_body(qf_ref, kTf_ref, vf_ref, gcf_ref, gTf_ref, qb_ref, kTb_ref, vb_ref, gcb_ref, gTb_ref,
                fbr_ref, fbc_ref, c0_ref, m0_ref, hf_ref, hb_ref, c_out_ref, m_out_ref, c_sc, m_sc):
    j = pl.program_id(0)

    @pl.when(j == 0)
    def _():
        c_sc[...] = c0_ref[...]
        m_sc[...] = m0_ref[...]

    ti = lax.broadcasted_iota(i32, (CHUNK, CHUNK), 0)
    si = lax.broadcasted_iota(i32, (CHUNK, CHUNK), 1)
    ones_c = jnp.ones((CHUNK, 128), f32)
    for d in range(2):
        q_ref, kT_ref, v_ref, gc_ref, gT_ref, h_ref = ((qf_ref, kTf_ref, vf_ref, gcf_ref, gTf_ref, hf_ref) if d == 0 else
                                                      (qb_ref, kTb_ref, vb_ref, gcb_ref, gTb_ref, hb_ref))
        mask = (si <= ti) if d == 0 else (si >= ti)
        lmat = mask.astype(f32)
        umat = ((ti <= si) if d == 0 else (ti >= si)).astype(f32)
        last = CHUNK - 1 if d == 0 else 0
        nsub = gc_ref.shape[0] // CHUNK
        for u in (range(nsub) if d == 0 else range(nsub - 1, -1, -1)):
            rows = slice(u * CHUNK, (u + 1) * CHUNK)
            gc = gc_ref[rows, :]
            gT = gT_ref[:, rows]
            fcol = _log_sigmoid(gc + fbr_ref[...])
            frow = _log_sigmoid(gT + fbc_ref[...])
            bcol = jnp.dot(lmat, fcol, precision=HIGHEST, preferred_element_type=f32)
            brow = jnp.dot(frow, umat, precision=HIGHEST, preferred_element_type=f32)
            for h in range(MLSTM_HEADS):
                idx = d * MLSTM_HEADS + h
                ji, jf = idx, 8 + idx
                b_t = bcol[:, jf:jf + 1]
                logd = jnp.where(mask, b_t - brow[jf:jf + 1, :] + gT[ji:ji + 1, :], -jnp.inf)
                m_old = m_sc[idx][:, 0:1]
                m_row = jnp.maximum(b_t + m_old, jnp.max(logd, axis=-1, keepdims=True))
                w_inter = jnp.exp(b_t + m_old - m_row)
                dm = jnp.exp(logd - m_row)
                qh = q_ref[rows, h * MLSTM_DK:(h + 1) * MLSTM_DK]
                kTh = kT_ref[h * MLSTM_DK:(h + 1) * MLSTM_DK, rows]
                vaug = jnp.concatenate([v_ref[rows, h * 128:(h + 1) * 128].astype(f32), ones_c], axis=1)
                s = _dot(qh, kTh) * dm
                intra = _dot(s.astype(bf16), vaug.astype(bf16))
                c_old = c_sc[idx]
                inter = _dot(qh, c_old.astype(bf16))
                num = intra[:, 0:128] + w_inter * inter[:, 0:128]
                den = intra[:, 128:129] + w_inter * inter[:, 128:129]
                h_ref[rows, h * 128:(h + 1) * 128] = num / jnp.maximum(jnp.abs(den), jnp.exp(-m_row))
                b_last = b_t[last:last + 1, :]
                lw = b_last - brow[jf:jf + 1, :] + gT[ji:ji + 1, :]
                m_new = jnp.maximum(b_last + m_old, jnp.max(lw, axis=-1, keepdims=True))
                kw = (kTh.astype(f32) * jnp.exp(lw - m_new)).astype(bf16)
                decay = jnp.exp(b_last + m_old - m_new)
                c_sc[idx] = decay * c_old + _dot(kw, vaug.astype(bf16))
                m_sc[idx] = jnp.broadcast_to(m_new, (1, 128))

    @pl.when(j == pl.num_programs(0) - 1)
    def _():
        c_out_ref[...] = c_sc[...]
        m_out_ref[...] = m_sc[...]


def _mlstm(mq, mkT, mv, gc, gT, fbr, fbc, c0, m0):
    n = mq.shape[0]
    tb = MLSTM_STEP_CHUNKS * CHUNK
    assert n % tb == 0
    nc = n // tb
    fwd = lambda j: j
    bwd = lambda j: nc - 1 - j

    def specs(ix):
        return [pl.BlockSpec((tb, 256), lambda j: (ix(j), 0)),
                pl.BlockSpec((256, tb), lambda j: (0, ix(j))),
                pl.BlockSpec((tb, 512), lambda j: (ix(j), 0)),
                pl.BlockSpec((tb, LANES), lambda j: (ix(j), 0)),
                pl.BlockSpec((16, tb), lambda j: (0, ix(j)))]

    nh = 2 * MLSTM_HEADS
    return pl.pallas_call(
        _mlstm_body,
        grid=(nc,),
        in_specs=specs(fwd) + specs(bwd) + [_full(fbr.shape), _full(fbc.shape), _full(c0.shape), _full(m0.shape)],
        out_specs=[pl.BlockSpec((tb, 512), lambda j: (fwd(j), 0)),
                   pl.BlockSpec((tb, 512), lambda j: (bwd(j), 0)),
                   _full(c0.shape), _full(m0.shape)],
        out_shape=[jax.ShapeDtypeStruct((n, 512), f32), jax.ShapeDtypeStruct((n, 512), f32),
                   jax.ShapeDtypeStruct(c0.shape, f32), jax.ShapeDtypeStruct(m0.shape, f32)],
        scratch_shapes=[pltpu.VMEM((nh, MLSTM_DK, 256), f32), pltpu.VMEM((nh, 1, 128), f32)],
        compiler_params=_cp(("arbitrary",)),
        name="mlstm",
    )(mq, mkT, mv, gc, gT, mq, mkT, mv, gc, gT, fbr, fbc, c0, m0)


def _post_block(x, y, mod_ref, g2_ref, rw_ref, rb_ref, run_sc, h_ref, t_ref, meta_ref, cnt_ref, d):
    tm = x.shape[0]
    i = pl.program_id(0)

    @pl.when(i == 0)
    def _():
        run_sc[...] = jnp.zeros(run_sc.shape, f32)

    h1 = x + mod_ref[0:1, 2 * d:3 * d] * y
    h_ref[...] = h1
    t = _rms(h1, g2_ref[...]) * (1.0 + mod_ref[0:1, 4 * d:5 * d]) + mod_ref[0:1, 3 * d:4 * d]
    for j in range(d // LANES):
        t_ref[pl.ds(j, tm, stride=d // LANES), :] = t[:, j * LANES:(j + 1) * LANES]
    t_hi = t.astype(bf16)
    t_lo = (t - t_hi.astype(f32)).astype(bf16)
    logits = _dot(t_hi, rw_ref[0]) + (_dot(t_hi, rw_ref[1]) + _dot(t_lo, rw_ref[0])) + rb_ref[...]
    lane = lax.broadcasted_iota(i32, (tm, LANES), 1)
    lanef = lane.astype(f32)
    neg = jnp.float32(-jnp.inf)
    cur = jnp.where(lane < N_EXPERTS, logits, neg)
    tops, ids, hots = [], [], []
    for _ in range(TOP_K):
        mx = jnp.max(cur, axis=-1, keepdims=True)
        idx = jnp.min(jnp.where(cur == mx, lanef, 1e9), axis=-1, keepdims=True)
        hot = lanef == idx
        cur = jnp.where(hot, neg, cur)
        tops.append(mx)
        ids.append(idx)
        hots.append(hot)
    es = [jnp.exp(tk - tops[0]) for tk in tops]
    den = es[0] + es[1] + es[2] + es[3]
    cnt = jnp.zeros((tm, LANES), f32)
    for hot in hots:
        cnt = cnt + hot.astype(f32)
    ri = lax.broadcasted_iota(i32, (tm, tm), 0)
    ci = lax.broadcasted_iota(i32, (tm, tm), 1)
    strict = (ci < ri).astype(bf16)
    before = _dot(strict, cnt.astype(bf16)) + run_sc[...]
    meta = jnp.zeros((tm, LANES), f32)
    for k in range(TOP_K):
        rank = jnp.sum(jnp.where(hots[k], before, 0.0), axis=-1, keepdims=True)
        meta = jnp.where(lane == k, ids[k], meta)
        meta = jnp.where(lane == TOP_K + k, rank, meta)
        meta = jnp.where(lane == 2 * TOP_K + k, es[k] / den, meta)
    meta_ref[...] = meta
    run_sc[...] = run_sc[...] + jnp.sum(cnt, axis=0, keepdims=True)

    @pl.when(i == pl.num_programs(0) - 1)
    def _():
        cnt_ref[...] = jnp.broadcast_to(run_sc[...], cnt_ref.shape).astype(i32)


def _post_specs(n, d, tm):
    row = lambda w: pl.BlockSpec((tm, w), lambda i: (i, 0))
    out_specs = [row(d), pl.BlockSpec((tm * (d // LANES), LANES), lambda i: (i, 0)), row(LANES), _full((8, LANES))]
    out_shape = [jax.ShapeDtypeStruct((n, d), f32), jax.ShapeDtypeStruct((n * (d // LANES), LANES), f32),
                 jax.ShapeDtypeStruct((n, LANES), f32), jax.ShapeDtypeStruct((8, LANES), i32)]
    return out_specs, out_shape


def _ev_out_body(att_ref, hf_ref, hb_ref, og_ref, x_ref, mod_ref, hg_ref, w_ref, b_ref, g2_ref, rw_ref, rb_ref,
                 h_ref, t_ref, meta_ref, cnt_ref, run_sc, *, d):
    hs = hf_ref[...] + hb_ref[...]
    og = og_ref[...]
    parts = [att_ref[...]]
    for h in range(MLSTM_HEADS):
        sl = slice(h * 128, (h + 1) * 128)
        hh = hs[:, sl]
        hn = hh * lax.rsqrt(jnp.mean(hh * hh, axis=-1, keepdims=True) + EPS) * hg_ref[:, sl]
        parts.append((jax.nn.sigmoid(og[:, sl]) * hn).astype(bf16))
    y = _dot(jnp.concatenate(parts, axis=1), w_ref[...]) + b_ref[...]
    _post_block(x_ref[...], y, mod_ref, g2_ref, rw_ref, rb_ref, run_sc, h_ref, t_ref, meta_ref, cnt_ref, d)


def _ev_out(att, hf, hb, og, x, mod, hg, w, b, g2, rw, rb, tm):
    n, d = x.shape
    row = lambda wd: pl.BlockSpec((tm, wd), lambda i: (i, 0))
    out_specs, out_shape = _post_specs(n, d, tm)
    return pl.pallas_call(
        functools.partial(_ev_out_body, d=d),
        grid=(n // tm,),
        in_specs=[row(512), row(512), row(512), row(512), row(d), _full(mod.shape), _full(hg.shape), _full(w.shape),
                  _full(b.shape), _full(g2.shape), _full(rw.shape), _full(rb.shape)],
        out_specs=out_specs, out_shape=out_shape,
        scratch_shapes=[pltpu.VMEM((1, LANES), f32)],
        compiler_params=_cp(("arbitrary",)),
        name="ev_out",
    )(att, hf, hb, og, x, mod, hg, w, b, g2, rw, rb)


ROW_UNROLL = 8
MOE_MOVE_TOKENS = 512


TOKEN_ROWS = 8


def _dispatch_body(dest_ref, t_ref, buf_ref, sem):
    tm = t_ref.shape[0] // TOKEN_ROWS

    def issue(rb, c):
        base = pl.multiple_of(rb * (ROW_UNROLL * TOKEN_ROWS), ROW_UNROLL * TOKEN_ROWS)
        for u in range(ROW_UNROLL):
            src = t_ref.at[pl.ds(base + u * TOKEN_ROWS, TOKEN_ROWS)]
            for k in range(TOP_K):
                dest = pl.multiple_of(dest_ref[(rb * ROW_UNROLL + u) * TOP_K + k], TOKEN_ROWS)
                pltpu.make_async_copy(src, buf_ref.at[pl.ds(dest, TOKEN_ROWS)], sem).start(priority=k % 2)
        return c

    lax.fori_loop(0, tm // ROW_UNROLL, issue, 0)
    for k in range(TOP_K):
        pltpu.make_async_copy(t_ref, buf_ref.at[pl.ds(0, tm * TOKEN_ROWS)], sem).wait()


def _dispatch(dest, t, tm):
    rows, l = t.shape
    return pl.pallas_call(
        _dispatch_body,
        grid=(rows // (tm * TOKEN_ROWS),),
        in_specs=[pl.BlockSpec((tm * TOP_K,), lambda i: (i,), memory_space=pltpu.SMEM),
                  pl.BlockSpec((tm * TOKEN_ROWS, l), lambda i: (i, 0))],
        out_specs=pl.BlockSpec(memory_space=pl.ANY),
        out_shape=jax.ShapeDtypeStruct((rows * TOP_K, l), f32),
        scratch_shapes=[pltpu.SemaphoreType.DMA(())],
        compiler_params=_cp(("arbitrary",)),
        name="moe_dispatch",
    )(dest, t)


def _expert_body(blk_ref, exp_ref, lo_ref, act_ref, chg_ref, nxt_ref, x_ref, wgu_hbm, bgu_ref, wd_hbm, bd_ref, o_ref,
                 wgu_f32, wd_f32, wgu_sc, wd_sc, sem, *, de, layer):
    i = pl.program_id(0)

    def weight_copies(e):
        return (pltpu.make_async_copy(wgu_hbm.at[layer, e], wgu_f32, sem.at[0]),
                pltpu.make_async_copy(wd_hbm.at[layer, e], wd_f32, sem.at[1]))

    @pl.when(i == 0)
    def _():
        for cp in weight_copies(exp_ref[0]):
            cp.start()

    @pl.when(chg_ref[i] == 1)
    def _():
        for cp in weight_copies(exp_ref[i]):
            cp.wait()
        wgu_sc[...] = wgu_f32[...].astype(bf16)
        wd_sc[...] = wd_f32[...].astype(bf16)

        @pl.when(nxt_ref[i] >= 0)
        def _():
            for cp in weight_copies(nxt_ref[i]):
                cp.start()

    ns = TOKEN_ROWS
    lo = lo_ref[i]

    def run(m, row0):
        base = row0 * ns
        x = jnp.concatenate([x_ref[pl.ds(base + j, m, stride=ns), :] for j in range(ns)], axis=1).astype(bf16)
        y = bd_ref[0, 0]
        for c0 in range(0, de, EXPERT_CHUNK):
            cols = slice(c0, c0 + EXPERT_CHUNK)
            ucols = slice(de + c0, de + c0 + EXPERT_CHUNK)
            g = jnp.minimum(_dot(x, wgu_sc[:, cols]) + bgu_ref[0, 0][:, cols], SWIGLU_LIMIT)
            u = jnp.clip(_dot(x, wgu_sc[:, ucols]) + bgu_ref[0, 0][:, ucols], -SWIGLU_LIMIT, SWIGLU_LIMIT)
            a = g * jax.nn.sigmoid(SWIGLU_ALPHA * g) * (u + 1.0)
            y = y + _dot(a.astype(bf16), wd_sc[cols, :])

        @pl.when(lo == 0)
        def _():
            for j in range(ns):
                o_ref[pl.ds(j, m, stride=ns), :] = y[:, j * LANES:(j + 1) * LANES]
            if m < MOE_BM:
                o_ref[m * ns:MOE_BM * ns, :] = jnp.zeros(((MOE_BM - m) * ns, LANES), f32)

        @pl.when(lo > 0)
        def _():
            keep = (lax.broadcasted_iota(i32, (m, LANES), 0) + row0) >= lo
            for j in range(ns):
                rows = pl.ds(base + j, m, stride=ns)
                o_ref[rows, :] = jnp.where(keep, y[:, j * LANES:(j + 1) * LANES], o_ref[rows, :])

    @pl.when(act_ref[i] == 1)
    def _():
        run(MOE_BM, 0)

    @pl.when(act_ref[i] >= 2)
    def _():
        run(MOE_BM // 2, pl.multiple_of((act_ref[i] - 2) * (MOE_BM // 2), MOE_BM // 2))


def _experts(sched, buf, wgu, bgu, wd, bd, layer):
    nl, ne, d, de2 = wgu.shape
    assert d == TOKEN_ROWS * LANES
    de = de2 // 2
    n_items = sched[0].shape[0]
    wmap = lambda i, b, e, l, a, c, n: (layer, e[i], 0, 0)
    rows = pl.BlockSpec((MOE_BM * TOKEN_ROWS, LANES), lambda i, b, e, l, a, c, n: (b[i], 0))
    return pl.pallas_call(
        functools.partial(_expert_body, de=de, layer=layer),
        grid_spec=pltpu.PrefetchScalarGridSpec(
            num_scalar_prefetch=6,
            grid=(n_items,),
            in_specs=[rows,
                      pl.BlockSpec(memory_space=pl.ANY),
                      pl.BlockSpec((1, 1, 1, de2), wmap),
                      pl.BlockSpec(memory_space=pl.ANY),
                      pl.BlockSpec((1, 1, 1, d), wmap)],
            out_specs=rows,
            scratch_shapes=[pltpu.VMEM((d, de2), f32), pltpu.VMEM((de, d), f32),
                            pltpu.VMEM((d, de2), bf16), pltpu.VMEM((de, d), bf16),
                            pltpu.SemaphoreType.DMA((2,))]),
        out_shape=jax.ShapeDtypeStruct(buf.shape, f32),
        compiler_params=_cp(("arbitrary",)),
        name="moe_experts",
    )(*sched, buf, wgu, bgu.reshape(nl, ne, 1, de2), wd, bd.reshape(nl, ne, 1, d))


def _combine_body(dest_ref, eo_ref, meta_ref, h_ref, mod_ref, fg_ref, o_ref, stage, sem, *, d, final):
    tm = h_ref.shape[0]
    ns = TOKEN_ROWS

    def issue(rb, c):
        base = pl.multiple_of(rb * (ROW_UNROLL * ns), ROW_UNROLL * ns)
        for u in range(ROW_UNROLL):
            for k in range(TOP_K):
                src = pl.multiple_of(dest_ref[(rb * ROW_UNROLL + u) * TOP_K + k], ns)
                pltpu.make_async_copy(eo_ref.at[pl.ds(src, ns)], stage.at[pl.ds(k * tm * ns + base + u * ns, ns)],
                                      sem).start(priority=k % 2)
        return c

    lax.fori_loop(0, tm // ROW_UNROLL, issue, 0)
    for k in range(TOP_K):
        pltpu.make_async_copy(eo_ref.at[pl.ds(0, tm * ns)], stage.at[pl.ds(k * tm * ns, tm * ns)], sem).wait()
    meta = meta_ref[...]
    f = None
    for k in range(TOP_K):
        rows = jnp.concatenate([stage[pl.ds(k * tm * ns + j, tm, stride=ns), :] for j in range(ns)], axis=1)
        term = meta[:, 2 * TOP_K + k:2 * TOP_K + k + 1] * rows
        f = term if f is None else f + term
    h = h_ref[...] + mod_ref[0:1, 5 * d:6 * d] * f
    if final:
        h = _rms(h, fg_ref[...])
    o_ref[...] = h


def _combine(dest, eo, meta, h, mod, fg, tm, final):
    n, d = h.shape
    return pl.pallas_call(
        functools.partial(_combine_body, d=d, final=final),
        grid=(n // tm,),
        in_specs=[pl.BlockSpec((tm * TOP_K,), lambda i: (i,), memory_space=pltpu.SMEM),
                  pl.BlockSpec(memory_space=pl.ANY),
                  pl.BlockSpec((tm, LANES), lambda i: (i, 0)),
                  pl.BlockSpec((tm, d), lambda i: (i, 0)),
                  _full(mod.shape), _full(fg.shape)],
        out_specs=pl.BlockSpec((tm, d), lambda i: (i, 0)),
        out_shape=jax.ShapeDtypeStruct((n, d), f32),
        scratch_shapes=[pltpu.VMEM((TOP_K * tm * TOKEN_ROWS, LANES), f32), pltpu.SemaphoreType.DMA(())],
        compiler_params=_cp(("arbitrary",)),
        name="moe_combine",
    )(dest, eo, meta, h, mod, fg)


def _moe(t, meta, counts, h, mod, fg, wgu, bgu, wd, bd, layer, final):
    n, d = h.shape
    m = n * TOP_K
    nblk = m // MOE_BM
    n_items = nblk + N_EXPERTS - 1
    counts = counts.astype(i32)
    ends = jnp.cumsum(counts)
    starts = ends - counts
    first_blk = starts // MOE_BM
    last_blk = jnp.maximum(ends - 1, starts) // MOE_BM
    per = jnp.where(counts > 0, last_blk - first_blk + 1, 0)
    cum = jnp.cumsum(per)
    off = cum - per
    total = cum[-1]
    it = jnp.minimum(jnp.arange(n_items, dtype=i32), jnp.maximum(total - 1, 0))
    e_it = jnp.minimum(jnp.sum((cum[None, :] <= it[:, None]).astype(i32), axis=1), N_EXPERTS - 1)
    eids = jnp.arange(N_EXPERTS, dtype=i32)[None, :]
    pick = lambda v: jnp.sum(jnp.where(e_it[:, None] == eids, v[None, :], 0), axis=1)
    blk_it = pick(first_blk) + it - pick(off)
    lo_it = jnp.maximum(pick(starts) - blk_it * MOE_BM, 0)
    hi_it = jnp.minimum(pick(ends) - blk_it * MOE_BM, MOE_BM)
    half = MOE_BM // 2
    mode = jnp.where(hi_it <= half, 2, jnp.where(lo_it >= half, 3, 1))
    active = jnp.where(jnp.arange(n_items, dtype=i32) < total, mode, 0).astype(i32)
    changed = jnp.concatenate([jnp.ones((1,), i32), (e_it[1:] != e_it[:-1]).astype(i32)])
    later = (eids > eids.T) & (counts[None, :] > 0)
    nxt_e = jnp.min(jnp.where(later, eids, N_EXPERTS), axis=1)
    nxt_it = pick(jnp.where(nxt_e < N_EXPERTS, nxt_e, -1))
    sched = (blk_it.astype(i32), e_it.astype(i32), lo_it.astype(i32), active, changed, nxt_it.astype(i32))
    ids = meta[:, 0:TOP_K].astype(i32)
    ranks = meta[:, TOP_K:2 * TOP_K].astype(i32)
    dest = (jnp.sum(jnp.where(ids[:, :, None] == eids[None], starts[None, None, :], 0), axis=-1) + ranks).reshape(-1)
    dest = dest * TOKEN_ROWS
    tm = MOE_MOVE_TOKENS if n % MOE_MOVE_TOKENS == 0 else 256
    buf = _dispatch(dest, t, tm)
    eo = _experts(sched, buf, wgu, bgu, wd, bd, layer)
    return _combine(dest, eo, meta, h, mod, fg, tm, final)


def _rope_tables(n):
    rows = n // GRID_W
    row = np.repeat(np.arange(rows), GRID_W)
    col = np.tile(np.arange(GRID_W), rows)
    pos = np.stack([row, col], axis=-1).astype(np.float64)
    inv_freq = ROPE_THETA ** (-np.arange(ROPE_PAIRS, dtype=np.float64) / ROPE_PAIRS)
    ang = pos[:, :, None] * inv_freq
    c, s = np.cos(ang), np.sin(ang)
    cos = np.concatenate([c[:, 0], c[:, 0], c[:, 1], c[:, 1]], axis=-1).astype(np.float32)
    sin = np.concatenate([-s[:, 0], s[:, 0], -s[:, 1], s[:, 1]], axis=-1).astype(np.float32)
    return jnp.asarray(cos), jnp.asarray(sin)


def _router_params(router_w, router_b):
    d = router_w.shape[0]
    rw = jnp.zeros((d, LANES), f32).at[:, :N_EXPERTS].set(router_w)
    rb = jnp.zeros((1, LANES), f32).at[0, :N_EXPERTS].set(router_b)
    rw_hi = rw.astype(bf16)
    rw_lo = (rw - rw_hi.astype(f32)).astype(bf16)
    return jnp.stack([rw_hi, rw_lo]), rb


def _even_layer(x, ctx, mod, g1, g2, w_in, b_in, q_gain, k_gain, f_bias, h_gain, w_out, b_out, router_w, router_b):
    n, d = x.shape
    nctx = ctx.shape[0]
    cols = lambda a, b: w_in[:, a:b]
    wm = jnp.concatenate([cols(0, 1280), cols(1536, 2048), cols(2064, 2576)], axis=1).astype(bf16)
    bm = jnp.concatenate([b_in[0:1280], b_in[1536:2048], b_in[2064:2576]])[None, :]
    wkT = cols(1280, 1536).T.astype(bf16)
    bkT = b_in[1280:1536][:, None]
    wgate = cols(2048, 2064)
    wg = jnp.zeros((d, LANES), f32).at[:, :16].set(wgate).astype(bf16)
    bg = jnp.zeros((1, LANES), f32).at[0, :16].set(b_in[2048:2064])
    wgT = wgate.T.astype(bf16)
    bgT = b_in[2048:2064][:, None]
    wts = (wm, bm, wkT, bkT, wg, bg, wgT, bgT, q_gain[None, :], k_gain[None, :])
    cos, sin = _rope_tables(n)
    lat = _ev_in(x, mod, g1, wts, cos, sin, 0, min(512, n))
    cx = _ev_in(ctx, mod, g1, wts, jnp.ones((nctx, LANES), f32), jnp.zeros((nctx, LANES), f32), 1, nctx)
    q, k, v, mq, mkT, mv, og, gc, gT = lat
    _, kc, vc, mqc, mkTc, mvc, _, gcc, gTc = cx

    v_all = jnp.concatenate([v, vc], axis=0)
    ones = jnp.ones((n + nctx, 128), bf16)
    v_aug = jnp.concatenate([v_all[:, 0:128], ones, v_all[:, 128:256], ones], axis=1)
    att = _attention(q, jnp.concatenate([k, kc], axis=0), v_aug, 1024 if n % 1024 == 0 else min(512, n), 1280)

    fb = f_bias.reshape(-1)
    fbr = jnp.zeros((1, LANES), f32).at[0, 8:16].set(fb)
    fbc = jnp.zeros((16, LANES), f32).at[8:16, :].set(jnp.broadcast_to(fb[:, None], (8, LANES)))
    nh = 2 * MLSTM_HEADS
    c0 = jnp.zeros((nh, MLSTM_DK, 256), f32)
    m0 = jnp.zeros((nh, 1, 128), f32)
    _, _, c1, m1 = _mlstm(mqc, mkTc, mvc, gcc, gTc, fbr, fbc, c0, m0)
    hf, hb, _, _ = _mlstm(mq, mkT, mv, gc, gT, fbr, fbc, c1, m1)

    rw, rb = _router_params(router_w, router_b)
    return _ev_out(att, hf, hb, og, x, mod, h_gain[None, :], w_out.astype(bf16), b_out[None, :], g2, rw, rb,
                   min(512, n))


FFT_N2 = 128
CH_TILE = 8


def _ct_shape(ch, nblocks):
    return jax.ShapeDtypeStruct((ch // CH_TILE, nblocks * CH_TILE, FFT_N2), f32)


def _ct_block(ch, per):
    return pl.BlockSpec((ch // CH_TILE, per * CH_TILE, FFT_N2), lambda i: (0, i, 0))


def _ct_slab(nblocks):
    return pl.BlockSpec((1, nblocks * CH_TILE, FFT_N2), lambda i: (i, 0, 0))


def _ct_channel(ref, c, nblocks):
    return ref.at[0], pl.ds(c, nblocks, stride=CH_TILE)


def _hy_in_body(x_ref, mod_ref, g1_ref, wT_ref, b_ref, zT_ref, *, d):
    a = (_rms(x_ref[...], g1_ref[...]) * (1.0 + mod_ref[0:1, d:2 * d]) + mod_ref[0:1, 0:d]).astype(bf16)
    zT_ref[...] = _dot_nt(wT_ref[...], a) + b_ref[...]


def _hy_in(x, mod, g1, wT, b, tm):
    n, d = x.shape
    c3 = wT.shape[0]
    return pl.pallas_call(
        functools.partial(_hy_in_body, d=d),
        grid=(n // tm,),
        in_specs=[pl.BlockSpec((tm, d), lambda i: (i, 0)), _full(mod.shape), _full(g1.shape), _full(wT.shape),
                  _full(b.shape)],
        out_specs=pl.BlockSpec((c3, tm), lambda i: (0, i)),
        out_shape=jax.ShapeDtypeStruct((c3, n), f32),
        compiler_params=_cp(("arbitrary",)),
        name="hy_in",
    )(x, mod, g1, wT, b)


def _hy_conv_body(z_ref, zp_ref, zn_ref, cw_ref, vp_ref, x0_ref, *, ch):
    i = pl.program_id(0)
    tt = z_ref.shape[1]
    has_prev = (i > 0).astype(f32)
    has_next = (i < pl.num_programs(0) - 1).astype(f32)
    lane = lax.broadcasted_iota(i32, (ch, tt), 1)
    first, last = lane == 0, lane == tt - 1

    def conv(g):
        rows = slice(g * ch, (g + 1) * ch)
        z = z_ref[rows, :]
        pcol = zp_ref[rows, FFT_N2 - 1:FFT_N2] * has_prev
        ncol = zn_ref[rows, 0:1] * has_next
        zm1 = jnp.where(first, pcol, pltpu.roll(z, 1, 1))
        zp1 = jnp.where(last, ncol, pltpu.roll(z, tt - 1, 1))
        cw = cw_ref[rows, :]
        coef = lambda k: jnp.concatenate([cw[:, k * LANES:(k + 1) * LANES]] * (tt // LANES), axis=1)
        return coef(0) * zm1 + coef(1) * z + coef(2) * zp1 + coef(3)

    x0 = conv(0)
    vp = conv(2) * conv(1)
    for j in range(tt // FFT_N2):
        rows = slice(j * CH_TILE, (j + 1) * CH_TILE)
        x0_ref[:, rows, :] = x0[:, j * FFT_N2:(j + 1) * FFT_N2].reshape(ch // CH_TILE, CH_TILE, FFT_N2)
        vp_ref[:, rows, :] = vp[:, j * FFT_N2:(j + 1) * FFT_N2].reshape(ch // CH_TILE, CH_TILE, FFT_N2)


def _hy_conv(zT, cw, tt):
    c3, n = zT.shape
    ch = c3 // 3
    nb = n // FFT_N2
    per = tt // FFT_N2
    return pl.pallas_call(
        functools.partial(_hy_conv_body, ch=ch),
        grid=(n // tt,),
        in_specs=[pl.BlockSpec((c3, tt), lambda i: (0, i)),
                  pl.BlockSpec((c3, FFT_N2), lambda i: (0, jnp.maximum(i * per - 1, 0))),
                  pl.BlockSpec((c3, FFT_N2), lambda i: (0, jnp.minimum(i * per + per, nb - 1))),
                  _full(cw.shape)],
        out_specs=[_ct_block(ch, per), _ct_block(ch, per)],
        out_shape=[_ct_shape(ch, nb), _ct_shape(ch, nb)],
        compiler_params=_cp(("arbitrary",)),
        name="hy_conv",
    )(zT, zT, zT, cw)


def _hy_filt_body(zz_ref, w1_ref, b1_ref, fr_ref, w2_ref, b2_ref, w3_ref, b3_ref, w4T_ref, dl_ref, tr_ref, mr_ref,
                  o_ref):
    hdot = lambda a, b: jnp.dot(a, b, precision=HIGHEST, preferred_element_type=f32)
    fr = fr_ref[...]
    h = jnp.sin(fr * (hdot(w1_ref[...], zz_ref[...]) + b1_ref[...]))
    h = jnp.sin(fr * (hdot(w2_ref[...], h) + b2_ref[...]))
    h = jnp.sin(fr * (hdot(w3_ref[...], h) + b3_ref[...]))
    taps = _dot(w4T_ref[0], h.astype(bf16))
    ch = taps.shape[0]
    dl = dl_ref[...]
    for j in range(tr_ref.shape[0]):
        blk = taps[:, j * FFT_N2:(j + 1) * FFT_N2] * jnp.exp(-dl * tr_ref[j]) * mr_ref[j]
        o_ref[:, j * CH_TILE:(j + 1) * CH_TILE, :] = blk.reshape(ch // CH_TILE, CH_TILE, FFT_N2)


def _hy_filt(zz, w1, b1, fr, w2, b2, w3, b3, w4T, deltas, trow, mrow):
    n2l = zz.shape[1]
    nb2 = n2l // FFT_N2
    ch = w4T.shape[1]
    per = 8 if nb2 % 16 == 0 else (4 if nb2 % 8 == 0 else 1)
    steps = nb2 // per
    return pl.pallas_call(
        _hy_filt_body,
        grid=(steps,),
        in_specs=[pl.BlockSpec((LANES, per * FFT_N2), lambda i: (0, i)), _full(w1.shape), _full(b1.shape),
                  _full(fr.shape), _full(w2.shape), _full(b2.shape), _full(w3.shape), _full(b3.shape),
                  pl.BlockSpec((1,) + w4T.shape[1:], lambda i: ((i >= steps // 2).astype(i32), 0, 0)),
                  _full(deltas.shape),
                  pl.BlockSpec((per, 1, FFT_N2), lambda i: (i, 0, 0)),
                  pl.BlockSpec((per, 1, FFT_N2), lambda i: (i, 0, 0))],
        out_specs=_ct_block(ch, per),
        out_shape=_ct_shape(ch, nb2),
        compiler_params=_cp(("arbitrary",)),
        name="hy_filt",
    )(zz, w1, b1, fr, w2, b2, w3, b3, w4T, deltas, trow, mrow)


def _dft_stage12(x_ref, f1, tr_ref, ti_ref, r2_ref):
    ct = CH_TILE
    n1 = f1.shape[0] // 2
    n_in = f1.shape[1]
    xs = []
    for c in range(ct):
        slab, rows = _ct_channel(x_ref, c, n_in)
        xs.append(slab[rows, :])
    x = jnp.concatenate(xs, axis=1).astype(bf16)
    a = _dot(f1, x)
    tr, ti = tr_ref[...], ti_ref[...]
    lhs = []
    for c in range(ct):
        ar = a[0:n1, c * FFT_N2:(c + 1) * FFT_N2]
        ai = a[n1:2 * n1, c * FFT_N2:(c + 1) * FFT_N2]
        lhs.append(jnp.concatenate([ar * tr - ai * ti, ar * ti + ai * tr], axis=1).astype(bf16))
    xf = _dot(jnp.concatenate(lhs, axis=0), r2_ref[...])
    return [(xf[c * n1:(c + 1) * n1, 0:FFT_N2], xf[c * n1:(c + 1) * n1, FFT_N2:2 * FFT_N2]) for c in range(ct)]


def _hy_fftconv_body(v_ref, t_ref, f1_ref, tr_ref, ti_ref, r2_ref, r2i_ref, i2_ref, y_ref):
    ct = CH_TILE
    n1 = f1_ref.shape[0] // 2
    n_out = i2_ref.shape[0]
    tr, ti = tr_ref[...], ti_ref[...]
    f1 = f1_ref[...]
    g = _dft_stage12(t_ref, f1, tr_ref, ti_ref, r2_ref)
    x = _dft_stage12(v_ref, f1[:, 0:n_out], tr_ref, ti_ref, r2_ref)
    lhs = []
    for (xr, xi), (gr, gi) in zip(x, g):
        lhs.append(jnp.concatenate([xr * gr - xi * gi, xr * gi + xi * gr], axis=1).astype(bf16))
    b = _dot(jnp.concatenate(lhs, axis=0), r2i_ref[...])
    brs, bis = [], []
    for c in range(ct):
        br = b[c * n1:(c + 1) * n1, 0:FFT_N2]
        bi = b[c * n1:(c + 1) * n1, FFT_N2:2 * FFT_N2]
        brs.append(br * tr + bi * ti)
        bis.append(bi * tr - br * ti)
    rhs = jnp.concatenate([jnp.concatenate(brs, axis=1), jnp.concatenate(bis, axis=1)], axis=0).astype(bf16)
    y = _dot(i2_ref[...], rhs)
    for c in range(ct):
        slab, rows = _ct_channel(y_ref, c, n_out)
        slab[rows, :] = y[:, c * FFT_N2:(c + 1) * FFT_N2]


def _hy_fftconv(vp3, taps3, f1, tr, ti, r2, r2i, i2):
    vblk = _ct_slab(i2.shape[0])
    tblk = _ct_slab(f1.shape[1])
    return pl.pallas_call(
        _hy_fftconv_body,
        grid=(vp3.shape[0],),
        in_specs=[vblk, tblk, _full(f1.shape), _full(tr.shape), _full(ti.shape), _full(r2.shape),
                  _full(r2i.shape), _full(i2.shape)],
        out_specs=vblk,
        out_shape=jax.ShapeDtypeStruct(vp3.shape, f32),
        compiler_params=_cp(("arbitrary",)),
        name="hy_fftconv",
    )(vp3, taps3, f1, tr, ti, r2, r2i, i2)


def _hy_out_body(y_ref, vp_ref, x0_ref, sk_ref, w_ref, b_ref, x_ref, mod_ref, g2_ref, rw_ref, rb_ref,
                 h_ref, t_ref, meta_ref, cnt_ref, run_sc, *, d):
    sk = sk_ref[...]
    ch = sk.shape[0]

    def block(ref, j):
        return ref[:, j * CH_TILE:(j + 1) * CH_TILE, :].reshape(ch, FFT_N2)

    g = jnp.concatenate([((block(y_ref, j) + block(vp_ref, j) * sk) * block(x0_ref, j)).astype(bf16)
                         for j in range(y_ref.shape[1] // CH_TILE)], axis=1)
    y = _dot_tn(g, w_ref[...]) + b_ref[...]
    _post_block(x_ref[...], y, mod_ref, g2_ref, rw_ref, rb_ref, run_sc, h_ref, t_ref, meta_ref, cnt_ref, d)


def _hy_out(y3, vp3, x03, skip, w, b, x, mod, g2, rw, rb, tm):
    n, d = x.shape
    blk = _ct_block(skip.shape[0], tm // FFT_N2)
    out_specs, out_shape = _post_specs(n, d, tm)
    return pl.pallas_call(
        functools.partial(_hy_out_body, d=d),
        grid=(n // tm,),
        in_specs=[blk, blk, blk, _full(skip.shape), _full(w.shape), _full(b.shape),
                  pl.BlockSpec((tm, d), lambda i: (i, 0)), _full(mod.shape), _full(g2.shape), _full(rw.shape),
                  _full(rb.shape)],
        out_specs=out_specs, out_shape=out_shape,
        scratch_shapes=[pltpu.VMEM((1, LANES), f32)],
        compiler_params=_cp(("arbitrary",)),
        name="hy_out",
    )(y3, vp3, x03, skip, w, b, x, mod, g2, rw, rb)


def _dft_constants(n):
    nn = 2 * n
    n1 = nn // FFT_N2
    k1 = np.arange(n1)[:, None]
    phi = 2.0 * np.pi * (k1 * np.arange(n1)[None, :] % n1) / n1
    f1 = np.concatenate([np.cos(phi), -np.sin(phi)], axis=0)
    th = 2.0 * np.pi * (k1 * np.arange(FFT_N2)[None, :]) / nn
    tr, ti = np.cos(th), -np.sin(th)
    ps = 2.0 * np.pi * (np.arange(FFT_N2)[:, None] * np.arange(FFT_N2)[None, :] % FFT_N2) / FFT_N2
    c2, s2 = np.cos(ps), np.sin(ps)
    r2 = np.block([[c2, -s2], [s2, c2]])
    r2i = np.block([[c2, s2], [-s2, c2]])
    phi_i = phi[: n1 // 2, :]
    i2 = np.concatenate([np.cos(phi_i), -np.sin(phi_i)], axis=1) / nn
    as32 = lambda a: jnp.asarray(a.astype(np.float32))
    return dict(f1=as32(f1).astype(bf16), tr=as32(tr), ti=as32(ti),
                r2=as32(r2).astype(bf16), r2i=as32(r2i).astype(bf16), i2=as32(i2).astype(bf16))


def _filter_tables(n):
    t = np.linspace(0.0, 1.0, n)[:, None]
    w = (2.0 * np.pi / n) * np.arange(n)[:, None]
    f = np.linspace(1e-4, FILTER_BANDS - 1, FILTER_BANDS)
    z = np.concatenate([t, np.cos(f * w), -np.sin(f * w)], axis=-1)
    pos2 = np.arange(2 * n)
    pos = np.where(pos2 < n, pos2, np.minimum(2 * n - pos2, n - 1))
    zz = np.zeros((LANES, 2 * n), np.float32)
    zz[:FILTER_EMB, :] = z[pos].T
    trow = t[pos, 0].astype(np.float32).reshape(2 * n // FFT_N2, 1, FFT_N2)
    mrow = (pos2 != n).astype(np.float32).reshape(2 * n // FFT_N2, 1, FFT_N2)
    return jnp.asarray(zz), jnp.asarray(trow), jnp.asarray(mrow)


def _odd_layer(x, mod, g1, g2, w_in, b_in, conv_w, conv_b, fw1, fb1, ffreq, fw2, fb2, fw3, fb3, fw4, skip, w_out,
               b_out, router_w, router_b):
    n, d = x.shape
    ch = w_out.shape[0]
    tm = min(512, n)
    zT = _hy_in(x, mod, g1, w_in.T.astype(bf16), b_in[:, None], tm)
    cw = jnp.repeat(jnp.concatenate([conv_w.T, conv_b[:, None]], axis=1), LANES, axis=1)
    vp3, x03 = _hy_conv(zT, cw, tm)

    zz, trow, mrow = _filter_tables(n)
    w1T = jnp.zeros((fw1.shape[1], LANES), f32).at[:, :FILTER_EMB].set(fw1.T)
    w4T = jnp.stack([fw4[:, :ch].T, fw4[:, ch:].T]).astype(bf16)
    deltas = jnp.abs(jnp.linspace(math.log(DECAY_TARGET) / SLOW_DECAY, math.log(DECAY_TARGET) / FAST_DECAY, ch,
                                  dtype=f32))[:, None]
    taps3 = _hy_filt(zz, w1T, fb1[:, None], ffreq[:, None], fw2.T, fb2[:, None], fw3.T, fb3[:, None], w4T, deltas,
                     trow, mrow)
    k = _dft_constants(n)
    y3 = _hy_fftconv(vp3, taps3, k["f1"], k["tr"], k["ti"], k["r2"], k["r2i"], k["i2"])
    rw, rb = _router_params(router_w, router_b)
    return _hy_out(y3, vp3, x03, skip[:, None], w_out.astype(bf16), b_out[None, :], x, mod, g2, rw, rb, tm)


def kernel(x, c, ctx, c_ctx, ada_w, ada_b, norm1_g, norm2_g, ev_w_in, ev_b_in, ev_q_gain, ev_k_gain, ev_f_bias,
           ev_h_gain, ev_w_out, ev_b_out, od_w_in, od_b_in, od_conv_w, od_conv_b, od_filt_w1, od_filt_b1,
           od_filt_freq, od_filt_w2, od_filt_b2, od_filt_w3, od_filt_b3, od_filt_w4, od_skip, od_w_out, od_b_out,
           router_w, router_b, moe_w_gu, moe_b_gu, moe_w_down, moe_b_down, final_g):
    bsz, n, d = x.shape
    assert bsz == 1 and c.shape[0] == 1
    depth = ada_w.shape[0]
    assert depth in (1, 2)
    cv = jnp.zeros((8, d), f32).at[0].set(c[0]).at[1].set(c_ctx)
    mods = _mods(cv, ada_w, ada_b)
    fg = final_g[None, :]

    h, t, meta, cnt = _even_layer(
        x[0], ctx[0], mods[0], norm1_g[0][None, :], norm2_g[0][None, :], ev_w_in[0], ev_b_in[0], ev_q_gain[0],
        ev_k_gain[0], ev_f_bias[0], ev_h_gain[0], ev_w_out[0], ev_b_out[0], router_w[0], router_b[0])
    h = _moe(t, meta, cnt[0, :N_EXPERTS], h, mods[0], fg, moe_w_gu, moe_b_gu, moe_w_down, moe_b_down, 0,
             final=(depth == 1))
    if depth == 2:
        h, t, meta, cnt = _odd_layer(
            h, mods[1], norm1_g[1][None, :], norm2_g[1][None, :], od_w_in[0], od_b_in[0], od_conv_w[0], od_conv_b[0],
            od_filt_w1[0], od_filt_b1[0], od_filt_freq[0], od_filt_w2[0], od_filt_b2[0], od_filt_w3[0],
            od_filt_b3[0], od_filt_w4[0], od_skip[0], od_w_out[0], od_b_out[0], router_w[1], router_b[1])
        h = _moe(t, meta, cnt[0, :N_EXPERTS], h, mods[1], fg, moe_w_gu, moe_b_gu, moe_w_down, moe_b_down, 1,
                 final=True)
    return h[None]
```
